```python
import jax, jax.numpy as jnp
from jax import lax
import numpy as np

D_MODEL = 1024
BATCH = 8
SEQ = 2048
DEPTH = 2
DEC_BATCH = 128
DEC_SEQ = 1
PAST_LEN = 2048
PAGE_SIZE = 128

HEAD_DIM = 64
NSA_HEADS = 4
NSA_CMP_LEN = 32
NSA_CMP_STRIDE = 16
NSA_CMP_HID = 128
NSA_SLC_LEN = 64
NSA_TOPN = 8
NSA_WINDOW = 512
DSA_HEADS = 4
IDX_HEADS = 4
IDX_DIM = HEAD_DIM
DSA_TOPK = 256
GMLP_GROUPS = 4
GMLP_WIDTH = GMLP_GROUPS * HEAD_DIM
CHUNK = 128
RWKV_HEADS = 4
RWKV_WIDTH = RWKV_HEADS * HEAD_DIM
RWKV_DECAY_LORA = 64
RWKV_A_LORA = 64
RWKV_GATE_LORA = 128
RWKV_COL_WIDTHS = (RWKV_WIDTH, RWKV_WIDTH, RWKV_WIDTH, RWKV_DECAY_LORA, RWKV_A_LORA, RWKV_GATE_LORA)
RWKV_PROJ = sum(RWKV_COL_WIDTHS)
RWKV_GN_EPS = 64e-5
N_BRANCH = 4
BRANCH_WIDTH = 256
FF_RAW = -(-8 * D_MODEL // 3)
D_FF = -(-FF_RAW // 256) * 256
COL_NAMES = ('q_nsa', 'kv_nsa', 'g_nsa', 'q_dsa', 'kv_dsa', 'q_idx', 'k_idx', 'w_idx', 'gmlp', 'rwkv', 'merge')
COL_WIDTHS = (NSA_HEADS * HEAD_DIM, 6 * HEAD_DIM, 3 * NSA_HEADS, DSA_HEADS * HEAD_DIM, 2 * HEAD_DIM,
              IDX_HEADS * IDX_DIM, IDX_DIM, IDX_HEADS, 2 * GMLP_WIDTH, RWKV_PROJ, N_BRANCH * D_MODEL)
IN_COLS = sum(COL_WIDTHS)
QB = 128
EPS = 1e-6
LN_EPS = 1e-5
NEG = -1e30
FORCE_BONUS = 1e4

kernel_name = 'hybrid_nsa_dsa_gmlp_rwkv7_decode_step'


def rms_norm(x, g):
    xf = x.astype(jnp.float32)
    y = xf * lax.rsqrt(jnp.mean(xf * xf, axis=-1, keepdims=True) + EPS)
    return (y * g).astype(x.dtype)


def layer_norm(x, g, b):
    xf = x.astype(jnp.float32)
    mu = jnp.mean(xf, axis=-1, keepdims=True)
    var = jnp.mean(jnp.square(xf - mu), axis=-1, keepdims=True)
    return ((xf - mu) * lax.rsqrt(var + LN_EPS) * g + b).astype(x.dtype)


def masked_softmax(s, mask):
    s = jnp.where(mask, s.astype(jnp.float32), NEG)
    return jnp.where(mask, jax.nn.softmax(s, axis=-1), 0.0)


def alibi_slopes(n):
    return jnp.asarray(2.0 ** (-8.0 * np.arange(1, n + 1) / n), jnp.float32)


def split_cols(z):
    parts = jnp.split(z, np.cumsum(COL_WIDTHS)[:-1].tolist(), axis=-1)
    return dict(zip(COL_NAMES, parts))


def cmp_to_slc(n_cmp, n_slc):
    start = np.arange(n_cmp) * NSA_CMP_STRIDE
    bstart = np.arange(n_slc) * NSA_SLC_LEN
    ov = np.minimum(start[:, None] + NSA_CMP_LEN, bstart[None, :] + NSA_SLC_LEN) - np.maximum(start[:, None], bstart[None, :])
    return jnp.asarray(np.clip(ov, 0, None) / NSA_CMP_LEN, jnp.float32)


def attn_inputs(cols):
    b, t, _ = cols['q_nsa'].shape
    kv_n = cols['kv_nsa'].reshape(b, t, 6, HEAD_DIM)
    return {
        'q_n': cols['q_nsa'].reshape(b, t, NSA_HEADS, HEAD_DIM),
        'g_n': cols['g_nsa'].reshape(b, t, NSA_HEADS, 3),
        'rows_n': kv_n[:, :, :4],
        'win_n': kv_n[:, :, 4:],
        'q_d': cols['q_dsa'].reshape(b, t, DSA_HEADS, HEAD_DIM),
        'rows_d': jnp.concatenate([cols['kv_dsa'].reshape(b, t, 2, HEAD_DIM), cols['k_idx'][:, :, None]], axis=2),
        'qi': cols['q_idx'].reshape(b, t, IDX_HEADS, IDX_DIM),
        'wi': cols['w_idx'],
    }


def nsa_keyside(rows, w1, w2, pos):
    b, l = rows.shape[0], rows.shape[1]
    n_cmp = (l - NSA_CMP_LEN) // NSA_CMP_STRIDE + 1
    idx = np.arange(n_cmp)[:, None] * NSA_CMP_STRIDE + np.arange(NSA_CMP_LEN)[None, :]
    kv = jnp.moveaxis(rows[:, :, 0:2], 2, 0)
    blocks = kv[:, :, idx] + pos[:, None, None]
    flat = blocks.reshape(2, b, n_cmp, NSA_CMP_LEN * HEAD_DIM)
    comp = jnp.einsum('zbnh,zhd->zbnd', jax.nn.gelu(jnp.einsum('zbnf,zfh->zbnh', flat, w1)), w2)
    n_slc = -(-l // NSA_SLC_LEN)
    slc = jnp.pad(rows[:, :, 2:4], ((0, 0), (0, n_slc * NSA_SLC_LEN - l), (0, 0), (0, 0)))
    slc = slc.reshape(b, n_slc, NSA_SLC_LEN, 2, HEAD_DIM)
    return comp[0], comp[1], idx[:, -1], slc[..., 0, :], slc[..., 1, :]


def nsa_attend(q, q_pos, gate_logits, keyside, k_win, v_win, win_pos):
    k_cmp, v_cmp, cmp_end, k_slc, v_slc = keyside
    b, qb = q.shape[0], q.shape[1]
    n_cmp, n_slc = k_cmp.shape[1], k_slc.shape[1]
    scale = HEAD_DIM ** -0.5
    slopes = alibi_slopes(NSA_HEADS)
    d_cmp = (q_pos[:, None] - cmp_end[None, :]).astype(jnp.float32)
    s = jnp.einsum('bqhd,bnd->bqhn', q, k_cmp).astype(jnp.float32) * scale - slopes[:, None] * d_cmp[:, None, :]
    p_cmp = masked_softmax(s, (d_cmp >= 0)[None, :, None, :])
    o_cmp = jnp.einsum('bqhn,bnd->bqhd', p_cmp.astype(v_cmp.dtype), v_cmp)
    imp = jnp.einsum('bqn,nm->bqm', p_cmp.sum(axis=2), cmp_to_slc(n_cmp, n_slc))
    blk = jnp.arange(n_slc)[None, :]
    cur = (q_pos // NSA_SLC_LEN)[:, None]
    adm = blk * NSA_SLC_LEN <= q_pos[:, None]
    forced = (blk == 0) | (blk == cur) | (blk == cur - 1)
    score = jnp.where(adm, imp + jnp.where(forced, FORCE_BONUS, 0.0), NEG)
    top_v, top_i = lax.top_k(score, min(NSA_TOPN, n_slc))
    valid = top_v > 0.5 * NEG
    kg = jax.vmap(lambda a, i: a[i])(k_slc, top_i)
    vg = jax.vmap(lambda a, i: a[i])(v_slc, top_i)
    kpos = top_i[..., None] * NSA_SLC_LEN + jnp.arange(NSA_SLC_LEN)
    d_slc = (q_pos[None, :, None, None] - kpos).astype(jnp.float32)
    mask = (valid[..., None] & (d_slc >= 0)).reshape(b, qb, 1, -1)
    s = jnp.einsum('bqhd,bqnsd->bqhns', q, kg).astype(jnp.float32) * scale - slopes.reshape(1, 1, -1, 1, 1) * d_slc[:, :, None]
    p = masked_softmax(s.reshape(b, qb, NSA_HEADS, -1), mask).reshape(s.shape)
    o_slc = jnp.einsum('bqhns,bqnsd->bqhd', p.astype(vg.dtype), vg)
    d_win = q_pos[:, None] - win_pos[None, :]
    mask_w = ((d_win >= 0) & (d_win <= NSA_WINDOW) & (win_pos[None, :] >= 0))[None, :, None, :]
    s = jnp.einsum('bqhd,bld->bqhl', q, k_win).astype(jnp.float32) * scale - slopes[:, None] * d_win.astype(jnp.float32)[:, None, :]
    o_win = jnp.einsum('bqhl,bld->bqhd', masked_softmax(s, mask_w).astype(v_win.dtype), v_win)
    g = jax.nn.sigmoid(gate_logits.astype(jnp.float32)).astype(q.dtype)
    return g[..., 0:1] * o_cmp + g[..., 1:2] * o_slc + g[..., 2:3] * o_win


def dsa_attend(q, q_pos, q_idx, w_idx, rows):
    l = rows.shape[1]
    logits = jnp.einsum('bqhd,bld->bqhl', q_idx, rows[:, :, 2]).astype(jnp.float32) * IDX_DIM ** -0.5
    index = jnp.einsum('bqh,bqhl->bql', w_idx.astype(jnp.float32) * IDX_HEADS ** -0.5, jax.nn.relu(logits))
    causal = jnp.arange(l)[None, :] <= q_pos[:, None]
    top_v, top_i = lax.top_k(jnp.where(causal[None], index, NEG), min(DSA_TOPK, l // 4))
    valid = top_v > 0.5 * NEG
    kv = jax.vmap(lambda a, i: a[i])(rows[:, :, 0:2], top_i)
    dist = (q_pos[None, :, None] - top_i).astype(jnp.float32)
    slopes = alibi_slopes(DSA_HEADS)
    s = jnp.einsum('bqhd,bqnd->bqhn', q, kv[..., 0, :]).astype(jnp.float32) * HEAD_DIM ** -0.5 - slopes[None, None, :, None] * dist[:, :, None, :]
    p = masked_softmax(s, valid[:, :, None, :])
    return jnp.einsum('bqhn,bqnd->bqhd', p.astype(kv.dtype), kv[..., 1, :])


def gmlp_mix(z, ln_g, ln_b, w_s, b_s):
    b, t, _ = z.shape
    u, v = jnp.split(jax.nn.gelu(z), 2, axis=-1)
    v = layer_norm(v, ln_g, ln_b)
    c = min(CHUNK, t)
    vc = v.reshape(b, t // c, c, GMLP_GROUPS, GMLP_WIDTH // GMLP_GROUPS)
    w = w_s[:, :c, :c] * jnp.tril(jnp.ones((c, c), w_s.dtype))
    s = jnp.einsum('gij,bnjgd->bnigd', w, vc) + b_s[:, :c].T[None, None, :, :, None]
    return u * s.reshape(b, t, GMLP_WIDTH), v


def rwkv_mix(feat, shift_prev, wkv0, lp):
    b, t, _ = feat.shape
    f32 = jnp.float32
    prev = jnp.concatenate([shift_prev[:, None].astype(feat.dtype), feat[:, :-1]], axis=1)
    f = feat + lp['rwkv_mu'] * (prev - feat)
    r, k, v, wl, al, gl = jnp.split(f, np.cumsum(RWKV_COL_WIDTHS)[:-1].tolist(), axis=-1)
    w_log = -jax.nn.softplus(-(lp['rwkv_w0'] + jnp.tanh(wl) @ lp['rwkv_w2']).astype(f32)) - 0.5
    decay = jnp.exp(-jnp.exp(w_log))
    a = jax.nn.sigmoid((lp['rwkv_a0'] + al @ lp['rwkv_a2']).astype(f32))
    g = jax.nn.sigmoid(gl) @ lp['rwkv_g2']
    hd = lambda z: z.reshape(b, t, RWKV_HEADS, HEAD_DIM).astype(f32)
    r, k, v, decay, a = hd(r), hd(k), hd(v), hd(decay), hd(a)
    kk = k * lp['rwkv_kk'].reshape(RWKV_HEADS, HEAD_DIM)
    kk = kk * lax.rsqrt(jnp.sum(kk * kk, axis=-1, keepdims=True) + 1e-12)
    k = k * (1.0 + (a - 1.0) * lp['rwkv_ka'].reshape(RWKV_HEADS, HEAD_DIM))

    def step(S, inp):
        r_t, w_t, k_t, v_t, kk_t, a_t = inp
        sa = jnp.einsum('bhvk,bhk->bhv', S, -kk_t)
        S = S * w_t[:, :, None, :] + sa[..., None] * (kk_t * a_t)[:, :, None, :] + v_t[..., None] * k_t[:, :, None, :]
        return S, jnp.einsum('bhvk,bhk->bhv', S, r_t)

    xs = tuple(jnp.moveaxis(z, 1, 0) for z in (r, decay, k, v, kk, a))
    S, y = lax.scan(step, wkv0.astype(f32), xs)
    y = jnp.moveaxis(y, 0, 1)
    mu = jnp.mean(y, axis=-1, keepdims=True)
    var = jnp.mean(jnp.square(y - mu), axis=-1, keepdims=True)
    y = ((y - mu) * lax.rsqrt(var + RWKV_GN_EPS)).reshape(b, t, RWKV_WIDTH) * lp['rwkv_ln_w'] + lp['rwkv_ln_b']
    bonus = jnp.sum(r * k * lp['rwkv_rk'], axis=-1, keepdims=True) * v
    y = (y + bonus.reshape(b, t, RWKV_WIDTH)) * g
    return y.astype(feat.dtype), feat[:, -1], S.astype(wkv0.dtype)


def prompt_mixers(cols, lp):
    b, t, _ = cols['q_nsa'].shape
    ai = attn_inputs(cols)
    keyside = nsa_keyside(ai['rows_n'], lp['nsa_cmp_w1'], lp['nsa_cmp_w2'], lp['nsa_cmp_pos'])
    win_pad = jnp.pad(ai['win_n'], ((0, 0), (NSA_WINDOW, 0), (0, 0), (0, 0)))

    def block(bi):
        q0 = bi * QB
        q_pos = q0 + jnp.arange(QB)
        sl = lambda z: lax.dynamic_slice_in_dim(z, q0, QB, axis=1)
        wkv = lax.dynamic_slice_in_dim(win_pad, q0, NSA_WINDOW + QB, axis=1)
        w_pos = q0 - NSA_WINDOW + jnp.arange(NSA_WINDOW + QB)
        o_n = nsa_attend(sl(ai['q_n']), q_pos, sl(ai['g_n']), keyside, wkv[:, :, 0], wkv[:, :, 1], w_pos)
        o_d = dsa_attend(sl(ai['q_d']), q_pos, sl(ai['qi']), sl(ai['wi']), ai['rows_d'])
        return o_n, o_d

    o_n, o_d = lax.map(block, jnp.arange(t // QB))
    unblock = lambda o: jnp.moveaxis(o, 0, 1).reshape(b, t, -1)
    o_c, _ = gmlp_mix(cols['gmlp'], lp['gmlp_ln_g'], lp['gmlp_ln_b'], lp['gmlp_ws'], lp['gmlp_bs'])
    shift0 = jnp.zeros((b, RWKV_PROJ), cols['rwkv'].dtype)
    wkv0 = jnp.zeros((b, RWKV_HEADS, HEAD_DIM, HEAD_DIM), cols['rwkv'].dtype)
    o_r, shift, wkv = rwkv_mix(cols['rwkv'], shift0, wkv0, lp)
    state = (ai['rows_n'], ai['rows_d'], ai['win_n'][:, -min(NSA_WINDOW, t):], shift, wkv)
    return [unblock(o_n), unblock(o_d), o_c, o_r], state


def sample_mixers(cols, lp, layer, cache_nsa, cache_dsa, win_buf, shift_prev, wkv_prev, page_table):
    b, t, _ = cols['q_nsa'].shape
    ai = attn_inputs(cols)
    past_len = page_table.shape[1] * cache_nsa.shape[2]
    gather = lambda cache: cache[layer, page_table].reshape(b, past_len, *cache.shape[3:])
    full_n = jnp.concatenate([gather(cache_nsa), ai['rows_n']], axis=1)
    full_d = jnp.concatenate([gather(cache_dsa), ai['rows_d']], axis=1)
    keyside = nsa_keyside(full_n, lp['nsa_cmp_w1'], lp['nsa_cmp_w2'], lp['nsa_cmp_pos'])
    q_pos = past_len + jnp.arange(t)
    wb = win_buf.shape[1]
    win_all = jnp.concatenate([win_buf, ai['win_n']], axis=1)
    w_pos = past_len - wb + jnp.arange(wb + t)
    o_n = nsa_attend(ai['q_n'], q_pos, ai['g_n'], keyside, win_all[:, :, 0], win_all[:, :, 1], w_pos)
    o_d = dsa_attend(ai['q_d'], q_pos, ai['qi'], ai['wi'], full_d)
    o_c, v_c = gmlp_mix(cols['gmlp'], lp['gmlp_ln_g'], lp['gmlp_ln_b'], lp['gmlp_ws'], lp['gmlp_bs'])
    o_r, shift, wkv = rwkv_mix(cols['rwkv'], shift_prev, wkv_prev, lp)
    state = (ai['rows_n'], ai['rows_d'], win_all[:, -wb:], v_c, shift, wkv)
    return [o_n.reshape(b, t, -1), o_d.reshape(b, t, -1), o_c, o_r], state


def apply_layer(x, c, lp, mixers):
    b = x.shape[0]
    mod = (jax.nn.silu(c) @ lp['w_ada'] + lp['b_ada']).reshape(b, 6, D_MODEL)
    h = rms_norm(x, lp['norm_mix_g']) * (1.0 + mod[:, None, 1]) + mod[:, None, 0]
    cols = split_cols(h @ lp['w_in'])
    outs, state = mixers(cols)
    branches = jnp.einsum('nbtc,ncd->nbtd', jnp.stack(outs), lp['w_branch'])
    gates = jax.nn.sigmoid(cols['merge'].reshape(b, -1, N_BRANCH, D_MODEL))
    mixed = jnp.einsum('btnd,nbtd->btd', gates, branches) @ lp['w_out']
    x = x + mod[:, None, 2] * mixed
    h2 = rms_norm(x, lp['norm_ffn_g']) * (1.0 + mod[:, None, 4]) + mod[:, None, 3]
    gate, up = jnp.split(h2 @ lp['w_ffn_in'], 2, axis=-1)
    x = x + mod[:, None, 5] * ((jax.nn.silu(gate) * up) @ lp['w_ffn_out'])
    return x, state


def setup_inputs(seed: int = 0) -> dict:
    key = jax.random.key(seed)
    ks = iter(jax.random.split(key, 40))
    nrm = lambda shape, s=1.0: jax.random.normal(next(ks), shape, jnp.float32) * s
    n_pages = PAST_LEN // PAGE_SIZE
    n_phys = (5 * DEC_BATCH * n_pages + 3) // 4
    win_buf = min(NSA_WINDOW, PAST_LEN)
    L = DEPTH
    page_table = jax.random.permutation(next(ks), n_phys)[:DEC_BATCH * n_pages].reshape(DEC_BATCH, n_pages).astype(jnp.int32)
    return {
        'x_prompt': nrm((BATCH, SEQ, D_MODEL)),
        'x_sample': nrm((DEC_BATCH, DEC_SEQ, D_MODEL)),
        'cache_nsa': nrm((L, n_phys, PAGE_SIZE, 4, HEAD_DIM)),
        'cache_dsa': nrm((L, n_phys, PAGE_SIZE, 3, HEAD_DIM)),
        'state_nsa_win': nrm((L, DEC_BATCH, win_buf, 2, HEAD_DIM)),
        'state_rwkv_shift': nrm((L, DEC_BATCH, RWKV_PROJ)),
        'state_rwkv_wkv': nrm((L, DEC_BATCH, RWKV_HEADS, HEAD_DIM, HEAD_DIM), 0.5),
        'page_table': page_table,
        'c_prompt': nrm((BATCH, D_MODEL)),
        'c_sample': nrm((DEC_BATCH, D_MODEL)),
        'w_ada': nrm((L, D_MODEL, 6 * D_MODEL), 0.5 * D_MODEL ** -0.5),
        'b_ada': nrm((L, 6 * D_MODEL), 0.01),
        'norm_mix_g': 1.0 + nrm((L, D_MODEL), 0.01),
        'norm_ffn_g': 1.0 + nrm((L, D_MODEL), 0.01),
        'w_in': nrm((L, D_MODEL, IN_COLS), D_MODEL ** -0.5),
        'nsa_cmp_w1': nrm((L, 2, NSA_CMP_LEN * HEAD_DIM, NSA_CMP_HID), (NSA_CMP_LEN * HEAD_DIM) ** -0.5),
        'nsa_cmp_w2': nrm((L, 2, NSA_CMP_HID, HEAD_DIM), NSA_CMP_HID ** -0.5),
        'nsa_cmp_pos': nrm((L, 2, NSA_CMP_LEN, HEAD_DIM), 0.5),
        'gmlp_ln_g': 1.0 + nrm((L, GMLP_WIDTH), 0.01),
        'gmlp_ln_b': nrm((L, GMLP_WIDTH), 0.01),
        'gmlp_ws': nrm((L, GMLP_GROUPS, CHUNK, CHUNK), 0.5 * CHUNK ** -0.5),
        'gmlp_bs': 1.0 + nrm((L, GMLP_GROUPS, CHUNK), 0.01),
        'rwkv_mu': jax.random.uniform(next(ks), (L, RWKV_PROJ), jnp.float32),
        'rwkv_w0': nrm((L, RWKV_WIDTH), 0.5),
        'rwkv_w2': nrm((L, RWKV_DECAY_LORA, RWKV_WIDTH), 0.1),
        'rwkv_a0': nrm((L, RWKV_WIDTH), 0.1),
        'rwkv_a2': nrm((L, RWKV_A_LORA, RWKV_WIDTH), 0.1),
        'rwkv_g2': nrm((L, RWKV_GATE_LORA, RWKV_WIDTH), RWKV_GATE_LORA ** -0.5),
        'rwkv_kk': 1.0 + nrm((L, RWKV_WIDTH), 0.1),
        'rwkv_ka': 1.0 + nrm((L, RWKV_WIDTH), 0.1),
        'rwkv_rk': nrm((L, RWKV_HEADS, HEAD_DIM), 0.1),
        'rwkv_ln_w': 1.0 + nrm((L, RWKV_WIDTH), 0.01),
        'rwkv_ln_b': nrm((L, RWKV_WIDTH), 0.01),
        'w_branch': nrm((L, N_BRANCH, BRANCH_WIDTH, D_MODEL), BRANCH_WIDTH ** -0.5),
        'w_out': nrm((L, D_MODEL, D_MODEL), D_MODEL ** -0.5),
        'w_ffn_in': nrm((L, D_MODEL, 2 * D_FF), D_MODEL ** -0.5),
        'w_ffn_out': nrm((L, D_FF, D_MODEL), D_FF ** -0.5),
        'final_norm_g': 1.0 + nrm((D_MODEL,), 0.01),
    }


def reference(x_prompt, x_sample, cache_nsa, cache_dsa, state_nsa_win, state_rwkv_shift, state_rwkv_wkv,
              page_table, c_prompt, c_sample, w_ada, b_ada, norm_mix_g, norm_ffn_g, w_in,
              nsa_cmp_w1, nsa_cmp_w2, nsa_cmp_pos, gmlp_ln_g, gmlp_ln_b, gmlp_ws, gmlp_bs,
              rwkv_mu, rwkv_w0, rwkv_w2, rwkv_a0, rwkv_a2, rwkv_g2, rwkv_kk, rwkv_ka, rwkv_rk,
              rwkv_ln_w, rwkv_ln_b, w_branch, w_out, w_ffn_in, w_ffn_out, final_norm_g):
    xp, xs = x_prompt, x_sample
    new_p, new_s = [], []
    for l in range(DEPTH):
        lp = {'w_ada': w_ada[l], 'b_ada': b_ada[l], 'norm_mix_g': norm_mix_g[l], 'norm_ffn_g': norm_ffn_g[l],
              'w_in': w_in[l], 'nsa_cmp_w1': nsa_cmp_w1[l], 'nsa_cmp_w2': nsa_cmp_w2[l], 'nsa_cmp_pos': nsa_cmp_pos[l],
              'gmlp_ln_g': gmlp_ln_g[l], 'gmlp_ln_b': gmlp_ln_b[l], 'gmlp_ws': gmlp_ws[l], 'gmlp_bs': gmlp_bs[l],
              'rwkv_mu': rwkv_mu[l], 'rwkv_w0': rwkv_w0[l], 'rwkv_w2': rwkv_w2[l], 'rwkv_a0': rwkv_a0[l],
              'rwkv_a2': rwkv_a2[l], 'rwkv_g2': rwkv_g2[l], 'rwkv_kk': rwkv_kk[l], 'rwkv_ka': rwkv_ka[l],
              'rwkv_rk': rwkv_rk[l], 'rwkv_ln_w': rwkv_ln_w[l], 'rwkv_ln_b': rwkv_ln_b[l],
              'w_branch': w_branch[l], 'w_out': w_out[l], 'w_ffn_in': w_ffn_in[l], 'w_ffn_out': w_ffn_out[l]}
        xp, st_p = apply_layer(xp, c_prompt, lp, lambda cols, lp=lp: prompt_mixers(cols, lp))
        xs, st_s = apply_layer(xs, c_sample, lp, lambda cols, lp=lp, l=l: sample_mixers(
            cols, lp, l, cache_nsa, cache_dsa, state_nsa_win[l], state_rwkv_shift[l], state_rwkv_wkv[l], page_table))
        new_p.append(st_p)
        new_s.append(st_s)
    y_prompt = rms_norm(xp, final_norm_g)
    y_sample = rms_norm(xs, final_norm_g)
    nsa_rows_prompt = jnp.stack([s[0] for s in new_p])
    nsa_rows_sample = jnp.stack([s[0] for s in new_s])
    dsa_rows_prompt = jnp.stack([s[1] for s in new_p])
    dsa_rows_sample = jnp.stack([s[1] for s in new_s])
    nsa_win_prompt = jnp.stack([s[2] for s in new_p])
    nsa_win_sample = jnp.stack([s[2] for s in new_s])
    gmlp_v_sample = jnp.stack([s[3] for s in new_s])
    rwkv_shift_prompt = jnp.stack([s[3] for s in new_p])
    rwkv_shift_sample = jnp.stack([s[4] for s in new_s])
    rwkv_wkv_prompt = jnp.stack([s[4] for s in new_p])
    rwkv_wkv_sample = jnp.stack([s[5] for s in new_s])
    return (y_prompt, y_sample, nsa_rows_prompt, nsa_rows_sample, dsa_rows_prompt, dsa_rows_sample,
            nsa_win_prompt, nsa_win_sample, gmlp_v_sample, rwkv_shift_prompt, rwkv_shift_sample,
            rwkv_wkv_prompt, rwkv_wkv_sample)
```

```python
import functools

import numpy as np
import jax
import jax.numpy as jnp
from jax import lax
from jax.experimental import pallas as pl
from jax.experimental.pallas import tpu as pltpu

F32 = jnp.float32
BF16 = jnp.bfloat16
I32 = jnp.int32

HEAD_DIM = 64
N_HEADS = 4
NSA_CMP_LEN = 32
NSA_CMP_STRIDE = 16
NSA_SLC_LEN = 64
NSA_TOPN = 8
NSA_WINDOW = 512
DSA_TOPK = 256
GMLP_GROUPS = 4
GMLP_WIDTH = GMLP_GROUPS * HEAD_DIM
CHUNK = 128
RWKV_WIDTH = N_HEADS * HEAD_DIM
RWKV_GN_EPS = 64e-5
QB = 128
EPS = 1e-6
LN_EPS = 1e-5
NEG = -1e30
FORCE_BONUS = 1e4
LANE = 128
VMEM_LIMIT = 56 * 1024 * 1024

C_MERGE, C_RWKV, C_GMLP = 0, 4096, 5120
C_QN, C_QD, C_QI, C_MISC, C_K = 5632, 6144, 6656, 7168, 7296
N_PROJ = 8192
K_PAIRS = 5
P_CMP, P_SLC, P_WIN, P_DSA, P_IDX = range(K_PAIRS)


def _cparams(sem):
    return pltpu.CompilerParams(dimension_semantics=sem, vmem_limit_bytes=VMEM_LIMIT)


def _dot(a, b):
    return jnp.dot(a, b, preferred_element_type=F32)


def _dot_nt(a, b):
    return lax.dot_general(a, b, (((1,), (1,)), ((), ())), preferred_element_type=F32)


def _dot_tn(a, b):
    return lax.dot_general(a, b, (((0,), (0,)), ((), ())), preferred_element_type=F32)


def _gelu_tanh(x):
    return 0.5 * x * (1.0 + jnp.tanh(np.sqrt(2.0 / np.pi).astype(np.float32) * (x + 0.044715 * (x * x * x))))


def _sigmoid(x):
    return 1.0 / (1.0 + jnp.exp(-x))


def _rms_mod(x, g, scale, shift):
    ms = jnp.mean(x * x, axis=-1, keepdims=True)
    return (x * lax.rsqrt(ms + EPS) * g) * (1.0 + scale) + shift


def _ada_kernel(c_ref, w_ref, b_ref, o_ref):
    c = c_ref[...]
    s = (c * _sigmoid(c)).astype(BF16)
    o_ref[...] = _dot(s, w_ref[...].astype(BF16)) + b_ref[...]


def _ada(c, w, b):
    m, d = c.shape
    n = w.shape[1]
    tn = 1536
    return pl.pallas_call(
        _ada_kernel,
        grid=(n // tn,),
        in_specs=[pl.BlockSpec((m, d), lambda j: (0, 0)),
                  pl.BlockSpec((d, tn), lambda j: (0, j)),
                  pl.BlockSpec((1, tn), lambda j: (0, j))],
        out_specs=pl.BlockSpec((m, tn), lambda j: (0, j)),
        out_shape=jax.ShapeDtypeStruct((m, n), F32),
        compiler_params=_cparams(("parallel",)),
        name="ada",
    )(c, w, b.reshape(1, n))


def _inproj_kernel(x_ref, g_ref, sh_ref, sc_ref, w_ref, o_ref):
    h = _rms_mod(x_ref[0], g_ref[...], sc_ref[0], sh_ref[0])
    o_ref[0] = _dot(h.astype(BF16), w_ref[...])


def _inproj(x, g, shift, scale, w, tm, tn):
    b, t, d = x.shape
    n = w.shape[1]
    tmod = tm if shift.shape[1] == t else 1
    mod_map = (lambda j, bi, i: (bi, i, 0)) if shift.shape[1] == t else (lambda j, bi, i: (bi, 0, 0))
    return pl.pallas_call(
        _inproj_kernel,
        grid=(n // tn, b, t // tm),
        in_specs=[pl.BlockSpec((1, tm, d), lambda j, bi, i: (bi, i, 0)),
                  pl.BlockSpec((1, d), lambda j, bi, i: (0, 0)),
                  pl.BlockSpec((1, tmod, d), mod_map),
                  pl.BlockSpec((1, tmod, d), mod_map),
                  pl.BlockSpec((d, tn), lambda j, bi, i: (0, j))],
        out_specs=pl.BlockSpec((1, tm, tn), lambda j, bi, i: (bi, i, j)),
        out_shape=jax.ShapeDtypeStruct((b, t, n), F32),
        compiler_params=_cparams(("parallel", "parallel", "parallel")),
        name="inproj",
    )(x, g.reshape(1, d), shift, scale, w)


def _merge_kernel(zm_ref, oa_ref, oc_ref, or_ref, wb_ref, wo_ref, x_ref, gate_ref, o_ref):
    bw = wb_ref.shape[1]
    d = x_ref.shape[-1]
    outs = (oa_ref[0, :, 0:bw], oa_ref[0, :, bw:2 * bw], oc_ref[0], or_ref[0])
    mixed = None
    for n, o in enumerate(outs):
        br = _dot(o.astype(BF16), wb_ref[n])
        term = _sigmoid(zm_ref[0, :, n * d:(n + 1) * d]) * br
        mixed = term if mixed is None else mixed + term
    y = _dot(mixed.astype(BF16), wo_ref[...])
    o_ref[0] = x_ref[0] + gate_ref[0] * y


def _merge(z, o_att, o_c, o_r, w_branch, w_out, x, gate, tm):
    b, t, d = x.shape
    nb, bw, _ = w_branch.shape
    tmod = tm if gate.shape[1] == t else 1
    mod_map = (lambda bi, i: (bi, i, 0)) if gate.shape[1] == t else (lambda bi, i: (bi, 0, 0))
    return pl.pallas_call(
        _merge_kernel,
        grid=(b, t // tm),
        in_specs=[pl.BlockSpec((1, tm, nb * d), lambda bi, i: (bi, i, C_MERGE // (nb * d))),
                  pl.BlockSpec((1, tm, 2 * bw), lambda bi, i: (bi, i, 0)),
                  pl.BlockSpec((1, tm, bw), lambda bi, i: (bi, i, 0)),
                  pl.BlockSpec((1, tm, bw), lambda bi, i: (bi, i, 0)),
                  pl.BlockSpec((nb, bw, d), lambda bi, i: (0, 0, 0)),
                  pl.BlockSpec((d, d), lambda bi, i: (0, 0)),
                  pl.BlockSpec((1, tm, d), lambda bi, i: (bi, i, 0)),
                  pl.BlockSpec((1, tmod, d), mod_map)],
        out_specs=pl.BlockSpec((1, tm, d), lambda bi, i: (bi, i, 0)),
        out_shape=jax.ShapeDtypeStruct((b, t, d), F32),
        compiler_params=_cparams(("parallel", "parallel")),
        name="merge",
    )(z, o_att, o_c, o_r, w_branch, w_out, x, gate)


def _ffn_kernel(x_ref, g_ref, sh_ref, sc_ref, gate_ref, wg_ref, wu_ref, wd_ref, fg_ref, o_ref,
                h_scr, acc_scr, *, final_norm):
    k = pl.program_id(2)

    @pl.when(k == 0)
    def _():
        h_scr[...] = _rms_mod(x_ref[0], g_ref[...], sc_ref[0], sh_ref[0]).astype(BF16)
        acc_scr[...] = jnp.zeros_like(acc_scr)

    h = h_scr[...]
    gt = _dot(h, wg_ref[...])
    up = _dot(h, wu_ref[...])
    act = (gt * _sigmoid(gt)) * up
    acc_scr[...] += _dot(act.astype(BF16), wd_ref[...])

    @pl.when(k == pl.num_programs(2) - 1)
    def _():
        y = x_ref[0] + gate_ref[0] * acc_scr[...]
        if final_norm:
            ms = jnp.mean(y * y, axis=-1, keepdims=True)
            y = y * lax.rsqrt(ms + EPS) * fg_ref[...]
        o_ref[0] = y


def _ffn(x, g, shift, scale, gate, w_in, w_out, final_g, final_norm, tm, tf):
    b, t, d = x.shape
    ff = w_out.shape[0]
    nk = ff // tf
    tmod = tm if gate.shape[1] == t else 1
    mod_map = (lambda bi, i, k: (bi, i, 0)) if gate.shape[1] == t else (lambda bi, i, k: (bi, 0, 0))
    return pl.pallas_call(
        functools.partial(_ffn_kernel, final_norm=final_norm),
        grid=(b, t // tm, nk),
        in_specs=[pl.BlockSpec((1, tm, d), lambda bi, i, k: (bi, i, 0)),
                  pl.BlockSpec((1, d), lambda bi, i, k: (0, 0)),
                  pl.BlockSpec((1, tmod, d), mod_map),
                  pl.BlockSpec((1, tmod, d), mod_map),
                  pl.BlockSpec((1, tmod, d), mod_map),
                  pl.BlockSpec((d, tf), lambda bi, i, k: (0, k)),
                  pl.BlockSpec((d, tf), lambda bi, i, k: (0, nk + k)),
                  pl.BlockSpec((tf, d), lambda bi, i, k: (k, 0)),
                  pl.BlockSpec((1, d), lambda bi, i, k: (0, 0))],
        out_specs=pl.BlockSpec((1, tm, d), lambda bi, i, k: (bi, i, 0)),
        out_shape=jax.ShapeDtypeStruct((b, t, d), F32),
        scratch_shapes=[pltpu.VMEM((tm, d), BF16), pltpu.VMEM((tm, d), F32)],
        compiler_params=_cparams(("parallel", "parallel", "arbitrary")),
        name="ffn",
    )(x, g.reshape(1, d), shift, scale, gate, w_in, w_in, w_out, final_g.reshape(1, d))


def _cmp_kernel(r_ref, pa_ref, pb_ref, w1a_ref, w1b_ref, w2_ref, ob_ref, ot_ref):
    n16 = r_ref.shape[2]
    comp = []
    for z in range(2):
        r = r_ref[0, z]
        a = _dot((r + pa_ref[z]).astype(BF16), w1a_ref[z])
        b = _dot((r + pb_ref[z]).astype(BF16), w1b_ref[z])
        hid = _gelu_tanh(a + pltpu.roll(b, n16 - 1, axis=0))
        comp.append(_dot(hid.astype(BF16), w2_ref[z]))
    pair = jnp.concatenate(comp, axis=1)
    ob_ref[0] = pair.astype(BF16)
    ot_ref[0] = pair.T.astype(BF16)


def _nsa_compress(r, pos_a, pos_b, w1a, w1b, w2):
    b, _, n16, kd = r.shape
    hid = w1a.shape[-1]
    return pl.pallas_call(
        _cmp_kernel,
        grid=(b,),
        in_specs=[pl.BlockSpec((1, 2, n16, kd), lambda bi: (bi, 0, 0, 0)),
                  pl.BlockSpec((2, 1, kd), lambda bi: (0, 0, 0)),
                  pl.BlockSpec((2, 1, kd), lambda bi: (0, 0, 0)),
                  pl.BlockSpec((2, kd, hid), lambda bi: (0, 0, 0)),
                  pl.BlockSpec((2, kd, hid), lambda bi: (0, 0, 0)),
                  pl.BlockSpec((2, hid, HEAD_DIM), lambda bi: (0, 0, 0))],
        out_specs=[pl.BlockSpec((1, n16, 2 * HEAD_DIM), lambda bi: (bi, 0, 0)),
                   pl.BlockSpec((1, 2 * HEAD_DIM, n16), lambda bi: (bi, 0, 0))],
        out_shape=[jax.ShapeDtypeStruct((b, n16, 2 * HEAD_DIM), BF16),
                   jax.ShapeDtypeStruct((b, 2 * HEAD_DIM, n16), BF16)],
        compiler_params=_cparams(("parallel",)),
        name="nsa_compress",
    )(r, pos_a, pos_b, w1a, w1b, w2)


CK = 256


def _kprep_kernel(z_ref, ob_ref, ot_ref):
    x = z_ref[0]
    ob_ref[0, 0] = x.astype(BF16)
    ot_ref[0, 0, 0] = x.T.astype(BF16)


def _kprep(z):
    b, t, _ = z.shape
    return pl.pallas_call(
        _kprep_kernel,
        grid=(b, K_PAIRS, t // CK),
        in_specs=[pl.BlockSpec((1, CK, LANE), lambda bi, p, c: (bi, c, C_K // LANE + p))],
        out_specs=[pl.BlockSpec((1, 1, CK, LANE), lambda bi, p, c: (bi, p, c, 0)),
                   pl.BlockSpec((1, 1, 1, LANE, CK), lambda bi, p, c: (bi, p, c, 0, 0))],
        out_shape=[jax.ShapeDtypeStruct((b, K_PAIRS, t, LANE), BF16),
                   jax.ShapeDtypeStruct((b, K_PAIRS, t // CK, LANE, CK), BF16)],
        compiler_params=_cparams(("parallel", "parallel", "parallel")),
        name="kprep",
    )(z)


INT_MIN = -2 ** 31


def _sortable_key(v):
    v = jnp.where(v == 0.0, 0.0, v)
    u = lax.bitcast_convert_type(v, I32)
    return jnp.where(u < 0, u ^ 0x7FFFFFFF, u)


_KEY_HALF_NEG = int(np.array(0.5 * NEG, np.float32).view(np.int32) ^ 0x7FFFFFFF)


def _lane_consts(q0):
    lane = lax.broadcasted_iota(I32, (1, N_HEADS * QB), 1)
    hl = lane >> 7
    q_pos = q0 + (lane & (QB - 1))
    slope = jnp.where(hl == 0, 2.0 ** -2, jnp.where(hl == 1, 2.0 ** -4, jnp.where(hl == 2, 2.0 ** -6, 2.0 ** -8)))
    return q_pos, slope.astype(F32)


def _tile_heads(x):
    return jnp.concatenate([x] * N_HEADS, axis=1)


def _place_heads(tiles):
    lane = lax.broadcasted_iota(I32, (QB, LANE), 1)
    out = []
    for t in range(N_HEADS // 2):
        out.append(jnp.where(lane < HEAD_DIM, pltpu.roll(tiles[2 * t], HEAD_DIM, axis=1), tiles[2 * t + 1]))
    return jnp.concatenate(out, axis=1)


def _attn_kernel(qn_ref, qd_ref, qi_ref, misc_ref, kb_ref, kt_ref, cb_ref, ct_ref, c2s_ref, o_ref,
                 key_scr, dsel_scr, acc_scr, ml_scr, *, seq_len, topk):
    t = seq_len
    q0 = pl.program_id(1) * QB
    nc = (q0 + QB + CK - 1) // CK
    hq = N_HEADS * QB
    scale = HEAD_DIM ** -0.5

    def stack_q(ref):
        return (jnp.concatenate([ref[0, :, h * LANE:(h + 1) * LANE] for h in range(N_HEADS)], axis=0) * scale).astype(BF16)

    qn, qd, qi = stack_q(qn_ref), stack_q(qd_ref), stack_q(qi_ref)
    misc_t = misc_ref[0].T
    q_pos, slope = _lane_consts(q0)

    n16 = cb_ref.shape[1]
    sc = _dot_nt(cb_ref[0], qn)
    cmp_end = lax.broadcasted_iota(I32, (n16, hq), 0) * NSA_CMP_STRIDE + (NSA_CMP_LEN - 1)
    d = q_pos - cmp_end
    mask = d >= 0
    sm = jnp.where(mask, sc - slope * d.astype(F32), NEG)
    e = jnp.where(mask, jnp.exp(sm - jnp.max(sm, axis=0, keepdims=True)), 0.0)
    l = jnp.sum(e, axis=0, keepdims=True)
    p = e / jnp.where(l > 0.0, l, 1.0)
    o_cmp_t = _dot(ct_ref[0], p.astype(BF16))
    psum_t = p[:, 0:QB]
    for h in range(1, N_HEADS):
        psum_t = psum_t + p[:, h * QB:(h + 1) * QB]
    imp_t = jnp.dot(c2s_ref[...], psum_t, preferred_element_type=F32, precision=lax.Precision.HIGHEST)

    nslc = c2s_ref.shape[0]
    jrow = lax.broadcasted_iota(I32, (nslc, QB), 0)
    qp = q0 + lax.broadcasted_iota(I32, (nslc, QB), 1)
    cur = qp >> 6
    adm = jrow * NSA_SLC_LEN <= qp
    forced = (jrow == 0) | (jrow == cur) | (jrow == cur - 1)
    score = jnp.where(adm, imp_t + jnp.where(forced, FORCE_BONUS, 0.0), NEG)
    rank = jnp.zeros((nslc, QB), I32)
    for j in range(nslc):
        row = score[j:j + 1, :]
        rank = rank + jnp.where((row > score) | ((row == score) & (jrow > j)), 1, 0)
    sel_t = jnp.where((rank < min(NSA_TOPN, nslc)) & (score > 0.5 * NEG), 1.0, 0.0).astype(BF16)

    w_rows = [misc_t[12 + h:13 + h, :] * (N_HEADS ** -0.5) for h in range(N_HEADS)]

    def idx_body(c, carry):
        rows = pl.ds(pl.multiple_of(c * CK, CK), CK)
        lg = jnp.maximum(_dot_nt(kb_ref[0, P_IDX, rows, :], qi), 0.0)
        idx = lg[:, 0:QB] * w_rows[0]
        for h in range(1, N_HEADS):
            idx = idx + lg[:, h * QB:(h + 1) * QB] * w_rows[h]
        kpos = c * CK + lax.broadcasted_iota(I32, (CK, QB), 0)
        causal = kpos <= q0 + lax.broadcasted_iota(I32, (CK, QB), 1)
        key_scr[rows, :] = _sortable_key(jnp.where(causal, idx, NEG))
        return carry

    lax.fori_loop(0, nc, idx_body, 0)

    def count(pred_fn):
        def body(c, acc):
            rows = pl.ds(pl.multiple_of(c * CK, CK), CK)
            return acc + jnp.sum(jnp.where(pred_fn(key_scr[rows, :]), 1, 0), axis=0, keepdims=True)
        return lax.fori_loop(0, nc, body, jnp.zeros((1, QB), I32))

    def bit_body(i, ans):
        cand = ans + lax.shift_left(jnp.int32(1), 31 - i)
        return jnp.where(count(lambda k: k >= cand) >= topk, cand, ans)

    thr = lax.fori_loop(0, 32, bit_body, jnp.full((1, QB), INT_MIN, I32))
    need = (topk - count(lambda k: k > thr)).astype(F32)

    tri = jnp.where(lax.broadcasted_iota(I32, (CK, CK), 1) < lax.broadcasted_iota(I32, (CK, CK), 0), 1.0, 0.0).astype(BF16)

    def tie_body(c, running):
        rows = pl.ds(pl.multiple_of(c * CK, CK), CK)
        k = key_scr[rows, :]
        eq = k == thr
        eqf = jnp.where(eq, 1.0, 0.0)
        before = _dot(tri, eqf.astype(BF16)) + running
        sel = ((k > thr) | (eq & (before < need))) & (k > _KEY_HALF_NEG)
        dsel_scr[rows, :] = jnp.where(sel, 1.0, 0.0)
        return running + jnp.sum(eqf, axis=0, keepdims=True)

    lax.fori_loop(0, nc, tie_body, jnp.zeros((1, QB), F32))

    ml_scr[0:2, :] = jnp.full((2, hq), NEG, F32)
    ml_scr[2:4, :] = jnp.zeros((2, hq), F32)
    acc_scr[...] = jnp.zeros_like(acc_scr)

    def online(br, s, mask, kt):
        sm_ = jnp.where(mask, s, NEG)
        m_old = ml_scr[br:br + 1, :]
        m_new = jnp.maximum(m_old, jnp.max(sm_, axis=0, keepdims=True))
        alpha = jnp.exp(m_old - m_new)
        e_ = jnp.exp(sm_ - m_new)
        ml_scr[br:br + 1, :] = m_new
        ml_scr[2 + br:3 + br, :] = alpha * ml_scr[2 + br:3 + br, :] + jnp.sum(e_, axis=0, keepdims=True)
        acc_scr[br] = alpha * acc_scr[br] + _dot(kt, e_.astype(BF16))

    def flash_body(c, carry):
        rows = pl.ds(pl.multiple_of(c * CK, CK), CK)
        kpos = c * CK + lax.broadcasted_iota(I32, (CK, hq), 0)
        dd = q_pos - kpos
        bias = slope * dd.astype(F32)
        blk = ((c * CK + lax.broadcasted_iota(I32, (CK, nslc), 0)) >> 6) == lax.broadcasted_iota(I32, (CK, nslc), 1)
        sel_rows = _dot(jnp.where(blk, 1.0, 0.0).astype(BF16), sel_t)
        online(0, _dot_nt(kb_ref[0, P_SLC, rows, :], qn) - bias,
               (dd >= 0) & (_tile_heads(sel_rows) > 0.5), kt_ref[0, P_SLC, c])
        online(1, _dot_nt(kb_ref[0, P_DSA, rows, :], qd) - bias,
               _tile_heads(dsel_scr[rows, :]) > 0.5, kt_ref[0, P_DSA, c])
        return carry

    lax.fori_loop(0, nc, flash_body, 0)

    def finish(br):
        m_, l_ = ml_scr[br:br + 1, :], ml_scr[2 + br:3 + br, :]
        return acc_scr[br] * jnp.where(m_ > 0.5 * NEG, 1.0 / l_, 0.0)

    o_slc_t, o_dsa_t = finish(0), finish(1)

    wk = min(NSA_WINDOW + QB, t)
    ws = pl.multiple_of(jnp.clip(q0 - NSA_WINDOW, 0, t - wk), QB)
    kw = kb_ref[0, P_WIN, pl.ds(ws, wk), :]
    dw = q_pos - (ws + lax.broadcasted_iota(I32, (wk, hq), 0))
    mask = (dw >= 0) & (dw <= NSA_WINDOW)
    sm = jnp.where(mask, _dot_nt(kw, qn) - slope * dw.astype(F32), NEG)
    mw = jnp.max(sm, axis=0, keepdims=True)
    e = jnp.exp(sm - mw)
    lw = jnp.sum(e, axis=0, keepdims=True)
    o_win_t = _dot_tn(kw, e.astype(BF16)) * jnp.where(mw > 0.5 * NEG, 1.0 / lw, 0.0)

    gates = _sigmoid(misc_t[0:3 * N_HEADS, :])
    nsa_tiles, dsa_tiles = [], []
    for h in range(N_HEADS):
        cols = slice(h * QB, (h + 1) * QB)
        on = (gates[3 * h:3 * h + 1, :] * o_cmp_t[:, cols] + gates[3 * h + 1:3 * h + 2, :] * o_slc_t[:, cols]
              + gates[3 * h + 2:3 * h + 3, :] * o_win_t[:, cols])
        nsa_tiles.append(on.T)
        dsa_tiles.append(o_dsa_t[:, cols].T)
    o_ref[0, :, 0:N_HEADS * HEAD_DIM] = _place_heads(nsa_tiles)
    o_ref[0, :, N_HEADS * HEAD_DIM:2 * N_HEADS * HEAD_DIM] = _place_heads(dsa_tiles)


def _cmp_to_slc_t(n16, n_slc):
    start = np.arange(n16) * NSA_CMP_STRIDE
    bstart = np.arange(n_slc) * NSA_SLC_LEN
    ov = np.minimum(start[:, None] + NSA_CMP_LEN, bstart[None, :] + NSA_SLC_LEN) - np.maximum(start[:, None], bstart[None, :])
    return (np.clip(ov, 0, None) / NSA_CMP_LEN).T.astype(np.float32)


def _prompt_attention(z, kb, kt, cb, ct):
    b, t, _ = z.shape
    n16 = cb.shape[1]
    n_slc = -(-t // NSA_SLC_LEN)
    hq = N_HEADS * QB
    qw = N_HEADS * LANE
    topk = min(DSA_TOPK, t // 4)
    c2s = jnp.asarray(_cmp_to_slc_t(n16, n_slc))
    qspec =lambda col: pl.BlockSpec((1, QB, qw), lambda bi, i: (bi, i, col // qw))
    return pl.pallas_call(
        functools.partial(_attn_kernel, seq_len=t, topk=topk),
        grid=(b, t // QB),
        in_specs=[qspec(C_QN), qspec(C_QD), qspec(C_QI),
                  pl.BlockSpec((1, QB, LANE), lambda bi, i: (bi, i, C_MISC // LANE)),
                  pl.BlockSpec((1, K_PAIRS, t, LANE), lambda bi, i: (bi, 0, 0, 0)),
                  pl.BlockSpec((1, K_PAIRS, t // CK, LANE, CK), lambda bi, i: (bi, 0, 0, 0, 0)),
                  pl.BlockSpec((1, n16, LANE), lambda bi, i: (bi, 0, 0)),
                  pl.BlockSpec((1, LANE, n16), lambda bi, i: (bi, 0, 0)),
                  pl.BlockSpec((n_slc, n16), lambda bi, i: (0, 0))],
        out_specs=pl.BlockSpec((1, QB, 2 * N_HEADS * HEAD_DIM), lambda bi, i: (bi, i, 0)),
        out_shape=jax.ShapeDtypeStruct((b, t, 2 * N_HEADS * HEAD_DIM), F32),
        scratch_shapes=[pltpu.VMEM((t, QB), I32), pltpu.VMEM((t, QB), F32),
                        pltpu.VMEM((2, LANE, hq), F32), pltpu.VMEM((8, hq), F32)],
        compiler_params=_cparams(("parallel", "parallel")),
        name="prompt_attention",
    )(z, z, z, z, kb, kt, cb, ct, c2s)


def _layer_norm(v, g, b):
    mu = jnp.mean(v, axis=-1, keepdims=True)
    var = jnp.mean(jnp.square(v - mu), axis=-1, keepdims=True)
    return (v - mu) * lax.rsqrt(var + LN_EPS) * g + b


def _gmlp_kernel(z_ref, lng_ref, lnb_ref, ws_ref, bs_ref, o_ref):
    tc = z_ref.shape[1]
    c = ws_ref.shape[1]
    tril = lax.broadcasted_iota(I32, (c, c), 1) <= lax.broadcasted_iota(I32, (c, c), 0)
    lane_g = lax.broadcasted_iota(I32, (1, GMLP_WIDTH), 1) >> 6
    ws = [jnp.where(tril, ws_ref[g], 0.0).astype(BF16) for g in range(GMLP_GROUPS)]
    for ci in range(tc // c):
        zg = _gelu_tanh(z_ref[0, ci * c:(ci + 1) * c, :])
        u = zg[:, 0:GMLP_WIDTH]
        v = _layer_norm(zg[:, GMLP_WIDTH:2 * GMLP_WIDTH], lng_ref[...], lnb_ref[...]).astype(BF16)
        s = bs_ref[...]
        for g in range(GMLP_GROUPS):
            s = s + jnp.where(lane_g == g, _dot(ws[g], v), 0.0)
        o_ref[0, ci * c:(ci + 1) * c, :] = u * s


def _gmlp(z, ln_g, ln_b, w_s, b_s, tc):
    b, t, _ = z.shape
    c = w_s.shape[1]
    bs_exp = jnp.repeat(b_s.T, HEAD_DIM, axis=1)
    return pl.pallas_call(
        _gmlp_kernel,
        grid=(b, t // tc),
        in_specs=[pl.BlockSpec((1, tc, 2 * GMLP_WIDTH), lambda bi, i: (bi, i, C_GMLP // (2 * GMLP_WIDTH))),
                  pl.BlockSpec((1, GMLP_WIDTH), lambda bi, i: (0, 0)),
                  pl.BlockSpec((1, GMLP_WIDTH), lambda bi, i: (0, 0)),
                  pl.BlockSpec((GMLP_GROUPS, c, c), lambda bi, i: (0, 0, 0)),
                  pl.BlockSpec((c, GMLP_WIDTH), lambda bi, i: (0, 0))],
        out_specs=pl.BlockSpec((1, tc, GMLP_WIDTH), lambda bi, i: (bi, i, 0)),
        out_shape=jax.ShapeDtypeStruct((b, t, GMLP_WIDTH), F32),
        compiler_params=_cparams(("parallel", "parallel")),
        name="gmlp",
    )(z, ln_g.reshape(1, -1), ln_b.reshape(1, -1), w_s, bs_exp)


def _head_sum(x):
    lane_h = lax.broadcasted_iota(I32, (1, x.shape[-1]), 1) >> 6
    out = jnp.zeros_like(x)
    for h in range(x.shape[-1] // HEAD_DIM):
        msk = lane_h == h
        out = out + jnp.where(msk, jnp.sum(jnp.where(msk, x, 0.0), axis=-1, keepdims=True), 0.0)
    return out


def _softplus(x):
    return jnp.maximum(x, 0.0) + jnp.log1p(jnp.exp(-jnp.abs(x)))


def _rwkv_features(f, w0, w2, a0, a2, g2, kkw, ka, rk):
    wd = RWKV_WIDTH
    r, k, v = f[:, 0:wd], f[:, wd:2 * wd], f[:, 2 * wd:3 * wd]
    wl, al, gl = f[:, 3 * wd:3 * wd + 64], f[:, 3 * wd + 64:3 * wd + 128], f[:, 3 * wd + 128:3 * wd + 256]
    w_log = -_softplus(-(w0 + _dot(jnp.tanh(wl).astype(BF16), w2))) - 0.5
    log_decay = -jnp.exp(w_log)
    a = _sigmoid(a0 + _dot(al.astype(BF16), a2))
    g = _dot(_sigmoid(gl).astype(BF16), g2)
    kk = k * kkw
    kk = kk * lax.rsqrt(_head_sum(kk * kk) + 1e-12)
    k = k * (1.0 + (a - 1.0) * ka)
    bonus = _head_sum(r * k * rk) * v
    return r, log_decay, k, v, kk, kk * a, g, bonus


def _rwkv_pre_kernel(f_ref, prev_ref, shift_ref, mu_ref, w0_ref, w2_ref, a0_ref, a2_ref, g2_ref, kkw_ref, ka_ref, rk_ref,
                     r_o, lw_o, k_o, v_o, kk_o, be_o, g_o, bonus_o):
    feat = f_ref[0]
    tm = feat.shape[0]
    first = jnp.where(pl.program_id(1) == 0, shift_ref[0], prev_ref[0, 7:8, :])
    prev = jnp.where(lax.broadcasted_iota(I32, (tm, 1), 0) == 0, first, pltpu.roll(feat, 1, axis=0))
    f = feat + mu_ref[...] * (prev - feat)
    r, lw, k, v, kk, be, g, bonus = _rwkv_features(f, w0_ref[...], w2_ref[...], a0_ref[...], a2_ref[...], g2_ref[...],
                                                   kkw_ref[...], ka_ref[...], rk_ref[...])
    for o, x in ((r_o, r), (lw_o, lw), (k_o, k), (v_o, v), (kk_o, kk), (be_o, be)):
        for h in range(N_HEADS):
            o[0, h] = x[:, h * HEAD_DIM:(h + 1) * HEAD_DIM]
    g_o[0] = g
    bonus_o[0] = bonus


def _rwkv_pre(z, shift_prev, lp, tm):
    b, t, _ = z.shape
    wd = RWKV_WIDTH
    pw = 4 * wd
    vec = lambda n: pl.BlockSpec((1, n), lambda bi, i: (0, 0))
    mat = lambda r, c: pl.BlockSpec((r, c), lambda bi, i: (0, 0))
    hm = jax.ShapeDtypeStruct((b, N_HEADS, t, HEAD_DIM), F32)
    fl = jax.ShapeDtypeStruct((b, t, wd), F32)
    hspec = pl.BlockSpec((1, N_HEADS, tm, HEAD_DIM), lambda bi, i: (bi, 0, i, 0))
    fspec = pl.BlockSpec((1, tm, wd), lambda bi, i: (bi, i, 0))
    return pl.pallas_call(
        _rwkv_pre_kernel,
        grid=(b, t // tm),
        in_specs=[pl.BlockSpec((1, tm, pw), lambda bi, i: (bi, i, C_RWKV // pw)),
                  pl.BlockSpec((1, 8, pw), lambda bi, i: (bi, jnp.maximum(i * (tm // 8) - 1, 0), C_RWKV // pw)),
                  pl.BlockSpec((1, 1, pw), lambda bi, i: (bi, 0, 0)),
                  vec(pw), vec(wd), mat(64, wd), vec(wd), mat(64, wd), mat(128, wd), vec(wd), vec(wd), vec(wd)],
        out_specs=[hspec] * 6 + [fspec] * 2,
        out_shape=[hm] * 6 + [fl] * 2,
        compiler_params=_cparams(("parallel", "parallel")),
        name="rwkv_pre",
    )(z, z, shift_prev, lp['rwkv_mu'].reshape(1, pw), lp['rwkv_w0'].reshape(1, wd), lp['rwkv_w2'].astype(BF16),
      lp['rwkv_a0'].reshape(1, wd), lp['rwkv_a2'].astype(BF16), lp['rwkv_g2'].astype(BF16),
      lp['rwkv_kk'].reshape(1, wd), lp['rwkv_ka'].reshape(1, wd), lp['rwkv_rk'].reshape(1, wd))


RWKV_CHUNK = 64


def _hdot(a, b):
    return jnp.dot(a, b, preferred_element_type=F32, precision=lax.Precision.HIGHEST)


def _hdot_nt(a, b):
    return lax.dot_general(a, b, (((1,), (1,)), ((), ())), preferred_element_type=F32, precision=lax.Precision.HIGHEST)


def _hdot_tn(a, b):
    return lax.dot_general(a, b, (((0,), (0,)), ((), ())), preferred_element_type=F32, precision=lax.Precision.HIGHEST)


def _group_norm_heads(y):
    mu = jnp.mean(y, axis=-1, keepdims=True)
    var = jnp.mean(jnp.square(y - mu), axis=-1, keepdims=True)
    return (y - mu) * lax.rsqrt(var + RWKV_GN_EPS)


def _rwkv_scan_kernel(r_ref, lw_ref, k_ref, v_ref, kk_ref, be_ref, g_ref, bonus_ref, lnw_ref, lnb_ref, s0_ref,
                      o_ref, s_ref):
    @pl.when(pl.program_id(1) == 0)
    def _():
        s_ref[...] = s0_ref[...]

    c = r_ref.shape[2]
    row = lax.broadcasted_iota(I32, (c, c), 0)
    col = lax.broadcasted_iota(I32, (c, c), 1)
    incl, strict = col <= row, col < row
    ones_incl = jnp.where(incl, 1.0, 0.0)
    eye = jnp.where(row == col, 1.0, 0.0)
    ys = []
    for h in range(N_HEADS):
        r, lw, k, v, kk, be = r_ref[0, h], lw_ref[0, h], k_ref[0, h], v_ref[0, h], kk_ref[0, h], be_ref[0, h]
        cum = _hdot(ones_incl, lw)
        p, pinv = jnp.exp(cum), jnp.exp(-cum)
        kh, kb, bb, rh = kk * jnp.exp(cum - lw), k * pinv, be * pinv, r * p
        a_kb = jnp.where(strict, _hdot_nt(kh, bb), 0.0)
        a_kk = jnp.where(strict, _hdot_nt(kh, kb), 0.0)
        a_rk = jnp.where(incl, _hdot_nt(rh, kb), 0.0)
        a_rb = jnp.where(incl, _hdot_nt(rh, bb), 0.0)
        xp = -a_kb
        tinv = eye + xp
        for _ in range(int(np.log2(c)) - 1):
            xp = _hdot(xp, xp)
            tinv = tinv + _hdot(tinv, xp)
        s0 = s_ref[0, h]
        u = -_hdot(tinv, _hdot_nt(kh, s0) + _hdot(a_kk, v))
        y = _hdot_nt(rh, s0) + _hdot(a_rk, v) + _hdot(a_rb, u)
        pc = p[c - 1:c, :]
        s_ref[0, h] = s0 * pc + _hdot_tn(v, kb * pc) + _hdot_tn(u, bb * pc)
        ys.append(_group_norm_heads(y))
    y_all = jnp.concatenate(ys, axis=1)
    o_ref[0] = (y_all * lnw_ref[...] + lnb_ref[...] + bonus_ref[0]) * g_ref[0]


def _rwkv_scan(r, lw, k, v, kk, be, g, bonus, ln_w, ln_b, s0):
    b, _, t, _ = r.shape
    c = min(RWKV_CHUNK, t)
    wd = RWKV_WIDTH
    hspec = pl.BlockSpec((1, N_HEADS, c, HEAD_DIM), lambda bi, i: (bi, 0, i, 0))
    fspec = pl.BlockSpec((1, c, wd), lambda bi, i: (bi, i, 0))
    sspec = pl.BlockSpec((1, N_HEADS, HEAD_DIM, HEAD_DIM), lambda bi, i: (bi, 0, 0, 0))
    vec = pl.BlockSpec((1, wd), lambda bi, i: (0, 0))
    return pl.pallas_call(
        _rwkv_scan_kernel,
        grid=(b, t // c),
        in_specs=[hspec] * 6 + [fspec, fspec, vec, vec, sspec],
        out_specs=[fspec, sspec],
        out_shape=[jax.ShapeDtypeStruct((b, t, wd), F32),
                   jax.ShapeDtypeStruct((b, N_HEADS, HEAD_DIM, HEAD_DIM), F32)],
        compiler_params=_cparams(("parallel", "arbitrary")),
        name="rwkv_scan",
    )(r, lw, k, v, kk, be, g, bonus, ln_w.reshape(1, wd), ln_b.reshape(1, wd), s0)


COL_QN, COL_QD, COL_QI = 0, 4, 8
COL_KSLC, COL_VSLC, COL_KWIN, COL_VWIN, COL_KD, COL_VD, COL_KI = 12, 13, 14, 15, 16, 17, 18
SLOPES = tuple(2.0 ** (-8.0 * (h + 1) / N_HEADS) for h in range(N_HEADS))


def _qcols(cols, first):
    return [jnp.broadcast_to(cols[:, first + h:first + h + 1] * (HEAD_DIM ** -0.5), (HEAD_DIM, LANE)) for h in range(N_HEADS)]


def _col_dot(mat_t, qb):
    return jnp.sum(mat_t * qb, axis=0, keepdims=True)


def _sidx_kernel(pt_ref, cols_ref, misc_ref, *refs, n_pages, page):
    pages, o_ref = refs[:n_pages], refs[n_pages]
    cols = cols_ref[0]
    qi = _qcols(cols, COL_QI)
    w = [misc_ref[0, 0:1, 12 + h:13 + h] * (N_HEADS ** -0.5) for h in range(N_HEADS)]

    def index_of(kt, qs):
        idx = None
        for h in range(N_HEADS):
            term = jnp.maximum(_col_dot(kt, qs[h]), 0.0) * w[h]
            idx = term if idx is None else idx + term
        return idx

    for p in range(n_pages):
        o_ref[0, 0:1, p * page:(p + 1) * page] = index_of(pages[p][0, 0, 0], qi)
    idx_self = index_of(cols[:, COL_KI:COL_KI + 1], [q[:, 0:1] for q in qi])
    lane = lax.broadcasted_iota(I32, (1, LANE), 1)
    o_ref[0, 0:1, n_pages * page:n_pages * page + LANE] = jnp.where(lane == 0, idx_self, NEG)


def _sample_index(page_table, cols, misc, cache_d, layer):
    b, n_pages = page_table.shape
    page = cache_d.shape[-1]
    width = n_pages * page + LANE
    kern = functools.partial(_sidx_kernel, n_pages=n_pages, page=page)
    page_spec = lambda p: pl.BlockSpec((1, 1, 1, HEAD_DIM, page), lambda bi, pt: (layer, pt[bi, p], 2, 0, 0))
    gs = pltpu.PrefetchScalarGridSpec(
        num_scalar_prefetch=1, grid=(b,),
        in_specs=[pl.BlockSpec((1, HEAD_DIM, LANE), lambda bi, pt: (bi, 0, 0)),
                  pl.BlockSpec((1, 8, LANE), lambda bi, pt: (bi, 0, 0))] + [page_spec(p) for p in range(n_pages)],
        out_specs=pl.BlockSpec((1, 1, width), lambda bi, pt: (bi, 0, 0)))
    return pl.pallas_call(kern, grid_spec=gs, out_shape=jax.ShapeDtypeStruct((b, 1, width), F32),
                          compiler_params=_cparams(("parallel",)), name="sample_index",
                          )(page_table, cols, misc, *([cache_d] * n_pages))


def _stopk_kernel(idx_ref, o_ref, *, topk):
    key = _sortable_key(idx_ref[...])
    rows, width = key.shape

    def bit_body(i, ans):
        cand = ans + lax.shift_left(jnp.int32(1), 31 - i)
        cnt = jnp.sum(jnp.where(key >= cand, 1, 0), axis=1, keepdims=True)
        return jnp.where(cnt >= topk, cand, ans)

    thr = lax.fori_loop(0, 32, bit_body, jnp.full((rows, 1), INT_MIN, I32))
    need = (topk - jnp.sum(jnp.where(key > thr, 1, 0), axis=1, keepdims=True)).astype(F32)
    tri = jnp.where(lax.broadcasted_iota(I32, (LANE, LANE), 0) < lax.broadcasted_iota(I32, (LANE, LANE), 1), 1.0, 0.0).astype(BF16)
    running = jnp.zeros((rows, 1), F32)
    for c in range(width // LANE):
        k = key[:, c * LANE:(c + 1) * LANE]
        eq = k == thr
        eqf = jnp.where(eq, 1.0, 0.0)
        before = _dot(eqf.astype(BF16), tri) + running
        sel = ((k > thr) | (eq & (before < need))) & (k > _KEY_HALF_NEG)
        o_ref[:, c * LANE:(c + 1) * LANE] = jnp.where(sel, 1.0, 0.0)
        running = running + jnp.sum(eqf, axis=1, keepdims=True)


def _sample_topk(idx, topk):
    return pl.pallas_call(functools.partial(_stopk_kernel, topk=topk),
                          out_shape=jax.ShapeDtypeStruct(idx.shape, F32),
                          compiler_params=pltpu.CompilerParams(vmem_limit_bytes=VMEM_LIMIT), name="sample_topk")(idx)


def _sattn_kernel(pt_ref, cols_ref, misc_ref, dsel_ref, win_ref, w1a_ref, w1b_ref, posa_ref, posb_ref, w2_ref, c2s_ref,
                  exp_ref, *refs, n_pages, page):
    nsa_pages, dsa_pages = refs[:n_pages], refs[n_pages:2 * n_pages]
    o_ref, win_o_ref, xt_scr = refs[2 * n_pages:2 * n_pages + 3]
    past = n_pages * page
    q_pos = past
    cols = cols_ref[0]
    qn, qd = _qcols(cols, COL_QN), _qcols(cols, COL_QD)
    newcol = lambda j: cols[:, j:j + 1]
    gates = _sigmoid(misc_ref[0, 0:1, 0:3 * N_HEADS])
    gate = lambda h, c: gates[:, 3 * h + c:3 * h + c + 1]

    def attend(score_chunks, value_chunks, masks, dists, self_k, self_v, self_mask, q):
        slope = q[1]
        sm = [jnp.where(m, s - slope * d.astype(F32), NEG) for s, m, d in zip(score_chunks, masks, dists)]
        s_self = jnp.where(self_mask, jnp.sum(self_k * q[0][:, 0:1], axis=0, keepdims=True), NEG)
        mx = s_self
        for s in sm:
            mx = jnp.maximum(mx, jnp.max(s, axis=1, keepdims=True))
        e_self = jnp.where(self_mask, jnp.exp(s_self - mx), 0.0)
        l = e_self
        acc = None
        for s, m, vt in zip(sm, masks, value_chunks):
            e = jnp.where(m, jnp.exp(s - mx), 0.0)
            l = l + jnp.sum(e, axis=1, keepdims=True)
            part = vt * e
            acc = part if acc is None else acc + part
        n = acc.shape[1]
        o = acc[:, 0:LANE]
        for c in range(1, n // LANE):
            o = o + acc[:, c * LANE:(c + 1) * LANE]
        o = jnp.sum(o, axis=1, keepdims=True) + e_self * self_v
        return o / jnp.where(l > 0.0, l, 1.0)

    for p in range(n_pages):
        xt_scr[p * page:(p + 1) * page, :] = nsa_pages[p][0, 0, 0:2].reshape(2 * HEAD_DIM, page).T
    n16 = past // NSA_CMP_STRIDE
    x_all = jnp.concatenate([xt_scr[pl.ds(r, n16, stride=NSA_CMP_STRIDE), :] for r in range(NSA_CMP_STRIDE)], axis=1)
    a = _dot((x_all + posa_ref[...]).astype(BF16), w1a_ref[...])
    b = _dot((x_all + posb_ref[...]).astype(BF16), w1b_ref[...])
    hid = _gelu_tanh(a + pltpu.roll(b, n16 - 1, axis=0))
    comp_t = _dot_nt(w2_ref[...], hid.astype(BF16))
    kc_t, vc_t = comp_t[0:HEAD_DIM], comp_t[HEAD_DIM:2 * HEAD_DIM]

    blk = lax.broadcasted_iota(I32, (1, n16), 1)
    d_cmp = q_pos - (blk * NSA_CMP_STRIDE + NSA_CMP_LEN - 1)
    m_cmp = d_cmp >= 0
    o_cmp, psum = [], None
    for h in range(N_HEADS):
        sm = jnp.where(m_cmp, _col_dot(kc_t, qn[h][:, 0:1]) - SLOPES[h] * d_cmp.astype(F32), NEG)
        e = jnp.where(m_cmp, jnp.exp(sm - jnp.max(sm, axis=1, keepdims=True)), 0.0)
        l = jnp.sum(e, axis=1, keepdims=True)
        pr = e / jnp.where(l > 0.0, l, 1.0)
        o_cmp.append(jnp.sum(vc_t * pr, axis=1, keepdims=True))
        psum = pr if psum is None else psum + pr

    n_slc = past // NSA_SLC_LEN + 1
    imp = _hdot(jnp.broadcast_to(psum, (8, n16)), c2s_ref[...])[0:1]
    j = lax.broadcasted_iota(I32, (1, LANE), 1)
    cur = q_pos // NSA_SLC_LEN
    forced = (j == 0) | (j == cur) | (j == cur - 1)
    score = jnp.where((j * NSA_SLC_LEN <= q_pos) & (j < n_slc), imp + jnp.where(forced, FORCE_BONUS, 0.0), NEG)
    srow = jnp.broadcast_to(score, (LANE, LANE))
    scol = srow.T
    jp = lax.broadcasted_iota(I32, (LANE, LANE), 0)
    jj = lax.broadcasted_iota(I32, (LANE, LANE), 1)
    rank = jnp.sum(jnp.where((scol > srow) | ((scol == srow) & (jp < jj)), 1, 0), axis=0, keepdims=True)
    sel = jnp.where((rank < min(NSA_TOPN, n_slc)) & (score > 0.5 * NEG), 1.0, 0.0)
    sel_pos = _dot(jnp.broadcast_to(sel, (8, LANE)).astype(BF16), exp_ref[...])[0:1]

    pos = lax.broadcasted_iota(I32, (1, page), 1)
    dist = [q_pos - (p * page + pos) for p in range(n_pages)]
    m_slc = [sel_pos[:, p * page:(p + 1) * page] > 0.5 for p in range(n_pages)]
    m_dsa = [dsel_ref[0, 0:1, p * page:(p + 1) * page] > 0.5 for p in range(n_pages)]
    self_slc = sel_pos[:, past:past + 1] > 0.5
    self_dsa = dsel_ref[0, 0:1, past:past + 1] > 0.5
    wb = win_ref.shape[-1]
    d_win = q_pos - (past - wb + lax.broadcasted_iota(I32, (1, wb), 1))
    m_win = (d_win >= 0) & (d_win <= NSA_WINDOW) & (d_win <= q_pos)
    kw_t, vw_t = win_ref[0, 0, 0], win_ref[0, 0, 1]
    true11 = jnp.full((1, 1), True)
    outs = []
    for h in range(N_HEADS):
        qh = (qn[h], SLOPES[h])
        o_slc = attend([_col_dot(nsa_pages[p][0, 0, 2], qn[h]) for p in range(n_pages)],
                       [nsa_pages[p][0, 0, 3] for p in range(n_pages)], m_slc, dist,
                       newcol(COL_KSLC), newcol(COL_VSLC), self_slc, qh)
        o_win = attend([_col_dot(kw_t, qn[h][:, 0:1])], [vw_t], [m_win], [d_win],
                       newcol(COL_KWIN), newcol(COL_VWIN), true11, qh)
        outs.append(gate(h, 0) * o_cmp[h] + gate(h, 1) * o_slc + gate(h, 2) * o_win)
    for h in range(N_HEADS):
        outs.append(attend([_col_dot(dsa_pages[p][0, 0, 0], qd[h]) for p in range(n_pages)],
                           [dsa_pages[p][0, 0, 1] for p in range(n_pages)], m_dsa, dist,
                           newcol(COL_KD), newcol(COL_VD), self_dsa, (qd[h], SLOPES[h])))
    lane = lax.broadcasted_iota(I32, (HEAD_DIM, LANE), 1)
    res = jnp.zeros((HEAD_DIM, LANE), F32)
    for i, o in enumerate(outs):
        res = jnp.where(lane == i, o, res)
    o_ref[0] = res

    lane_w = lax.broadcasted_iota(I32, (HEAD_DIM, wb), 1)
    for c, colj in enumerate((COL_KWIN, COL_VWIN)):
        win_o_ref[0, c] = jnp.where(lane_w == wb - 1, newcol(colj), pltpu.roll(win_ref[0, 0, c], wb - 1, axis=1))


def _sample_attention(page_table, cols, misc, dsel, win_t, cmp_w, cache_n, cache_d, layer):
    b, n_pages = page_table.shape
    page = cache_n.shape[-1]
    past = n_pages * page
    wb = win_t.shape[-1]
    n16 = past // NSA_CMP_STRIDE
    w1a, w1b, pos_a, pos_b, w2 = cmp_w
    n_slc = past // NSA_SLC_LEN + 1
    c2s = np.zeros((n16, LANE), np.float32)
    c2s[:, :n_slc] = _cmp_to_slc_t(n16, n_slc).T
    expand = (np.arange(past + LANE)[None, :] // NSA_SLC_LEN == np.arange(LANE)[:, None]) & (np.arange(past + LANE)[None, :] <= past)
    kern = functools.partial(_sattn_kernel, n_pages=n_pages, page=page)
    full = lambda shape: pl.BlockSpec(shape, lambda bi, pt: (0,) * len(shape))
    nspec = lambda p: pl.BlockSpec((1, 1, 4, HEAD_DIM, page), lambda bi, pt: (layer, pt[bi, p], 0, 0, 0))
    dspec = lambda p: pl.BlockSpec((1, 1, 2, HEAD_DIM, page), lambda bi, pt: (layer, pt[bi, p], 0, 0, 0))
    gs = pltpu.PrefetchScalarGridSpec(
        num_scalar_prefetch=1, grid=(b,),
        in_specs=[pl.BlockSpec((1, HEAD_DIM, LANE), lambda bi, pt: (bi, 0, 0)),
                  pl.BlockSpec((1, 8, LANE), lambda bi, pt: (bi, 0, 0)),
                  pl.BlockSpec((1, 1, past + LANE), lambda bi, pt: (bi, 0, 0)),
                  pl.BlockSpec((1, 1, 2, HEAD_DIM, wb), lambda bi, pt: (layer, bi, 0, 0, 0)),
                  full(w1a.shape), full(w1b.shape), full(pos_a.shape), full(pos_b.shape), full(w2.shape),
                  full(c2s.shape), full(expand.shape)]
                 + [nspec(p) for p in range(n_pages)] + [dspec(p) for p in range(n_pages)],
        out_specs=[pl.BlockSpec((1, HEAD_DIM, LANE), lambda bi, pt: (bi, 0, 0)),
                   pl.BlockSpec((1, 2, HEAD_DIM, wb), lambda bi, pt: (bi, 0, 0, 0))],
        scratch_shapes=[pltpu.VMEM((past, 2 * HEAD_DIM), F32)])
    return pl.pallas_call(
        kern, grid_spec=gs,
        out_shape=[jax.ShapeDtypeStruct((b, HEAD_DIM, LANE), F32), jax.ShapeDtypeStruct((b, 2, HEAD_DIM, wb), F32)],
        compiler_params=_cparams(("parallel",)), name="sample_attention",
    )(page_table, cols, misc, dsel, win_t, w1a, w1b, pos_a, pos_b, w2, jnp.asarray(c2s), jnp.asarray(expand, BF16),
      *([cache_n] * n_pages), *([cache_d] * n_pages))


def _smix_kernel(zg_ref, zr_ref, shift_ref, s_ref, lng_ref, lnb_ref, gw_ref, gb_ref, mu_ref, w0_ref, w2_ref, a0_ref,
                 a2_ref, g2_ref, kkw_ref, ka_ref, rk_ref, lnw_ref, lnb2_ref, oc_ref, vc_ref, or_ref, so_ref):
    bt = zg_ref.shape[0]
    zg = _gelu_tanh(zg_ref[...])
    v = _layer_norm(zg[:, GMLP_WIDTH:2 * GMLP_WIDTH], lng_ref[...], lnb_ref[...])
    vc_ref[...] = v
    oc_ref[...] = zg[:, 0:GMLP_WIDTH] * (v * gw_ref[...] + gb_ref[...])

    feat = zr_ref[...]
    f = feat + mu_ref[...] * (shift_ref[...] - feat)
    r, lw, k, vv, kk, be, g, bonus = _rwkv_features(f, w0_ref[...], w2_ref[...], a0_ref[...], a2_ref[...], g2_ref[...],
                                                    kkw_ref[...], ka_ref[...], rk_ref[...])
    w = jnp.exp(lw)
    pad = lambda x: jnp.concatenate([x, jnp.zeros((7, HEAD_DIM), F32)], axis=0)
    rows = []
    for i in range(bt):
        ys = []
        for h in range(N_HEADS):
            sl = slice(h * HEAD_DIM, (h + 1) * HEAD_DIM)
            s0 = s_ref[i, h]
            sa = -_hdot_nt(pad(kk[i:i + 1, sl]), s0)[0:1]
            lhs = jnp.concatenate([sa, vv[i:i + 1, sl], jnp.zeros((6, HEAD_DIM), F32)], axis=0)
            rhs = jnp.concatenate([be[i:i + 1, sl], k[i:i + 1, sl], jnp.zeros((6, HEAD_DIM), F32)], axis=0)
            s1 = s0 * w[i:i + 1, sl] + _hdot_tn(lhs, rhs)
            so_ref[i, h] = s1
            ys.append(_group_norm_heads(_hdot_nt(pad(r[i:i + 1, sl]), s1)[0:1]))
        rows.append(jnp.concatenate(ys, axis=1))
    y = jnp.concatenate(rows, axis=0)
    or_ref[...] = (y * lnw_ref[...] + lnb2_ref[...] + bonus) * g


def _sample_mixers(z, shift_prev, wkv_prev, lp, bt):
    b = z.shape[0]
    wd = RWKV_WIDTH
    pw = 4 * wd
    gw = jnp.repeat(lp['gmlp_ws'][:, 0, 0], HEAD_DIM).reshape(1, GMLP_WIDTH)
    gb = jnp.repeat(lp['gmlp_bs'][:, 0], HEAD_DIM).reshape(1, GMLP_WIDTH)
    vec = lambda n: pl.BlockSpec((1, n), lambda i: (0, 0))
    mat = lambda r, c: pl.BlockSpec((r, c), lambda i: (0, 0))
    row = lambda n: pl.BlockSpec((bt, n), lambda i: (i, 0))
    sspec = pl.BlockSpec((bt, N_HEADS, HEAD_DIM, HEAD_DIM), lambda i: (i, 0, 0, 0))
    fl = jax.ShapeDtypeStruct((b, wd), F32)
    return pl.pallas_call(
        _smix_kernel,
        grid=(b // bt,),
        in_specs=[pl.BlockSpec((bt, 2 * GMLP_WIDTH), lambda i: (i, C_GMLP // (2 * GMLP_WIDTH))),
                  pl.BlockSpec((bt, pw), lambda i: (i, C_RWKV // pw)),
                  row(pw), sspec, vec(wd), vec(wd), vec(wd), vec(wd),
                  vec(pw), vec(wd), mat(64, wd), vec(wd), mat(64, wd), mat(128, wd), vec(wd), vec(wd), vec(wd), vec(wd), vec(wd)],
        out_specs=[row(wd), row(wd), row(wd), sspec],
        out_shape=[fl, fl, fl, jax.ShapeDtypeStruct(wkv_prev.shape, F32)],
        compiler_params=_cparams(("parallel",)),
        name="sample_mixers",
    )(z, z, shift_prev, wkv_prev, lp['gmlp_ln_g'].reshape(1, wd), lp['gmlp_ln_b'].reshape(1, wd), gw, gb,
      lp['rwkv_mu'].reshape(1, pw), lp['rwkv_w0'].reshape(1, wd), lp['rwkv_w2'].astype(BF16),
      lp['rwkv_a0'].reshape(1, wd), lp['rwkv_a2'].astype(BF16), lp['rwkv_g2'].astype(BF16),
      lp['rwkv_kk'].reshape(1, wd), lp['rwkv_ka'].reshape(1, wd), lp['rwkv_rk'].reshape(1, wd),
      lp['rwkv_ln_w'].reshape(1, wd), lp['rwkv_ln_b'].reshape(1, wd))


_W_IN_COLS = (('q_nsa', 256), ('kv_nsa', 384), ('g_nsa', 12), ('q_dsa', 256), ('kv_dsa', 128), ('q_idx', 256),
              ('k_idx', 64), ('w_idx', 4), ('gmlp', 512), ('rwkv', 1024), ('merge', 4096))


def _proj_weights(w_in):
    d = w_in.shape[0]
    parts, off = {}, 0
    for name, width in _W_IN_COLS:
        parts[name] = w_in[:, off:off + width]
        off += width

    def pad_heads(w):
        w4 = w.reshape(d, N_HEADS, HEAD_DIM)
        return jnp.concatenate([w4, jnp.zeros_like(w4)], axis=-1).reshape(d, N_HEADS * LANE)

    zeros = lambda n: jnp.zeros((d, n), w_in.dtype)
    w = jnp.concatenate([parts['merge'], parts['rwkv'], parts['gmlp'], pad_heads(parts['q_nsa']), pad_heads(parts['q_dsa']),
                         pad_heads(parts['q_idx']), parts['g_nsa'], parts['w_idx'], zeros(LANE - 16),
                         parts['kv_nsa'], parts['kv_dsa'], parts['k_idx'], zeros(HEAD_DIM), zeros(N_PROJ - C_K - K_PAIRS * LANE)],
                        axis=1)
    return w.astype(BF16)


def _compress_weights_rows(w1, w2, pos):
    half = NSA_CMP_STRIDE * HEAD_DIM
    return (w1[:, :half].astype(BF16), w1[:, half:].astype(BF16),
            pos[:, :NSA_CMP_STRIDE].reshape(2, 1, half), pos[:, NSA_CMP_STRIDE:].reshape(2, 1, half), w2.astype(BF16))


def _compress_weights_pairs(w1, w2, pos):
    hid = w1.shape[-1]
    w1r = w1.reshape(2, 2, NSA_CMP_STRIDE, HEAD_DIM, hid)
    posr = pos.reshape(2, 2, NSA_CMP_STRIDE, HEAD_DIM)
    bigs, poss = [], []
    for half in range(2):
        big = jnp.zeros((NSA_CMP_STRIDE, 2, HEAD_DIM, 2, hid), w1.dtype)
        for c in range(2):
            big = big.at[:, c, :, c, :].set(w1r[c, half])
        bigs.append(big.reshape(NSA_CMP_STRIDE * 2 * HEAD_DIM, 2 * hid).astype(BF16))
        poss.append(jnp.transpose(posr[:, half], (1, 0, 2)).reshape(1, NSA_CMP_STRIDE * 2 * HEAD_DIM))
    w2b = jnp.zeros((2, hid, 2, HEAD_DIM), w2.dtype)
    for c in range(2):
        w2b = w2b.at[c, :, c, :].set(w2[c])
    return bigs[0], bigs[1], poss[0], poss[1], w2b.reshape(2 * hid, 2 * HEAD_DIM).T.astype(BF16)


def _pick(n, cands):
    for c in cands:
        if n % c == 0:
            return c
    return n


def kernel(x_prompt, x_sample, cache_nsa, cache_dsa, state_nsa_win, state_rwkv_shift, state_rwkv_wkv, page_table,
           c_prompt, c_sample, w_ada, b_ada, norm_mix_g, norm_ffn_g, w_in, nsa_cmp_w1, nsa_cmp_w2, nsa_cmp_pos,
           gmlp_ln_g, gmlp_ln_b, gmlp_ws, gmlp_bs, rwkv_mu, rwkv_w0, rwkv_w2, rwkv_a0, rwkv_a2, rwkv_g2, rwkv_kk,
           rwkv_ka, rwkv_rk, rwkv_ln_w, rwkv_ln_b, w_branch, w_out, w_ffn_in, w_ffn_out, final_norm_g):
    depth = w_ada.shape[0]
    bp, t, d = x_prompt.shape
    bs = x_sample.shape[0]
    assert x_sample.shape[1] == 1 and t % CK == 0 and t % (8 * NSA_CMP_STRIDE) == 0
    cache_n = jnp.transpose(cache_nsa, (0, 1, 3, 4, 2))
    cache_d = jnp.transpose(cache_dsa, (0, 1, 3, 4, 2))
    win_t = jnp.transpose(state_nsa_win, (0, 1, 3, 4, 2))
    past = page_table.shape[1] * cache_nsa.shape[2]
    c_all = jnp.concatenate([c_prompt, c_sample], axis=0)
    xp, xs = x_prompt, x_sample.reshape(1, bs, d)
    tm_p = _pick(t, (512, 256, 128))
    tm_f = _pick(t, (1024, 512, 256, 128))
    tf = _pick(w_ffn_out.shape[1], (256, 128))
    tn = _pick(N_PROJ, (2048,))
    outs = {k: [] for k in ('rows_n_p', 'rows_n_s', 'rows_d_p', 'rows_d_s', 'win_p', 'win_s', 'v_s', 'shift_p', 'shift_s',
                            'wkv_p', 'wkv_s')}
    for l in range(depth):
        lp = {'gmlp_ln_g': gmlp_ln_g[l], 'gmlp_ln_b': gmlp_ln_b[l], 'gmlp_ws': gmlp_ws[l], 'gmlp_bs': gmlp_bs[l],
              'rwkv_mu': rwkv_mu[l], 'rwkv_w0': rwkv_w0[l], 'rwkv_w2': rwkv_w2[l], 'rwkv_a0': rwkv_a0[l],
              'rwkv_a2': rwkv_a2[l], 'rwkv_g2': rwkv_g2[l], 'rwkv_kk': rwkv_kk[l], 'rwkv_ka': rwkv_ka[l],
              'rwkv_rk': rwkv_rk[l], 'rwkv_ln_w': rwkv_ln_w[l], 'rwkv_ln_b': rwkv_ln_b[l]}
        last = l == depth - 1
        mod = _ada(c_all, w_ada[l], b_ada[l]).reshape(bp + bs, 6, d)
        mod_p = [mod[:bp, i:i + 1] for i in range(6)]
        mod_s = [mod[bp:, i][None] for i in range(6)]
        w_proj = _proj_weights(w_in[l])
        wb, wo = w_branch[l].astype(BF16), w_out[l].astype(BF16)
        wfi, wfo = w_ffn_in[l].astype(BF16), w_ffn_out[l].astype(BF16)

        zp = _inproj(xp, norm_mix_g[l], mod_p[0], mod_p[1], w_proj, tm_p, tn)
        outs['rows_n_p'].append(zp[..., C_K:C_K + 4 * HEAD_DIM].reshape(bp, t, 4, HEAD_DIM))
        outs['rows_d_p'].append(zp[..., C_K + 3 * LANE:C_K + 3 * LANE + 3 * HEAD_DIM].reshape(bp, t, 3, HEAD_DIM))
        wn = min(NSA_WINDOW, t)
        outs['win_p'].append(zp[:, t - wn:, C_K + 2 * LANE:C_K + 3 * LANE].reshape(bp, wn, 2, HEAD_DIM))
        outs['shift_p'].append(zp[:, t - 1, C_RWKV:C_RWKV + 4 * RWKV_WIDTH])
        n16 = t // NSA_CMP_STRIDE
        r = jnp.stack([zp[..., C_K:C_K + HEAD_DIM].reshape(bp, n16, NSA_CMP_STRIDE * HEAD_DIM),
                       zp[..., C_K + HEAD_DIM:C_K + 2 * HEAD_DIM].reshape(bp, n16, NSA_CMP_STRIDE * HEAD_DIM)], axis=1)
        w1a, w1b, pos_a, pos_b, w2 = _compress_weights_rows(nsa_cmp_w1[l], nsa_cmp_w2[l], nsa_cmp_pos[l])
        cb, ct = _nsa_compress(r, pos_a, pos_b, w1a, w1b, w2)
        kb, kt = _kprep(zp)
        o_att = _prompt_attention(zp, kb, kt, cb, ct)
        o_c = _gmlp(zp, lp['gmlp_ln_g'], lp['gmlp_ln_b'], lp['gmlp_ws'], lp['gmlp_bs'], _pick(t, (512, 256, 128)))
        pre = _rwkv_pre(zp, jnp.zeros((bp, 1, 4 * RWKV_WIDTH), F32), lp, _pick(t, (256, 128)))
        o_r, wkv_p = _rwkv_scan(*pre, lp['rwkv_ln_w'], lp['rwkv_ln_b'],
                                jnp.zeros((bp, N_HEADS, HEAD_DIM, HEAD_DIM), F32))
        outs['wkv_p'].append(wkv_p)
        xp = _merge(zp, o_att, o_c, o_r, wb, wo, xp, mod_p[2], tm_p)
        xp = _ffn(xp, norm_ffn_g[l], mod_p[3], mod_p[4], mod_p[5], wfi, wfo, final_norm_g, last, tm_f, tf)

        zs = _inproj(xs, norm_mix_g[l], mod_s[0], mod_s[1], w_proj, bs, tn)
        z2 = zs[0]
        outs['rows_n_s'].append(z2[:, C_K:C_K + 4 * HEAD_DIM].reshape(bs, 1, 4, HEAD_DIM))
        outs['rows_d_s'].append(z2[:, C_K + 3 * LANE:C_K + 3 * LANE + 3 * HEAD_DIM].reshape(bs, 1, 3, HEAD_DIM))
        outs['shift_s'].append(z2[:, C_RWKV:C_RWKV + 4 * RWKV_WIDTH])
        heads = lambda c0: z2[:, c0:c0 + N_HEADS * LANE].reshape(bs, N_HEADS, LANE)[:, :, :HEAD_DIM]
        vecs = jnp.concatenate([heads(C_QN), heads(C_QD), heads(C_QI),
                                z2[:, C_K + LANE:C_K + LANE + 7 * HEAD_DIM].reshape(bs, 7, HEAD_DIM)], axis=1)
        cols = jnp.transpose(jnp.pad(vecs, ((0, 0), (0, LANE - vecs.shape[1]), (0, 0))), (0, 2, 1))
        misc = jnp.broadcast_to(z2[:, None, C_MISC:C_MISC + LANE], (bs, 8, LANE))
        idx = _sample_index(page_table, cols, misc, cache_d, l)
        width = idx.shape[-1]
        dsel = _sample_topk(idx.reshape(bs, width), min(DSA_TOPK, (past + 1) // 4)).reshape(bs, 1, width)
        cmp_w = _compress_weights_pairs(nsa_cmp_w1[l], nsa_cmp_w2[l], nsa_cmp_pos[l])
        o_t, win_new = _sample_attention(page_table, cols, misc, dsel, win_t, cmp_w, cache_n, cache_d, l)
        o_heads = jnp.transpose(o_t[:, :, 0:2 * N_HEADS], (0, 2, 1)).reshape(bs, 2 * N_HEADS * HEAD_DIM)
        outs['win_s'].append(jnp.transpose(win_new, (0, 3, 1, 2)))
        o_cs, v_cs, o_rs, wkv_s = _sample_mixers(z2, state_rwkv_shift[l], state_rwkv_wkv[l], lp, 8)
        outs['v_s'].append(v_cs.reshape(bs, 1, GMLP_WIDTH))
        outs['wkv_s'].append(wkv_s)
        xs = _merge(zs, o_heads[None], o_cs[None], o_rs[None], wb, wo, xs, mod_s[2], bs)
        xs = _ffn(xs, norm_ffn_g[l], mod_s[3], mod_s[4], mod_s[5], wfi, wfo, final_norm_g, last, bs, tf)

    st = lambda k: jnp.stack(outs[k])
    return (xp, xs.reshape(bs, 1, d), st('rows_n_p'), st('rows_n_s'), st('rows_d_p'), st('rows_d_s'), st('win_p'),
            st('win_s'), st('v_s'), st('shift_p'), st('shift_s'), st('wkv_p'), st('wkv_s'))
```

```python
import functools

import numpy as np
import jax
import jax.numpy as jnp
from jax import lax
from jax.experimental import pallas as pl
from jax.experimental.pallas import tpu as pltpu

F32 = jnp.float32
BF16 = jnp.bfloat16
I32 = jnp.int32

HEAD_DIM = 64
N_HEADS = 4
NSA_CMP_LEN = 32
NSA_CMP_STRIDE = 16
NSA_SLC_LEN = 64
NSA_TOPN = 8
NSA_WINDOW = 512
DSA_TOPK = 256
GMLP_GROUPS = 4
GMLP_WIDTH = GMLP_GROUPS * HEAD_DIM
CHUNK = 128
RWKV_WIDTH = N_HEADS * HEAD_DIM
RWKV_GN_EPS = 64e-5
QB = 128
EPS = 1e-6
LN_EPS = 1e-5
NEG = -1e30
FORCE_BONUS = 1e4
LANE = 128
VMEM_LIMIT = 56 * 1024 * 1024

C_MERGE, C_RWKV, C_GMLP = 0, 4096, 5120
C_QN, C_QD, C_MISC, C_K = 5632, 6144, 6656, 6784
N_PROJ = 7296
K_PAIRS = 4
P_CMP, P_SLC, P_WIN, P_DSA = range(K_PAIRS)
ZI_Q, ZI_K, ZI_W, N_HP = 0, 512, 640, 768


def _cparams(sem):
    return pltpu.CompilerParams(dimension_semantics=sem, vmem_limit_bytes=VMEM_LIMIT)


def _dot(a, b):
    return jnp.dot(a, b, preferred_element_type=F32)


def _dot_nt(a, b):
    return lax.dot_general(a, b, (((1,), (1,)), ((), ())), preferred_element_type=F32)


def _dot_tn(a, b):
    return lax.dot_general(a, b, (((0,), (0,)), ((), ())), preferred_element_type=F32)


def _gelu_tanh(x):
    return 0.5 * x * (1.0 + jnp.tanh(np.sqrt(2.0 / np.pi).astype(np.float32) * (x + 0.044715 * (x * x * x))))


def _sigmoid(x):
    return 1.0 / (1.0 + jnp.exp(-x))


def _rms_mod(x, g, scale, shift):
    ms = jnp.mean(x * x, axis=-1, keepdims=True)
    return (x * lax.rsqrt(ms + EPS) * g) * (1.0 + scale) + shift


def _ada_kernel(c_ref, w_ref, b_ref, o_ref):
    c = c_ref[...]
    s = (c * _sigmoid(c)).astype(BF16)
    o_ref[...] = _dot(s, w_ref[...].astype(BF16)) + b_ref[...]


def _ada(c, w, b):
    m, d = c.shape
    n = w.shape[1]
    tn = 1536
    return pl.pallas_call(
        _ada_kernel,
        grid=(n // tn,),
        in_specs=[pl.BlockSpec((m, d), lambda j: (0, 0)),
                  pl.BlockSpec((d, tn), lambda j: (0, j)),
                  pl.BlockSpec((1, tn), lambda j: (0, j))],
        out_specs=pl.BlockSpec((m, tn), lambda j: (0, j)),
        out_shape=jax.ShapeDtypeStruct((m, n), F32),
        compiler_params=_cparams(("parallel",)),
        name="ada",
    )(c, w, b.reshape(1, n))


def _inproj_kernel(x_ref, g_ref, sh_ref, sc_ref, w_ref, o_ref):
    h = _rms_mod(x_ref[0], g_ref[...], sc_ref[0], sh_ref[0])
    o_ref[0] = _dot(h.astype(BF16), w_ref[...])


def _inproj(x, g, shift, scale, w, tm, tn):
    b, t, d = x.shape
    n = w.shape[1]
    tmod = tm if shift.shape[1] == t else 1
    mod_map = (lambda j, bi, i: (bi, i, 0)) if shift.shape[1] == t else (lambda j, bi, i: (bi, 0, 0))
    return pl.pallas_call(
        _inproj_kernel,
        grid=(n // tn, b, t // tm),
        in_specs=[pl.BlockSpec((1, tm, d), lambda j, bi, i: (bi, i, 0)),
                  pl.BlockSpec((1, d), lambda j, bi, i: (0, 0)),
                  pl.BlockSpec((1, tmod, d), mod_map),
                  pl.BlockSpec((1, tmod, d), mod_map),
                  pl.BlockSpec((d, tn), lambda j, bi, i: (0, j))],
        out_specs=pl.BlockSpec((1, tm, tn), lambda j, bi, i: (bi, i, j)),
        out_shape=jax.ShapeDtypeStruct((b, t, n), F32),
        compiler_params=_cparams(("parallel", "parallel", "parallel")),
        name="inproj",
    )(x, g.reshape(1, d), shift, scale, w)


def _split_bf16(x):
    hi = x.astype(BF16)
    return hi, (x - hi.astype(F32)).astype(BF16)


def _inproj_hp_kernel(x_ref, g_ref, sh_ref, sc_ref, wh_ref, wl_ref, o_ref):
    h_hi, h_lo = _split_bf16(_rms_mod(x_ref[0], g_ref[...], sc_ref[0], sh_ref[0]))
    o_ref[0] = _dot(h_hi, wh_ref[...]) + (_dot(h_hi, wl_ref[...]) + _dot(h_lo, wh_ref[...]))


def _inproj_hp(x, g, shift, scale, w_hi, w_lo, tm):
    b, t, d = x.shape
    n = w_hi.shape[1]
    tmod = tm if shift.shape[1] == t else 1
    mod_map = (lambda bi, i: (bi, i, 0)) if shift.shape[1] == t else (lambda bi, i: (bi, 0, 0))
    return pl.pallas_call(
        _inproj_hp_kernel,
        grid=(b, t // tm),
        in_specs=[pl.BlockSpec((1, tm, d), lambda bi, i: (bi, i, 0)),
                  pl.BlockSpec((1, d), lambda bi, i: (0, 0)),
                  pl.BlockSpec((1, tmod, d), mod_map),
                  pl.BlockSpec((1, tmod, d), mod_map),
                  pl.BlockSpec((d, n), lambda bi, i: (0, 0)),
                  pl.BlockSpec((d, n), lambda bi, i: (0, 0))],
        out_specs=pl.BlockSpec((1, tm, n), lambda bi, i: (bi, i, 0)),
        out_shape=jax.ShapeDtypeStruct((b, t, n), F32),
        compiler_params=_cparams(("parallel", "parallel")),
        name="inproj_hp",
    )(x, g.reshape(1, d), shift, scale, w_hi, w_lo)


def _merge_kernel(zm_ref, oa_ref, oc_ref, or_ref, wb_ref, wo_ref, x_ref, gate_ref, o_ref):
    bw = wb_ref.shape[1]
    d = x_ref.shape[-1]
    outs = (oa_ref[0, :, 0:bw], oa_ref[0, :, bw:2 * bw], oc_ref[0], or_ref[0])
    mixed = None
    for n, o in enumerate(outs):
        br = _dot(o.astype(BF16), wb_ref[n])
        term = _sigmoid(zm_ref[0, :, n * d:(n + 1) * d]) * br
        mixed = term if mixed is None else mixed + term
    y = _dot(mixed.astype(BF16), wo_ref[...])
    o_ref[0] = x_ref[0] + gate_ref[0] * y


def _merge(z, o_att, o_c, o_r, w_branch, w_out, x, gate, tm):
    b, t, d = x.shape
    nb, bw, _ = w_branch.shape
    tmod = tm if gate.shape[1] == t else 1
    mod_map = (lambda bi, i: (bi, i, 0)) if gate.shape[1] == t else (lambda bi, i: (bi, 0, 0))
    return pl.pallas_call(
        _merge_kernel,
        grid=(b, t // tm),
        in_specs=[pl.BlockSpec((1, tm, nb * d), lambda bi, i: (bi, i, C_MERGE // (nb * d))),
                  pl.BlockSpec((1, tm, 2 * bw), lambda bi, i: (bi, i, 0)),
                  pl.BlockSpec((1, tm, bw), lambda bi, i: (bi, i, 0)),
                  pl.BlockSpec((1, tm, bw), lambda bi, i: (bi, i, 0)),
                  pl.BlockSpec((nb, bw, d), lambda bi, i: (0, 0, 0)),
                  pl.BlockSpec((d, d), lambda bi, i: (0, 0)),
                  pl.BlockSpec((1, tm, d), lambda bi, i: (bi, i, 0)),
                  pl.BlockSpec((1, tmod, d), mod_map)],
        out_specs=pl.BlockSpec((1, tm, d), lambda bi, i: (bi, i, 0)),
        out_shape=jax.ShapeDtypeStruct((b, t, d), F32),
        compiler_params=_cparams(("parallel", "parallel")),
        name="merge",
    )(z, o_att, o_c, o_r, w_branch, w_out, x, gate)


def _ffn_kernel(x_ref, g_ref, sh_ref, sc_ref, gate_ref, wg_ref, wu_ref, wd_ref, fg_ref, o_ref,
                h_scr, acc_scr, *, final_norm):
    k = pl.program_id(2)

    @pl.when(k == 0)
    def _():
        h_scr[...] = _rms_mod(x_ref[0], g_ref[...], sc_ref[0], sh_ref[0]).astype(BF16)
        acc_scr[...] = jnp.zeros_like(acc_scr)

    h = h_scr[...]
    gt = _dot(h, wg_ref[...])
    up = _dot(h, wu_ref[...])
    act = (gt * _sigmoid(gt)) * up
    acc_scr[...] += _dot(act.astype(BF16), wd_ref[...])

    @pl.when(k == pl.num_programs(2) - 1)
    def _():
        y = x_ref[0] + gate_ref[0] * acc_scr[...]
        if final_norm:
            ms = jnp.mean(y * y, axis=-1, keepdims=True)
            y = y * lax.rsqrt(ms + EPS) * fg_ref[...]
        o_ref[0] = y


def _ffn(x, g, shift, scale, gate, w_in, w_out, final_g, final_norm, tm, tf):
    b, t, d = x.shape
    ff = w_out.shape[0]
    nk = ff // tf
    tmod = tm if gate.shape[1] == t else 1
    mod_map = (lambda bi, i, k: (bi, i, 0)) if gate.shape[1] == t else (lambda bi, i, k: (bi, 0, 0))
    return pl.pallas_call(
        functools.partial(_ffn_kernel, final_norm=final_norm),
        grid=(b, t // tm, nk),
        in_specs=[pl.BlockSpec((1, tm, d), lambda bi, i, k: (bi, i, 0)),
                  pl.BlockSpec((1, d), lambda bi, i, k: (0, 0)),
                  pl.BlockSpec((1, tmod, d), mod_map),
                  pl.BlockSpec((1, tmod, d), mod_map),
                  pl.BlockSpec((1, tmod, d), mod_map),
                  pl.BlockSpec((d, tf), lambda bi, i, k: (0, k)),
                  pl.BlockSpec((d, tf), lambda bi, i, k: (0, nk + k)),
                  pl.BlockSpec((tf, d), lambda bi, i, k: (k, 0)),
                  pl.BlockSpec((1, d), lambda bi, i, k: (0, 0))],
        out_specs=pl.BlockSpec((1, tm, d), lambda bi, i, k: (bi, i, 0)),
        out_shape=jax.ShapeDtypeStruct((b, t, d), F32),
        scratch_shapes=[pltpu.VMEM((tm, d), BF16), pltpu.VMEM((tm, d), F32)],
        compiler_params=_cparams(("parallel", "parallel", "arbitrary")),
        name="ffn",
    )(x, g.reshape(1, d), shift, scale, gate, w_in, w_in, w_out, final_g.reshape(1, d))


def _cmp_kernel(r_ref, pa_ref, pb_ref, w1a_ref, w1b_ref, w2_ref, ob_ref, ot_ref):
    n16 = r_ref.shape[2]
    comp = []
    for z in range(2):
        r = r_ref[0, z]
        a = _dot((r + pa_ref[z]).astype(BF16), w1a_ref[z])
        b = _dot((r + pb_ref[z]).astype(BF16), w1b_ref[z])
        hid = _gelu_tanh(a + pltpu.roll(b, n16 - 1, axis=0))
        comp.append(_dot(hid.astype(BF16), w2_ref[z]))
    pair = jnp.concatenate(comp, axis=1)
    ob_ref[0] = pair.astype(BF16)
    ot_ref[0] = pair.T.astype(BF16)


def _nsa_compress(r, pos_a, pos_b, w1a, w1b, w2):
    b, _, n16, kd = r.shape
    hid = w1a.shape[-1]
    return pl.pallas_call(
        _cmp_kernel,
        grid=(b,),
        in_specs=[pl.BlockSpec((1, 2, n16, kd), lambda bi: (bi, 0, 0, 0)),
                  pl.BlockSpec((2, 1, kd), lambda bi: (0, 0, 0)),
                  pl.BlockSpec((2, 1, kd), lambda bi: (0, 0, 0)),
                  pl.BlockSpec((2, kd, hid), lambda bi: (0, 0, 0)),
                  pl.BlockSpec((2, kd, hid), lambda bi: (0, 0, 0)),
                  pl.BlockSpec((2, hid, HEAD_DIM), lambda bi: (0, 0, 0))],
        out_specs=[pl.BlockSpec((1, n16, 2 * HEAD_DIM), lambda bi: (bi, 0, 0)),
                   pl.BlockSpec((1, 2 * HEAD_DIM, n16), lambda bi: (bi, 0, 0))],
        out_shape=[jax.ShapeDtypeStruct((b, n16, 2 * HEAD_DIM), BF16),
                   jax.ShapeDtypeStruct((b, 2 * HEAD_DIM, n16), BF16)],
        compiler_params=_cparams(("parallel",)),
        name="nsa_compress",
    )(r, pos_a, pos_b, w1a, w1b, w2)


CK = 256
AUX_POS = 32


def _attn_aux(t):
    pos = np.arange(t)
    aux = np.zeros((t, LANE), np.float32)
    aux[pos, pos // NSA_SLC_LEN] = 1.0
    aux[:, AUX_POS] = pos % CK
    aux[:, AUX_POS + 1] = pos // CK
    return jnp.asarray(aux, BF16)


def _kprep_kernel(z_ref, aux_ref, ob_ref, ot_ref):
    x = z_ref[0]
    ob_ref[0, 0] = jnp.concatenate([x.astype(BF16), aux_ref[...]], axis=1)
    ot_ref[0, 0, 0] = x.T[HEAD_DIM:2 * HEAD_DIM].astype(BF16)


def _kprep(z):
    b, t, _ = z.shape
    return pl.pallas_call(
        _kprep_kernel,
        grid=(b, K_PAIRS, t // CK),
        in_specs=[pl.BlockSpec((1, CK, LANE), lambda bi, p, c: (bi, c, C_K // LANE + p)),
                  pl.BlockSpec((CK, LANE), lambda bi, p, c: (c, 0))],
        out_specs=[pl.BlockSpec((1, 1, CK, 2 * LANE), lambda bi, p, c: (bi, p, c, 0)),
                   pl.BlockSpec((1, 1, 1, HEAD_DIM, CK), lambda bi, p, c: (bi, p, c, 0, 0))],
        out_shape=[jax.ShapeDtypeStruct((b, K_PAIRS, t, 2 * LANE), BF16),
                   jax.ShapeDtypeStruct((b, K_PAIRS, t // CK, HEAD_DIM, CK), BF16)],
        compiler_params=_cparams(("parallel", "parallel", "parallel")),
        name="kprep",
    )(z, _attn_aux(t))


def _kprep_idx_kernel(z_ref, o_ref):
    x = z_ref[0]
    hi = x.astype(BF16).astype(F32)
    o_ref[0] = jnp.concatenate([hi + pltpu.roll(x - hi, HEAD_DIM, axis=1), hi], axis=1).astype(BF16)


def _kprep_idx(zi):
    b, t, _ = zi.shape
    tt = _pick(t, (512, 256, 128))
    return pl.pallas_call(
        _kprep_idx_kernel,
        grid=(b, t // tt),
        in_specs=[pl.BlockSpec((1, tt, LANE), lambda bi, i: (bi, i, ZI_K // LANE))],
        out_specs=pl.BlockSpec((1, tt, 2 * LANE), lambda bi, i: (bi, i, 0)),
        out_shape=jax.ShapeDtypeStruct((b, t, 2 * LANE), BF16),
        compiler_params=_cparams(("parallel", "parallel")),
        name="kprep_idx",
    )(zi)


INT_MIN = -2 ** 31


def _sortable_key(v):
    v = jnp.where(v == 0.0, 0.0, v)
    u = lax.bitcast_convert_type(v, I32)
    return jnp.where(u < 0, u ^ 0x7FFFFFFF, u)


_KEY_HALF_NEG = int(np.array(0.5 * NEG, np.float32).view(np.int32) ^ 0x7FFFFFFF)


def _lane_consts(q0):
    lane = lax.broadcasted_iota(I32, (1, N_HEADS * QB), 1)
    hl = lane >> 7
    q_pos = q0 + (lane & (QB - 1))
    slope = jnp.where(hl == 0, 2.0 ** -2, jnp.where(hl == 1, 2.0 ** -4, jnp.where(hl == 2, 2.0 ** -6, 2.0 ** -8)))
    return q_pos, slope.astype(F32)


def _tile_heads(x):
    return jnp.concatenate([x] * N_HEADS, axis=1)


def _place_heads(tiles):
    lane = lax.broadcasted_iota(I32, (QB, LANE), 1)
    out = []
    for t in range(N_HEADS // 2):
        out.append(jnp.where(lane < HEAD_DIM, pltpu.roll(tiles[2 * t], HEAD_DIM, axis=1), tiles[2 * t + 1]))
    return jnp.concatenate(out, axis=1)


def _attn_kernel(qn_ref, qd_ref, misc_ref, qi_ref, wi_ref, kb_ref, kt_ref, ki_ref, cb_ref, ct_ref, c2s_ref, o_ref,
                 key_scr, dsel_scr, *, seq_len, topk):
    t = seq_len
    q0 = pl.program_id(1) * QB
    nc = (q0 + QB + CK - 1) // CK
    hq = N_HEADS * QB
    scale = HEAD_DIM ** -0.5

    def stack_q(ref):
        return jnp.concatenate([ref[0, :, h * LANE:(h + 1) * LANE] for h in range(N_HEADS)], axis=0) * scale

    qn, qd = stack_q(qn_ref).astype(BF16), stack_q(qd_ref).astype(BF16)
    qi = stack_q(qi_ref)
    qi_cat = jnp.concatenate([(qi + pltpu.roll(qi, HEAD_DIM, axis=1)).astype(BF16),
                              (qi - qi.astype(BF16).astype(F32)).astype(BF16)], axis=1)
    misc_t = misc_ref[0].T
    wi_t = wi_ref[0].T
    q_pos, slope = _lane_consts(q0)

    n16 = cb_ref.shape[1]
    sc = _dot_nt(cb_ref[0], qn)
    cmp_end = lax.broadcasted_iota(I32, (n16, hq), 0) * NSA_CMP_STRIDE + (NSA_CMP_LEN - 1)
    d = q_pos - cmp_end
    mask = d >= 0
    sm = jnp.where(mask, sc - slope * d.astype(F32), NEG)
    e = jnp.where(mask, jnp.exp(sm - jnp.max(sm, axis=0, keepdims=True)), 0.0)
    l = jnp.sum(e, axis=0, keepdims=True)
    p = e / jnp.where(l > 0.0, l, 1.0)
    o_cmp_t = _dot(ct_ref[0], p.astype(BF16))
    psum_t = p[:, 0:QB]
    for h in range(1, N_HEADS):
        psum_t = psum_t + p[:, h * QB:(h + 1) * QB]
    imp_t = jnp.dot(c2s_ref[...], psum_t, preferred_element_type=F32, precision=lax.Precision.HIGHEST)

    nslc = c2s_ref.shape[0]
    jrow = lax.broadcasted_iota(I32, (nslc, QB), 0)
    qp = q0 + lax.broadcasted_iota(I32, (nslc, QB), 1)
    cur = qp >> 6
    adm = jrow * NSA_SLC_LEN <= qp
    forced = (jrow == 0) | (jrow == cur) | (jrow == cur - 1)
    score = jnp.where(adm, imp_t + jnp.where(forced, FORCE_BONUS, 0.0), NEG)
    rank = jnp.zeros((nslc, QB), I32)
    for j in range(nslc):
        row = score[j:j + 1, :]
        rank = rank + jnp.where((row > score) | ((row == score) & (jrow > j)), 1, 0)
    sel_bias = jnp.where((rank < min(NSA_TOPN, nslc)) & (score > 0.5 * NEG), 0.0, NEG)

    w_rows = [wi_t[h:h + 1, :] * (N_HEADS ** -0.5) for h in range(N_HEADS)]

    npair = (nc + 1) // 2

    def idx_body(c, carry):
        rows = pl.ds(pl.multiple_of(c * (2 * CK), 2 * CK), 2 * CK)
        lg = jnp.maximum(_dot_nt(ki_ref[0, rows, :], qi_cat), 0.0)
        idx = lg[:, 0:QB] * w_rows[0]
        for h in range(1, N_HEADS):
            idx = idx + lg[:, h * QB:(h + 1) * QB] * w_rows[h]
        kpos = c * (2 * CK) + lax.broadcasted_iota(I32, (2 * CK, QB), 0)
        causal = kpos <= q0 + lax.broadcasted_iota(I32, (2 * CK, QB), 1)
        key_scr[rows, :] = _sortable_key(jnp.where(causal, idx, NEG))
        return carry

    lax.fori_loop(0, npair, idx_body, 0)

    def count(pred_fn):
        def body(c, acc):
            rows = pl.ds(pl.multiple_of(c * (2 * CK), 2 * CK), 2 * CK)
            return acc + jnp.sum(jnp.where(pred_fn(key_scr[rows, :]), 1, 0), axis=0, keepdims=True)
        return lax.fori_loop(0, npair, body, jnp.zeros((1, QB), I32))

    def bit_body(i, ans):
        cand = ans + lax.shift_left(jnp.int32(1), 31 - i)
        return jnp.where(count(lambda k: k >= cand) >= topk, cand, ans)

    thr = lax.fori_loop(0, 32, bit_body, jnp.full((1, QB), INT_MIN, I32))
    need = (topk - count(lambda k: k > thr)).astype(F32)

    tri = jnp.where(lax.broadcasted_iota(I32, (CK, CK), 1) < lax.broadcasted_iota(I32, (CK, CK), 0), 1.0, 0.0).astype(BF16)

    def tie_body(c, running):
        rows = pl.ds(pl.multiple_of(c * CK, CK), CK)
        k = key_scr[rows, :]
        eq = k == thr
        eqf = jnp.where(eq, 1.0, 0.0)
        before = _dot(tri, eqf.astype(BF16)) + running
        sel = ((k > thr) | (eq & (before < need))) & (k > _KEY_HALF_NEG)
        dsel_scr[rows, :] = jnp.where(sel, 0.0, NEG)
        return running + jnp.sum(eqf, axis=0, keepdims=True)

    lax.fori_loop(0, nc, tie_body, jnp.zeros((1, QB), F32))

    sel_bias_t = jnp.concatenate([sel_bias, jnp.zeros((LANE - nslc, QB), F32)], axis=0).T
    lane = lax.broadcasted_iota(I32, (QB, LANE), 1)
    q_slc, q_dsa = [], []
    for h in range(N_HEADS):
        pos_cols = jnp.where(lane == AUX_POS, SLOPES[h], jnp.where(lane == AUX_POS + 1, SLOPES[h] * CK, 0.0))
        q_slc.append(sel_bias_t + pos_cols)
        q_dsa.append(pos_cols)
    qn_cat = jnp.concatenate([qn, jnp.concatenate(q_slc, axis=0).astype(BF16)], axis=1)
    qd_cat = jnp.concatenate([qd, jnp.concatenate(q_dsa, axis=0).astype(BF16)], axis=1)
    def online(state, s, vt):
        m_old, l_old, acc_old = state
        m_new = jnp.maximum(m_old, jnp.max(s, axis=0, keepdims=True))
        alpha = jnp.exp(m_old - m_new)
        e_ = jnp.exp(s - m_new)
        return m_new, alpha * l_old + jnp.sum(e_, axis=0, keepdims=True), alpha * acc_old + _dot(vt, e_.astype(BF16))

    def scores(c, diagonal):
        rows = pl.ds(pl.multiple_of(c * CK, CK), CK)
        s_slc = _dot_nt(kb_ref[0, P_SLC, rows, :], qn_cat)
        if diagonal:
            kpos = c * CK + lax.broadcasted_iota(I32, (CK, hq), 0)
            s_slc = jnp.where(kpos <= q_pos, s_slc, NEG)
        return s_slc, _dot_nt(kb_ref[0, P_DSA, rows, :], qd_cat) + _tile_heads(dsel_scr[rows, :])

    def flash_chunks(chunks, state, diagonal=False):
        sc = [scores(c, diagonal) for c in chunks]
        st_slc, st_dsa = state
        for c, (s_slc, s_dsa) in zip(chunks, sc):
            st_slc = online(st_slc, s_slc, kt_ref[0, P_SLC, c])
            st_dsa = online(st_dsa, s_dsa, kt_ref[0, P_DSA, c])
        return st_slc, st_dsa

    init = (jnp.full((1, hq), NEG, F32), jnp.zeros((1, hq), F32), jnp.zeros((HEAD_DIM, hq), F32))
    n_full = nc - 1
    state = lax.fori_loop(0, n_full // 2, lambda i, st: flash_chunks((2 * i, 2 * i + 1), st), (init, init))
    state = lax.cond(n_full % 2 == 1, lambda st: flash_chunks((n_full - 1,), st), lambda st: st, state)
    state = flash_chunks((nc - 1,), state, diagonal=True)

    def finish(st):
        m_, l_, acc = st
        return acc * jnp.where(m_ > 0.5 * NEG, 1.0 / l_, 0.0)

    o_slc_t, o_dsa_t = finish(state[0]), finish(state[1])

    wk = min(NSA_WINDOW + QB, t)
    ws = pl.multiple_of(jnp.clip(q0 - NSA_WINDOW, 0, t - wk), QB)
    kw = kb_ref[0, P_WIN, pl.ds(ws, wk), 0:LANE]
    dw = q_pos - (ws + lax.broadcasted_iota(I32, (wk, hq), 0))
    mask = (dw >= 0) & (dw <= NSA_WINDOW)
    sm = jnp.where(mask, _dot_nt(kw, qn) - slope * dw.astype(F32), NEG)
    mw = jnp.max(sm, axis=0, keepdims=True)
    e = jnp.exp(sm - mw)
    lw = jnp.sum(e, axis=0, keepdims=True)
    o_win_t = _dot_tn(kw, e.astype(BF16)) * jnp.where(mw > 0.5 * NEG, 1.0 / lw, 0.0)

    gates = _sigmoid(misc_t[0:3 * N_HEADS, :])
    nsa_tiles, dsa_tiles = [], []
    top = jnp.zeros((HEAD_DIM, QB), F32)
    vrows = slice(HEAD_DIM, 2 * HEAD_DIM)
    for h in range(N_HEADS):
        cols = slice(h * QB, (h + 1) * QB)
        on = (gates[3 * h:3 * h + 1, :] * o_cmp_t[vrows, cols] + gates[3 * h + 1:3 * h + 2, :] * o_slc_t[:, cols]
              + gates[3 * h + 2:3 * h + 3, :] * o_win_t[vrows, cols])
        nsa_tiles.append(jnp.concatenate([top, on], axis=0).T)
        dsa_tiles.append(jnp.concatenate([top, o_dsa_t[:, cols]], axis=0).T)
    o_ref[0, :, 0:N_HEADS * HEAD_DIM] = _place_heads(nsa_tiles)
    o_ref[0, :, N_HEADS * HEAD_DIM:2 * N_HEADS * HEAD_DIM] = _place_heads(dsa_tiles)


def _cmp_to_slc_t(n16, n_slc):
    start = np.arange(n16) * NSA_CMP_STRIDE
    bstart = np.arange(n_slc) * NSA_SLC_LEN
    ov = np.minimum(start[:, None] + NSA_CMP_LEN, bstart[None, :] + NSA_SLC_LEN) - np.maximum(start[:, None], bstart[None, :])
    return (np.clip(ov, 0, None) / NSA_CMP_LEN).T.astype(np.float32)


def _prompt_attention(z, zi, kb, kt, ki, cb, ct):
    b, t, _ = z.shape
    n16 = cb.shape[1]
    n_slc = -(-t // NSA_SLC_LEN)
    assert n_slc <= AUX_POS and t // CK <= 256 and t % (2 * CK) == 0
    hq = N_HEADS * QB
    qw = N_HEADS * LANE
    topk = min(DSA_TOPK, t // 4)
    c2s = jnp.asarray(_cmp_to_slc_t(n16, n_slc))
    qspec = lambda col: pl.BlockSpec((1, QB, qw), lambda bi, i: (bi, i, col // qw))
    tile = lambda col: pl.BlockSpec((1, QB, LANE), lambda bi, i: (bi, i, col // LANE))
    return pl.pallas_call(
        functools.partial(_attn_kernel, seq_len=t, topk=topk),
        grid=(b, t // QB),
        in_specs=[qspec(C_QN), qspec(C_QD), tile(C_MISC), qspec(ZI_Q), tile(ZI_W),
                  pl.BlockSpec((1, K_PAIRS, t, 2 * LANE), lambda bi, i: (bi, 0, 0, 0)),
                  pl.BlockSpec((1, K_PAIRS, t // CK, HEAD_DIM, CK), lambda bi, i: (bi, 0, 0, 0, 0)),
                  pl.BlockSpec((1, t, 2 * LANE), lambda bi, i: (bi, 0, 0)),
                  pl.BlockSpec((1, n16, LANE), lambda bi, i: (bi, 0, 0)),
                  pl.BlockSpec((1, LANE, n16), lambda bi, i: (bi, 0, 0)),
                  pl.BlockSpec((n_slc, n16), lambda bi, i: (0, 0))],
        out_specs=pl.BlockSpec((1, QB, 2 * N_HEADS * HEAD_DIM), lambda bi, i: (bi, i, 0)),
        out_shape=jax.ShapeDtypeStruct((b, t, 2 * N_HEADS * HEAD_DIM), F32),
        scratch_shapes=[pltpu.VMEM((t, QB), I32), pltpu.VMEM((t, QB), F32)],
        compiler_params=_cparams(("parallel", "parallel")),
        name="prompt_attention",
    )(z, z, z, zi, zi, kb, kt, ki, cb, ct, c2s)


def _layer_norm(v, g, b):
    mu = jnp.mean(v, axis=-1, keepdims=True)
    var = jnp.mean(jnp.square(v - mu), axis=-1, keepdims=True)
    return (v - mu) * lax.rsqrt(var + LN_EPS) * g + b


def _gmlp_kernel(z_ref, lng_ref, lnb_ref, ws_ref, bs_ref, o_ref):
    tc = z_ref.shape[1]
    c = ws_ref.shape[1]
    tril = lax.broadcasted_iota(I32, (c, c), 1) <= lax.broadcasted_iota(I32, (c, c), 0)
    lane_g = lax.broadcasted_iota(I32, (1, GMLP_WIDTH), 1) >> 6
    ws = [jnp.where(tril, ws_ref[g], 0.0).astype(BF16) for g in range(GMLP_GROUPS)]
    for ci in range(tc // c):
        zg = _gelu_tanh(z_ref[0, ci * c:(ci + 1) * c, :])
        u = zg[:, 0:GMLP_WIDTH]
        v = _layer_norm(zg[:, GMLP_WIDTH:2 * GMLP_WIDTH], lng_ref[...], lnb_ref[...]).astype(BF16)
        s = bs_ref[...]
        for g in range(GMLP_GROUPS):
            s = s + jnp.where(lane_g == g, _dot(ws[g], v), 0.0)
        o_ref[0, ci * c:(ci + 1) * c, :] = u * s


def _gmlp(z, ln_g, ln_b, w_s, b_s, tc):
    b, t, _ = z.shape
    c = w_s.shape[1]
    bs_exp = jnp.repeat(b_s.T, HEAD_DIM, axis=1)
    return pl.pallas_call(
        _gmlp_kernel,
        grid=(b, t // tc),
        in_specs=[pl.BlockSpec((1, tc, 2 * GMLP_WIDTH), lambda bi, i: (bi, i, C_GMLP // (2 * GMLP_WIDTH))),
                  pl.BlockSpec((1, GMLP_WIDTH), lambda bi, i: (0, 0)),
                  pl.BlockSpec((1, GMLP_WIDTH), lambda bi, i: (0, 0)),
                  pl.BlockSpec((GMLP_GROUPS, c, c), lambda bi, i: (0, 0, 0)),
                  pl.BlockSpec((c, GMLP_WIDTH), lambda bi, i: (0, 0))],
        out_specs=pl.BlockSpec((1, tc, GMLP_WIDTH), lambda bi, i: (bi, i, 0)),
        out_shape=jax.ShapeDtypeStruct((b, t, GMLP_WIDTH), F32),
        compiler_params=_cparams(("parallel", "parallel")),
        name="gmlp",
    )(z, ln_g.reshape(1, -1), ln_b.reshape(1, -1), w_s, bs_exp)


def _head_sum(x):
    lane_h = lax.broadcasted_iota(I32, (1, x.shape[-1]), 1) >> 6
    out = jnp.zeros_like(x)
    for h in range(x.shape[-1] // HEAD_DIM):
        msk = lane_h == h
        out = out + jnp.where(msk, jnp.sum(jnp.where(msk, x, 0.0), axis=-1, keepdims=True), 0.0)
    return out


def _softplus(x):
    return jnp.maximum(x, 0.0) + jnp.log1p(jnp.exp(-jnp.abs(x)))


def _rwkv_features(f, w0, w2, a0, a2, g2, kkw, ka, rk):
    wd = RWKV_WIDTH
    r, k, v = f[:, 0:wd], f[:, wd:2 * wd], f[:, 2 * wd:3 * wd]
    wl, al, gl = f[:, 3 * wd:3 * wd + 64], f[:, 3 * wd + 64:3 * wd + 128], f[:, 3 * wd + 128:3 * wd + 256]
    w_log = -_softplus(-(w0 + _dot(jnp.tanh(wl).astype(BF16), w2))) - 0.5
    log_decay = -jnp.exp(w_log)
    a = _sigmoid(a0 + _dot(al.astype(BF16), a2))
    g = _dot(_sigmoid(gl).astype(BF16), g2)
    kk = k * kkw
    kk = kk * lax.rsqrt(_head_sum(kk * kk) + 1e-12)
    k = k * (1.0 + (a - 1.0) * ka)
    bonus = _head_sum(r * k * rk) * v
    return r, log_decay, k, v, kk, kk * a, g, bonus


RWKV_CHUNK = 64


def _rwkv_pre_kernel(f_ref, prev_ref, shift_ref, mu_ref, w0_ref, w2_ref, a0_ref, a2_ref, g2_ref, kkw_ref, ka_ref, rk_ref,
                     kh_o, rh_o, kb_o, bb_o, kbe_o, bbe_o, v_o, pe_o, g_o, bonus_o, *, chunk):
    feat = f_ref[0]
    tm = feat.shape[0]
    first = jnp.where(pl.program_id(1) == 0, shift_ref[0], prev_ref[0, 7:8, :])
    prev = jnp.where(lax.broadcasted_iota(I32, (tm, 1), 0) == 0, first, pltpu.roll(feat, 1, axis=0))
    f = feat + mu_ref[...] * (prev - feat)
    r, lw, k, v, kk, be, g, bonus = _rwkv_features(f, w0_ref[...], w2_ref[...], a0_ref[...], a2_ref[...], g2_ref[...],
                                                   kkw_ref[...], ka_ref[...], rk_ref[...])
    row = lax.broadcasted_iota(I32, (tm, tm), 0)
    col = lax.broadcasted_iota(I32, (tm, tm), 1)
    chunk_start = (row // chunk) * chunk
    cum = _hdot(jnp.where((col >= chunk_start) & (col <= row), 1.0, 0.0), lw)
    cum_end = _hdot(jnp.where(col == chunk_start + (chunk - 1), 1.0, 0.0), cum)
    down, to_end = jnp.exp(-cum), jnp.exp(cum_end - cum)
    for o, x in ((kh_o, kk * jnp.exp(cum - lw)), (rh_o, r * jnp.exp(cum)), (kb_o, k * down), (bb_o, be * down),
                 (kbe_o, k * to_end), (bbe_o, be * to_end), (v_o, v), (pe_o, jnp.exp(cum_end)), (g_o, g), (bonus_o, bonus)):
        o[0] = x.astype(o.dtype)


def _rwkv_pre(z, shift_prev, lp, tm):
    b, t, _ = z.shape
    wd = RWKV_WIDTH
    pw = 4 * wd
    vec = lambda n: pl.BlockSpec((1, n), lambda bi, i: (0, 0))
    mat = lambda r, c: pl.BlockSpec((r, c), lambda bi, i: (0, 0))
    fl = jax.ShapeDtypeStruct((b, t, wd), F32)
    hl = jax.ShapeDtypeStruct((b, t, wd), BF16)
    fspec = pl.BlockSpec((1, tm, wd), lambda bi, i: (bi, i, 0))
    return pl.pallas_call(
        functools.partial(_rwkv_pre_kernel, chunk=min(RWKV_CHUNK, t)),
        grid=(b, t // tm),
        in_specs=[pl.BlockSpec((1, tm, pw), lambda bi, i: (bi, i, C_RWKV // pw)),
                  pl.BlockSpec((1, 8, pw), lambda bi, i: (bi, jnp.maximum(i * (tm // 8) - 1, 0), C_RWKV // pw)),
                  pl.BlockSpec((1, 1, pw), lambda bi, i: (bi, 0, 0)),
                  vec(pw), vec(wd), mat(64, wd), vec(wd), mat(64, wd), mat(128, wd), vec(wd), vec(wd), vec(wd)],
        out_specs=[fspec] * 10,
        out_shape=[hl] * 7 + [fl] * 3,
        compiler_params=_cparams(("parallel", "parallel")),
        name="rwkv_pre",
    )(z, z, shift_prev, lp['rwkv_mu'].reshape(1, pw), lp['rwkv_w0'].reshape(1, wd), lp['rwkv_w2'].astype(BF16),
      lp['rwkv_a0'].reshape(1, wd), lp['rwkv_a2'].astype(BF16), lp['rwkv_g2'].astype(BF16),
      lp['rwkv_kk'].reshape(1, wd), lp['rwkv_ka'].reshape(1, wd), lp['rwkv_rk'].reshape(1, wd))


def _hdot(a, b):
    return jnp.dot(a, b, preferred_element_type=F32, precision=lax.Precision.HIGHEST)


def _hdot_nt(a, b):
    return lax.dot_general(a, b, (((1,), (1,)), ((), ())), preferred_element_type=F32, precision=lax.Precision.HIGHEST)


def _hdot_tn(a, b):
    return lax.dot_general(a, b, (((0,), (0,)), ((), ())), preferred_element_type=F32, precision=lax.Precision.HIGHEST)


def _group_norm_heads(y):
    mu = jnp.mean(y, axis=-1, keepdims=True)
    var = jnp.mean(jnp.square(y - mu), axis=-1, keepdims=True)
    return (y - mu) * lax.rsqrt(var + RWKV_GN_EPS)


def _rwkv_scan_kernel(kh_ref, rh_ref, kb_ref, bb_ref, kbe_ref, bbe_ref, v_ref, pe_ref, g_ref, bonus_ref, lnw_ref, lnb_ref,
                      s0_ref, o_ref, s_ref):
    @pl.when(pl.program_id(1) == 0)
    def _():
        s_ref[...] = s0_ref[...]

    bt, c, wd = kh_ref.shape
    n = N_HEADS * c
    row = lax.broadcasted_iota(I32, (n, wd), 0)
    col = lax.broadcasted_iota(I32, (n, wd), 1)
    same_head = (row // c) == (col // HEAD_DIM)
    tpos, jpos = row % c, col % HEAD_DIM
    strict, incl = same_head & (jpos < tpos), same_head & (jpos <= tpos)
    eye = jnp.where(same_head & (jpos == tpos), 1.0, 0.0)
    bf = lambda x: x.astype(BF16)
    block_diag = lambda ref, i: jnp.where(same_head, jnp.concatenate([ref[i]] * N_HEADS, axis=0), jnp.zeros((), BF16))
    for i in range(bt):
        kh, rh, kb, bb, kbe, bbe, v = (block_diag(ref, i) for ref in (kh_ref, rh_ref, kb_ref, bb_ref, kbe_ref, bbe_ref, v_ref))
        g1 = _dot_nt(jnp.concatenate([kh, rh], axis=0), jnp.concatenate([bb, kb], axis=0))
        a_kb = jnp.where(strict, g1[0:n, 0:n], 0.0)
        a_kk = jnp.where(strict, g1[0:n, n:2 * n], 0.0)
        a_rb = jnp.where(incl, g1[n:2 * n, 0:n], 0.0)
        a_rk = jnp.where(incl, g1[n:2 * n, n:2 * n], 0.0)
        s0 = s_ref[i]
        s0b = bf(s0)
        z = _dot_nt(kh, s0b) + _dot(bf(a_kk), v)
        y = _dot_nt(rh, s0b) + _dot(bf(a_rk), v)
        p = bf(-a_kb)
        tinv = eye - a_kb
        p = bf(_dot(p, p))
        for _ in range(int(np.log2(c)) - 2):
            both = _dot(jnp.concatenate([bf(tinv), p], axis=0), p)
            tinv = tinv + both[0:n]
            p = bf(both[n:2 * n])
        tinv = tinv + _dot(bf(tinv), p)
        u = bf(-_dot(bf(tinv), bf(z)))
        y = y + _dot(bf(a_rb), u)
        s_ref[i] = s0 * pe_ref[i, 0:1, :] + _dot_tn(v, kbe) + _dot_tn(u, bbe)
        y_flat = y[0:c]
        for h in range(1, N_HEADS):
            y_flat = y_flat + y[h * c:(h + 1) * c]
        mu = _head_sum(y_flat) * (1.0 / HEAD_DIM)
        yc = y_flat - mu
        yn = yc * lax.rsqrt(_head_sum(yc * yc) * (1.0 / HEAD_DIM) + RWKV_GN_EPS)
        o_ref[i] = (yn * lnw_ref[...] + lnb_ref[...] + bonus_ref[i]) * g_ref[i]


def _rwkv_scan(kh, rh, kb, bb, kbe, bbe, v, pe, g, bonus, ln_w, ln_b, s0):
    b, t, wd = kh.shape
    c = min(RWKV_CHUNK, t)
    assert c == HEAD_DIM
    bt = _pick(b, (2, 1))
    fspec = pl.BlockSpec((bt, c, wd), lambda bi, i: (bi, i, 0))
    sspec = pl.BlockSpec((bt, wd, wd), lambda bi, i: (bi, 0, 0))
    vec = pl.BlockSpec((1, wd), lambda bi, i: (0, 0))
    return pl.pallas_call(
        _rwkv_scan_kernel,
        grid=(b // bt, t // c),
        in_specs=[fspec] * 10 + [vec, vec, sspec],
        out_specs=[fspec, sspec],
        out_shape=[jax.ShapeDtypeStruct((b, t, wd), F32), jax.ShapeDtypeStruct((b, wd, wd), F32)],
        compiler_params=_cparams(("parallel", "arbitrary")),
        name="rwkv_scan",
    )(kh, rh, kb, bb, kbe, bbe, v, pe, g, bonus, ln_w.reshape(1, wd), ln_b.reshape(1, wd), s0)


COL_QN, COL_QD, COL_QI = 0, 4, 8
COL_KSLC, COL_VSLC, COL_KWIN, COL_VWIN, COL_KD, COL_VD, COL_KI = 12, 13, 14, 15, 16, 17, 18
SLOPES = tuple(2.0 ** (-8.0 * (h + 1) / N_HEADS) for h in range(N_HEADS))


def _qcols(cols, first):
    return [jnp.broadcast_to(cols[:, first + h:first + h + 1] * (HEAD_DIM ** -0.5), (HEAD_DIM, LANE)) for h in range(N_HEADS)]


def _col_dot(mat_t, qb):
    return jnp.sum(mat_t * qb, axis=0, keepdims=True)


def _sidx_kernel(pt_ref, cols_ref, misc_ref, *refs, n_pages, page):
    pages, o_ref = refs[:n_pages], refs[n_pages]
    cols = cols_ref[0]
    qi = _qcols(cols, COL_QI)
    w = [misc_ref[0, 0:1, 12 + h:13 + h] * (N_HEADS ** -0.5) for h in range(N_HEADS)]

    def index_of(kt, qs):
        idx = None
        for h in range(N_HEADS):
            term = jnp.maximum(_col_dot(kt, qs[h]), 0.0) * w[h]
            idx = term if idx is None else idx + term
        return idx

    for p in range(n_pages):
        o_ref[0, 0:1, p * page:(p + 1) * page] = index_of(pages[p][0, 0, 0], qi)
    idx_self = index_of(cols[:, COL_KI:COL_KI + 1], [q[:, 0:1] for q in qi])
    lane = lax.broadcasted_iota(I32, (1, LANE), 1)
    o_ref[0, 0:1, n_pages * page:n_pages * page + LANE] = jnp.where(lane == 0, idx_self, NEG)


def _sample_index(page_table, cols, misc, cache_d, layer):
    b, n_pages = page_table.shape
    page = cache_d.shape[-1]
    width = n_pages * page + LANE
    kern = functools.partial(_sidx_kernel, n_pages=n_pages, page=page)
    page_spec = lambda p: pl.BlockSpec((1, 1, 1, HEAD_DIM, page), lambda bi, pt: (layer, pt[bi, p], 2, 0, 0))
    gs = pltpu.PrefetchScalarGridSpec(
        num_scalar_prefetch=1, grid=(b,),
        in_specs=[pl.BlockSpec((1, HEAD_DIM, LANE), lambda bi, pt: (bi, 0, 0)),
                  pl.BlockSpec((1, 8, LANE), lambda bi, pt: (bi, 0, 0))] + [page_spec(p) for p in range(n_pages)],
        out_specs=pl.BlockSpec((1, 1, width), lambda bi, pt: (bi, 0, 0)))
    return pl.pallas_call(kern, grid_spec=gs, out_shape=jax.ShapeDtypeStruct((b, 1, width), F32),
                          compiler_params=_cparams(("parallel",)), name="sample_index",
                          )(page_table, cols, misc, *([cache_d] * n_pages))


def _stopk_kernel(idx_ref, o_ref, *, topk):
    key = _sortable_key(idx_ref[...])
    rows, width = key.shape

    def bit_body(i, ans):
        cand = ans + lax.shift_left(jnp.int32(1), 31 - i)
        cnt = jnp.sum(jnp.where(key >= cand, 1, 0), axis=1, keepdims=True)
        return jnp.where(cnt >= topk, cand, ans)

    thr = lax.fori_loop(0, 32, bit_body, jnp.full((rows, 1), INT_MIN, I32))
    need = (topk - jnp.sum(jnp.where(key > thr, 1, 0), axis=1, keepdims=True)).astype(F32)
    tri = jnp.where(lax.broadcasted_iota(I32, (LANE, LANE), 0) < lax.broadcasted_iota(I32, (LANE, LANE), 1), 1.0, 0.0).astype(BF16)
    running = jnp.zeros((rows, 1), F32)
    for c in range(width // LANE):
        k = key[:, c * LANE:(c + 1) * LANE]
        eq = k == thr
        eqf = jnp.where(eq, 1.0, 0.0)
        before = _dot(eqf.astype(BF16), tri) + running
        sel = ((k > thr) | (eq & (before < need))) & (k > _KEY_HALF_NEG)
        o_ref[:, c * LANE:(c + 1) * LANE] = jnp.where(sel, 1.0, 0.0)
        running = running + jnp.sum(eqf, axis=1, keepdims=True)


def _sample_topk(idx, topk):
    return pl.pallas_call(functools.partial(_stopk_kernel, topk=topk),
                          out_shape=jax.ShapeDtypeStruct(idx.shape, F32),
                          compiler_params=pltpu.CompilerParams(vmem_limit_bytes=VMEM_LIMIT), name="sample_topk")(idx)


def _sattn_kernel(pt_ref, cols_ref, misc_ref, dsel_ref, win_ref, w1a_ref, w1b_ref, posa_ref, posb_ref, w2_ref, c2s_ref,
                  exp_ref, *refs, n_pages, page):
    nsa_pages, dsa_pages = refs[:n_pages], refs[n_pages:2 * n_pages]
    o_ref, win_o_ref, xt_scr = refs[2 * n_pages:2 * n_pages + 3]
    past = n_pages * page
    q_pos = past
    cols = cols_ref[0]
    qn, qd = _qcols(cols, COL_QN), _qcols(cols, COL_QD)
    newcol = lambda j: cols[:, j:j + 1]
    gates = _sigmoid(misc_ref[0, 0:1, 0:3 * N_HEADS])
    gate = lambda h, c: gates[:, 3 * h + c:3 * h + c + 1]

    def attend(score_chunks, value_chunks, masks, dists, self_k, self_v, self_mask, q):
        slope = q[1]

        def fold(x, op):
            out = x[:, 0:LANE]
            for c in range(1, x.shape[1] // LANE):
                out = op(out, x[:, c * LANE:(c + 1) * LANE])
            return out

        sm = [jnp.where(m, s - slope * d.astype(F32), NEG) for s, m, d in zip(score_chunks, masks, dists)]
        s_self = jnp.where(self_mask, jnp.sum(self_k * q[0][:, 0:1], axis=0, keepdims=True), NEG)
        mx_el = None
        for s in sm:
            f = fold(s, jnp.maximum)
            mx_el = f if mx_el is None else jnp.maximum(mx_el, f)
        mx = jnp.maximum(s_self, jnp.max(mx_el, axis=1, keepdims=True))
        e_self = jnp.where(self_mask, jnp.exp(s_self - mx), 0.0)
        l_el, acc = None, None
        for s, m, vt in zip(sm, masks, value_chunks):
            e = jnp.where(m, jnp.exp(s - mx), 0.0)
            f, part = fold(e, jnp.add), fold(vt * e, jnp.add)
            l_el = f if l_el is None else l_el + f
            acc = part if acc is None else acc + part
        l = e_self + jnp.sum(l_el, axis=1, keepdims=True)
        o = jnp.sum(acc, axis=1, keepdims=True) + e_self * self_v
        return o / jnp.where(l > 0.0, l, 1.0)

    for p in range(n_pages):
        xt_scr[p * page:(p + 1) * page, :] = nsa_pages[p][0, 0, 0:2].reshape(2 * HEAD_DIM, page).T
    n16 = past // NSA_CMP_STRIDE
    x_all = jnp.concatenate([xt_scr[pl.ds(r, n16, stride=NSA_CMP_STRIDE), :] for r in range(NSA_CMP_STRIDE)], axis=1)
    a = _dot((x_all + posa_ref[...]).astype(BF16), w1a_ref[...])
    b = _dot((x_all + posb_ref[...]).astype(BF16), w1b_ref[...])
    hid = _gelu_tanh(a + pltpu.roll(b, n16 - 1, axis=0))
    comp_t = _dot_nt(w2_ref[...], hid.astype(BF16))
    kc_t, vc_t = comp_t[0:HEAD_DIM], comp_t[HEAD_DIM:2 * HEAD_DIM]

    blk = lax.broadcasted_iota(I32, (1, n16), 1)
    d_cmp = q_pos - (blk * NSA_CMP_STRIDE + NSA_CMP_LEN - 1)
    m_cmp = d_cmp >= 0
    o_cmp, psum = [], None
    for h in range(N_HEADS):
        sm = jnp.where(m_cmp, _col_dot(kc_t, qn[h][:, 0:1]) - SLOPES[h] * d_cmp.astype(F32), NEG)
        e = jnp.where(m_cmp, jnp.exp(sm - jnp.max(sm, axis=1, keepdims=True)), 0.0)
        l = jnp.sum(e, axis=1, keepdims=True)
        pr = e / jnp.where(l > 0.0, l, 1.0)
        o_cmp.append(jnp.sum(vc_t * pr, axis=1, keepdims=True))
        psum = pr if psum is None else psum + pr

    n_slc = past // NSA_SLC_LEN + 1
    imp = _hdot(jnp.broadcast_to(psum, (8, n16)), c2s_ref[...])[0:1]
    j = lax.broadcasted_iota(I32, (1, LANE), 1)
    cur = q_pos // NSA_SLC_LEN
    forced = (j == 0) | (j == cur) | (j == cur - 1)
    score = jnp.where((j * NSA_SLC_LEN <= q_pos) & (j < n_slc), imp + jnp.where(forced, FORCE_BONUS, 0.0), NEG)
    srow = jnp.broadcast_to(score, (LANE, LANE))
    scol = srow.T
    jp = lax.broadcasted_iota(I32, (LANE, LANE), 0)
    jj = lax.broadcasted_iota(I32, (LANE, LANE), 1)
    rank = jnp.sum(jnp.where((scol > srow) | ((scol == srow) & (jp < jj)), 1, 0), axis=0, keepdims=True)
    sel = jnp.where((rank < min(NSA_TOPN, n_slc)) & (score > 0.5 * NEG), 1.0, 0.0)
    sel_pos = _dot(jnp.broadcast_to(sel, (8, LANE)).astype(BF16), exp_ref[...])[0:1]

    pos = lax.broadcasted_iota(I32, (1, page), 1)
    dist = [q_pos - (p * page + pos) for p in range(n_pages)]
    m_slc = [sel_pos[:, p * page:(p + 1) * page] > 0.5 for p in range(n_pages)]
    m_dsa = [dsel_ref[0, 0:1, p * page:(p + 1) * page] > 0.5 for p in range(n_pages)]
    self_slc = sel_pos[:, past:past + 1] > 0.5
    self_dsa = dsel_ref[0, 0:1, past:past + 1] > 0.5
    wb = win_ref.shape[-1]
    d_win = q_pos - (past - wb + lax.broadcasted_iota(I32, (1, wb), 1))
    m_win = (d_win >= 0) & (d_win <= NSA_WINDOW) & (d_win <= q_pos)
    kw_t, vw_t = win_ref[0, 0, 0], win_ref[0, 0, 1]
    true11 = jnp.full((1, 1), True)
    outs = []
    for h in range(N_HEADS):
        qh = (qn[h], SLOPES[h])
        o_slc = attend([_col_dot(nsa_pages[p][0, 0, 2], qn[h]) for p in range(n_pages)],
                       [nsa_pages[p][0, 0, 3] for p in range(n_pages)], m_slc, dist,
                       newcol(COL_KSLC), newcol(COL_VSLC), self_slc, qh)
        o_win = attend([_col_dot(kw_t, qn[h][:, 0:1])], [vw_t], [m_win], [d_win],
                       newcol(COL_KWIN), newcol(COL_VWIN), true11, qh)
        outs.append(gate(h, 0) * o_cmp[h] + gate(h, 1) * o_slc + gate(h, 2) * o_win)
    for h in range(N_HEADS):
        outs.append(attend([_col_dot(dsa_pages[p][0, 0, 0], qd[h]) for p in range(n_pages)],
                           [dsa_pages[p][0, 0, 1] for p in range(n_pages)], m_dsa, dist,
                           newcol(COL_KD), newcol(COL_VD), self_dsa, (qd[h], SLOPES[h])))
    lane = lax.broadcasted_iota(I32, (HEAD_DIM, LANE), 1)
    res = jnp.zeros((HEAD_DIM, LANE), F32)
    for i, o in enumerate(outs):
        res = jnp.where(lane == i, o, res)
    o_ref[0] = res

    lane_w = lax.broadcasted_iota(I32, (HEAD_DIM, wb), 1)
    for c, colj in enumerate((COL_KWIN, COL_VWIN)):
        win_o_ref[0, c] = jnp.where(lane_w == wb - 1, newcol(colj), pltpu.roll(win_ref[0, 0, c], wb - 1, axis=1))


def _sample_attention(page_table, cols, misc, dsel, win_t, cmp_w, cache_n, cache_d, layer):
    b, n_pages = page_table.shape
    page = cache_n.shape[-1]
    past = n_pages * page
    wb = win_t.shape[-1]
    n16 = past // NSA_CMP_STRIDE
    w1a, w1b, pos_a, pos_b, w2 = cmp_w
    n_slc = past // NSA_SLC_LEN + 1
    c2s = np.zeros((n16, LANE), np.float32)
    c2s[:, :n_slc] = _cmp_to_slc_t(n16, n_slc).T
    expand = (np.arange(past + LANE)[None, :] // NSA_SLC_LEN == np.arange(LANE)[:, None]) & (np.arange(past + LANE)[None, :] <= past)
    kern = functools.partial(_sattn_kernel, n_pages=n_pages, page=page)
    full = lambda shape: pl.BlockSpec(shape, lambda bi, pt: (0,) * len(shape))
    nspec = lambda p: pl.BlockSpec((1, 1, 4, HEAD_DIM, page), lambda bi, pt: (layer, pt[bi, p], 0, 0, 0))
    dspec = lambda p: pl.BlockSpec((1, 1, 2, HEAD_DIM, page), lambda bi, pt: (layer, pt[bi, p], 0, 0, 0))
    gs = pltpu.PrefetchScalarGridSpec(
        num_scalar_prefetch=1, grid=(b,),
        in_specs=[pl.BlockSpec((1, HEAD_DIM, LANE), lambda bi, pt: (bi, 0, 0)),
                  pl.BlockSpec((1, 8, LANE), lambda bi, pt: (bi, 0, 0)),
                  pl.BlockSpec((1, 1, past + LANE), lambda bi, pt: (bi, 0, 0)),
                  pl.BlockSpec((1, 1, 2, HEAD_DIM, wb), lambda bi, pt: (layer, bi, 0, 0, 0)),
                  full(w1a.shape), full(w1b.shape), full(pos_a.shape), full(pos_b.shape), full(w2.shape),
                  full(c2s.shape), full(expand.shape)]
                 + [nspec(p) for p in range(n_pages)] + [dspec(p) for p in range(n_pages)],
        out_specs=[pl.BlockSpec((1, HEAD_DIM, LANE), lambda bi, pt: (bi, 0, 0)),
                   pl.BlockSpec((1, 2, HEAD_DIM, wb), lambda bi, pt: (bi, 0, 0, 0))],
        scratch_shapes=[pltpu.VMEM((past, 2 * HEAD_DIM), F32)])
    return pl.pallas_call(
        kern, grid_spec=gs,
        out_shape=[jax.ShapeDtypeStruct((b, HEAD_DIM, LANE), F32), jax.ShapeDtypeStruct((b, 2, HEAD_DIM, wb), F32)],
        compiler_params=_cparams(("parallel",)), name="sample_attention",
    )(page_table, cols, misc, dsel, win_t, w1a, w1b, pos_a, pos_b, w2, jnp.asarray(c2s), jnp.asarray(expand, BF16),
      *([cache_n] * n_pages), *([cache_d] * n_pages))


def _smix_kernel(zg_ref, zr_ref, shift_ref, s_ref, lng_ref, lnb_ref, gw_ref, gb_ref, mu_ref, w0_ref, w2_ref, a0_ref,
                 a2_ref, g2_ref, kkw_ref, ka_ref, rk_ref, lnw_ref, lnb2_ref, oc_ref, vc_ref, or_ref, so_ref):
    bt = zg_ref.shape[0]
    zg = _gelu_tanh(zg_ref[...])
    v = _layer_norm(zg[:, GMLP_WIDTH:2 * GMLP_WIDTH], lng_ref[...], lnb_ref[...])
    vc_ref[...] = v
    oc_ref[...] = zg[:, 0:GMLP_WIDTH] * (v * gw_ref[...] + gb_ref[...])

    feat = zr_ref[...]
    f = feat + mu_ref[...] * (shift_ref[...] - feat)
    r, lw, k, vv, kk, be, g, bonus = _rwkv_features(f, w0_ref[...], w2_ref[...], a0_ref[...], a2_ref[...], g2_ref[...],
                                                    kkw_ref[...], ka_ref[...], rk_ref[...])
    w = jnp.exp(lw)
    pad = lambda x: jnp.concatenate([x, jnp.zeros((7, HEAD_DIM), F32)], axis=0)
    rows = []
    for i in range(bt):
        ys = []
        for h in range(N_HEADS):
            sl = slice(h * HEAD_DIM, (h + 1) * HEAD_DIM)
            s0 = s_ref[i, h]
            sa = -_hdot_nt(pad(kk[i:i + 1, sl]), s0)[0:1]
            lhs = jnp.concatenate([sa, vv[i:i + 1, sl], jnp.zeros((6, HEAD_DIM), F32)], axis=0)
            rhs = jnp.concatenate([be[i:i + 1, sl], k[i:i + 1, sl], jnp.zeros((6, HEAD_DIM), F32)], axis=0)
            s1 = s0 * w[i:i + 1, sl] + _hdot_tn(lhs, rhs)
            so_ref[i, h] = s1
            ys.append(_group_norm_heads(_hdot_nt(pad(r[i:i + 1, sl]), s1)[0:1]))
        rows.append(jnp.concatenate(ys, axis=1))
    y = jnp.concatenate(rows, axis=0)
    or_ref[...] = (y * lnw_ref[...] + lnb2_ref[...] + bonus) * g


def _sample_mixers(z, shift_prev, wkv_prev, lp, bt):
    b = z.shape[0]
    wd = RWKV_WIDTH
    pw = 4 * wd
    gw = jnp.repeat(lp['gmlp_ws'][:, 0, 0], HEAD_DIM).reshape(1, GMLP_WIDTH)
    gb = jnp.repeat(lp['gmlp_bs'][:, 0], HEAD_DIM).reshape(1, GMLP_WIDTH)
    vec = lambda n: pl.BlockSpec((1, n), lambda i: (0, 0))
    mat = lambda r, c: pl.BlockSpec((r, c), lambda i: (0, 0))
    row = lambda n: pl.BlockSpec((bt, n), lambda i: (i, 0))
    sspec = pl.BlockSpec((bt, N_HEADS, HEAD_DIM, HEAD_DIM), lambda i: (i, 0, 0, 0))
    fl = jax.ShapeDtypeStruct((b, wd), F32)
    return pl.pallas_call(
        _smix_kernel,
        grid=(b // bt,),
        in_specs=[pl.BlockSpec((bt, 2 * GMLP_WIDTH), lambda i: (i, C_GMLP // (2 * GMLP_WIDTH))),
                  pl.BlockSpec((bt, pw), lambda i: (i, C_RWKV // pw)),
                  row(pw), sspec, vec(wd), vec(wd), vec(wd), vec(wd),
                  vec(pw), vec(wd), mat(64, wd), vec(wd), mat(64, wd), mat(128, wd), vec(wd), vec(wd), vec(wd), vec(wd), vec(wd)],
        out_specs=[row(wd), row(wd), row(wd), sspec],
        out_shape=[fl, fl, fl, jax.ShapeDtypeStruct(wkv_prev.shape, F32)],
        compiler_params=_cparams(("parallel",)),
        name="sample_mixers",
    )(z, z, shift_prev, wkv_prev, lp['gmlp_ln_g'].reshape(1, wd), lp['gmlp_ln_b'].reshape(1, wd), gw, gb,
      lp['rwkv_mu'].reshape(1, pw), lp['rwkv_w0'].reshape(1, wd), lp['rwkv_w2'].astype(BF16),
      lp['rwkv_a0'].reshape(1, wd), lp['rwkv_a2'].astype(BF16), lp['rwkv_g2'].astype(BF16),
      lp['rwkv_kk'].reshape(1, wd), lp['rwkv_ka'].reshape(1, wd), lp['rwkv_rk'].reshape(1, wd),
      lp['rwkv_ln_w'].reshape(1, wd), lp['rwkv_ln_b'].reshape(1, wd))


_W_IN_COLS = (('q_nsa', 256), ('kv_nsa', 384), ('g_nsa', 12), ('q_dsa', 256), ('kv_dsa', 128), ('q_idx', 256),
              ('k_idx', 64), ('w_idx', 4), ('gmlp', 512), ('rwkv', 1024), ('merge', 4096))


def _proj_weights(w_in):
    d = w_in.shape[0]
    parts, off = {}, 0
    for name, width in _W_IN_COLS:
        parts[name] = w_in[:, off:off + width]
        off += width

    def pad_heads(w):
        w4 = w.reshape(d, N_HEADS, HEAD_DIM)
        return jnp.concatenate([w4, jnp.zeros_like(w4)], axis=-1).reshape(d, N_HEADS * LANE)

    zeros = lambda n: jnp.zeros((d, n), w_in.dtype)
    w = jnp.concatenate([parts['merge'], parts['rwkv'], parts['gmlp'], pad_heads(parts['q_nsa']), pad_heads(parts['q_dsa']),
                         parts['g_nsa'], zeros(LANE - 12), parts['kv_nsa'], parts['kv_dsa']], axis=1)
    w_idx = jnp.concatenate([pad_heads(parts['q_idx']), parts['k_idx'], zeros(LANE - HEAD_DIM),
                             parts['w_idx'], zeros(LANE - N_HEADS)], axis=1)
    return w.astype(BF16), _split_bf16(w_idx)


def _compress_weights_rows(w1, w2, pos):
    half = NSA_CMP_STRIDE * HEAD_DIM
    return (w1[:, :half].astype(BF16), w1[:, half:].astype(BF16),
            pos[:, :NSA_CMP_STRIDE].reshape(2, 1, half), pos[:, NSA_CMP_STRIDE:].reshape(2, 1, half), w2.astype(BF16))


def _compress_weights_pairs(w1, w2, pos):
    hid = w1.shape[-1]
    w1r = w1.reshape(2, 2, NSA_CMP_STRIDE, HEAD_DIM, hid)
    posr = pos.reshape(2, 2, NSA_CMP_STRIDE, HEAD_DIM)
    bigs, poss = [], []
    for half in range(2):
        big = jnp.zeros((NSA_CMP_STRIDE, 2, HEAD_DIM, 2, hid), w1.dtype)
        for c in range(2):
            big = big.at[:, c, :, c, :].set(w1r[c, half])
        bigs.append(big.reshape(NSA_CMP_STRIDE * 2 * HEAD_DIM, 2 * hid).astype(BF16))
        poss.append(jnp.transpose(posr[:, half], (1, 0, 2)).reshape(1, NSA_CMP_STRIDE * 2 * HEAD_DIM))
    w2b = jnp.zeros((2, hid, 2, HEAD_DIM), w2.dtype)
    for c in range(2):
        w2b = w2b.at[c, :, c, :].set(w2[c])
    return bigs[0], bigs[1], poss[0], poss[1], w2b.reshape(2 * hid, 2 * HEAD_DIM).T.astype(BF16)


def _pick(n, cands):
    for c in cands:
        if n % c == 0:
            return c
    return n


def kernel(x_prompt, x_sample, cache_nsa, cache_dsa, state_nsa_win, state_rwkv_shift, state_rwkv_wkv, page_table,
           c_prompt, c_sample, w_ada, b_ada, norm_mix_g, norm_ffn_g, w_in, nsa_cmp_w1, nsa_cmp_w2, nsa_cmp_pos,
           gmlp_ln_g, gmlp_ln_b, gmlp_ws, gmlp_bs, rwkv_mu, rwkv_w0, rwkv_w2, rwkv_a0, rwkv_a2, rwkv_g2, rwkv_kk,
           rwkv_ka, rwkv_rk, rwkv_ln_w, rwkv_ln_b, w_branch, w_out, w_ffn_in, w_ffn_out, final_norm_g):
    depth = w_ada.shape[0]
    bp, t, d = x_prompt.shape
    bs = x_sample.shape[0]
    assert x_sample.shape[1] == 1 and t % CK == 0 and t % (8 * NSA_CMP_STRIDE) == 0
    cache_n = jnp.transpose(cache_nsa, (0, 1, 3, 4, 2))
    cache_d = jnp.transpose(cache_dsa, (0, 1, 3, 4, 2))
    win_t = jnp.transpose(state_nsa_win, (0, 1, 3, 4, 2))
    past = page_table.shape[1] * cache_nsa.shape[2]
    c_all = jnp.concatenate([c_prompt, c_sample], axis=0)
    xp, xs = x_prompt, x_sample.reshape(1, bs, d)
    tm_p = _pick(t, (512, 256, 128))
    tm_f = _pick(t, (1024, 512, 256, 128))
    tf = _pick(w_ffn_out.shape[1], (256, 128))
    tn = _pick(N_PROJ, (2432,))
    outs = {k: [] for k in ('rows_n_p', 'rows_n_s', 'rows_d_p', 'rows_d_s', 'win_p', 'win_s', 'v_s', 'shift_p', 'shift_s',
                            'wkv_p', 'wkv_s')}
    for l in range(depth):
        lp = {'gmlp_ln_g': gmlp_ln_g[l], 'gmlp_ln_b': gmlp_ln_b[l], 'gmlp_ws': gmlp_ws[l], 'gmlp_bs': gmlp_bs[l],
              'rwkv_mu': rwkv_mu[l], 'rwkv_w0': rwkv_w0[l], 'rwkv_w2': rwkv_w2[l], 'rwkv_a0': rwkv_a0[l],
              'rwkv_a2': rwkv_a2[l], 'rwkv_g2': rwkv_g2[l], 'rwkv_kk': rwkv_kk[l], 'rwkv_ka': rwkv_ka[l],
              'rwkv_rk': rwkv_rk[l], 'rwkv_ln_w': rwkv_ln_w[l], 'rwkv_ln_b': rwkv_ln_b[l]}
        last = l == depth - 1
        mod = _ada(c_all, w_ada[l], b_ada[l]).reshape(bp + bs, 6, d)
        mod_p = [mod[:bp, i:i + 1] for i in range(6)]
        mod_s = [mod[bp:, i][None] for i in range(6)]
        w_proj, (wi_hi, wi_lo) = _proj_weights(w_in[l])
        wb, wo = w_branch[l].astype(BF16), w_out[l].astype(BF16)
        wfi, wfo = w_ffn_in[l].astype(BF16), w_ffn_out[l].astype(BF16)

        zp = _inproj(xp, norm_mix_g[l], mod_p[0], mod_p[1], w_proj, tm_p, tn)
        zi = _inproj_hp(xp, norm_mix_g[l], mod_p[0], mod_p[1], wi_hi, wi_lo, tm_p)
        outs['rows_n_p'].append(zp[..., C_K:C_K + 4 * HEAD_DIM].reshape(bp, t, 4, HEAD_DIM))
        outs['rows_d_p'].append(jnp.concatenate([zp[..., C_K + 3 * LANE:C_K + 4 * LANE], zi[..., ZI_K:ZI_K + HEAD_DIM]],
                                                axis=-1).reshape(bp, t, 3, HEAD_DIM))
        wn = min(NSA_WINDOW, t)
        outs['win_p'].append(zp[:, t - wn:, C_K + 2 * LANE:C_K + 3 * LANE].reshape(bp, wn, 2, HEAD_DIM))
        outs['shift_p'].append(zp[:, t - 1, C_RWKV:C_RWKV + 4 * RWKV_WIDTH])
        n16 = t // NSA_CMP_STRIDE
        r = jnp.stack([zp[..., C_K:C_K + HEAD_DIM].reshape(bp, n16, NSA_CMP_STRIDE * HEAD_DIM),
                       zp[..., C_K + HEAD_DIM:C_K + 2 * HEAD_DIM].reshape(bp, n16, NSA_CMP_STRIDE * HEAD_DIM)], axis=1)
        w1a, w1b, pos_a, pos_b, w2 = _compress_weights_rows(nsa_cmp_w1[l], nsa_cmp_w2[l], nsa_cmp_pos[l])
        cb, ct = _nsa_compress(r, pos_a, pos_b, w1a, w1b, w2)
        kb, kt = _kprep(zp)
        o_att = _prompt_attention(zp, zi, kb, kt, _kprep_idx(zi), cb, ct)
        o_c = _gmlp(zp, lp['gmlp_ln_g'], lp['gmlp_ln_b'], lp['gmlp_ws'], lp['gmlp_bs'], _pick(t, (512, 256, 128)))
        pre = _rwkv_pre(zp, jnp.zeros((bp, 1, 4 * RWKV_WIDTH), F32), lp, _pick(t, (256, 128)))
        o_r, wkv_bd = _rwkv_scan(*pre, lp['rwkv_ln_w'], lp['rwkv_ln_b'], jnp.zeros((bp, RWKV_WIDTH, RWKV_WIDTH), F32))
        outs['wkv_p'].append(jnp.stack([wkv_bd[:, h * HEAD_DIM:(h + 1) * HEAD_DIM, h * HEAD_DIM:(h + 1) * HEAD_DIM]
                                        for h in range(N_HEADS)], axis=1))
        xp = _merge(zp, o_att, o_c, o_r, wb, wo, xp, mod_p[2], tm_p)
        xp = _ffn(xp, norm_ffn_g[l], mod_p[3], mod_p[4], mod_p[5], wfi, wfo, final_norm_g, last, tm_f, tf)

        zs = _inproj(xs, norm_mix_g[l], mod_s[0], mod_s[1], w_proj, bs, tn)
        z2 = zs[0]
        zi2 = _inproj_hp(xs, norm_mix_g[l], mod_s[0], mod_s[1], wi_hi, wi_lo, bs)[0]
        k_idx_new = zi2[:, ZI_K:ZI_K + HEAD_DIM]
        outs['rows_n_s'].append(z2[:, C_K:C_K + 4 * HEAD_DIM].reshape(bs, 1, 4, HEAD_DIM))
        outs['rows_d_s'].append(jnp.concatenate([z2[:, C_K + 3 * LANE:C_K + 4 * LANE], k_idx_new],
                                                axis=-1).reshape(bs, 1, 3, HEAD_DIM))
        outs['shift_s'].append(z2[:, C_RWKV:C_RWKV + 4 * RWKV_WIDTH])
        heads = lambda zz, c0: zz[:, c0:c0 + N_HEADS * LANE].reshape(bs, N_HEADS, LANE)[:, :, :HEAD_DIM]
        vecs = jnp.concatenate([heads(z2, C_QN), heads(z2, C_QD), heads(zi2, ZI_Q),
                                z2[:, C_K + LANE:C_K + 4 * LANE].reshape(bs, 6, HEAD_DIM), k_idx_new[:, None]], axis=1)
        cols = jnp.transpose(jnp.pad(vecs, ((0, 0), (0, LANE - vecs.shape[1]), (0, 0))), (0, 2, 1))
        misc_row = jnp.concatenate([z2[:, C_MISC:C_MISC + 3 * N_HEADS], zi2[:, ZI_W:ZI_W + N_HEADS],
                                    jnp.zeros((bs, LANE - 4 * N_HEADS), F32)], axis=1)
        misc = jnp.broadcast_to(misc_row[:, None], (bs, 8, LANE))
        idx = _sample_index(page_table, cols, misc, cache_d, l)
        width = idx.shape[-1]
        dsel = _sample_topk(idx.reshape(bs, width), min(DSA_TOPK, (past + 1) // 4)).reshape(bs, 1, width)
        cmp_w = _compress_weights_pairs(nsa_cmp_w1[l], nsa_cmp_w2[l], nsa_cmp_pos[l])
        o_t, win_new = _sample_attention(page_table, cols, misc, dsel, win_t, cmp_w, cache_n, cache_d, l)
        o_heads = jnp.transpose(o_t[:, :, 0:2 * N_HEADS], (0, 2, 1)).reshape(bs, 2 * N_HEADS * HEAD_DIM)
        outs['win_s'].append(jnp.transpose(win_new, (0, 3, 1, 2)))
        o_cs, v_cs, o_rs, wkv_s = _sample_mixers(z2, state_rwkv_shift[l], state_rwkv_wkv[l], lp, 8)
        outs['v_s'].append(v_cs.reshape(bs, 1, GMLP_WIDTH))
        outs['wkv_s'].append(wkv_s)
        xs = _merge(zs, o_heads[None], o_cs[None], o_rs[None], wb, wo, xs, mod_s[2], bs)
        xs = _ffn(xs, norm_ffn_g[l], mod_s[3], mod_s[4], mod_s[5], wfi, wfo, final_norm_g, last, bs, tf)

    st = lambda k: jnp.stack(outs[k])
    return (xp, xs.reshape(bs, 1, d), st('rows_n_p'), st('rows_n_s'), st('rows_d_p'), st('rows_d_s'), st('win_p'),
            st('win_s'), st('v_s'), st('shift_p'), st('shift_s'), st('wkv_p'), st('wkv_s'))
```

```python
import functools

import numpy as np
import jax
import jax.numpy as jnp
from jax import lax
from jax.experimental import pallas as pl
from jax.experimental.pallas import tpu as pltpu

F32 = jnp.float32
BF16 = jnp.bfloat16
I32 = jnp.int32

HEAD_DIM = 64
N_HEADS = 4
NSA_CMP_LEN = 32
NSA_CMP_STRIDE = 16
NSA_SLC_LEN = 64
NSA_TOPN = 8
NSA_WINDOW = 512
DSA_TOPK = 256
GMLP_GROUPS = 4
GMLP_WIDTH = GMLP_GROUPS * HEAD_DIM
CHUNK = 128
RWKV_WIDTH = N_HEADS * HEAD_DIM
RWKV_GN_EPS = 64e-5
QB = 128
EPS = 1e-6
LN_EPS = 1e-5
NEG = -1e30
FORCE_BONUS = 1e4
LANE = 128
VMEM_LIMIT = 56 * 1024 * 1024

C_MERGE, C_RWKV, C_GMLP = 0, 4096, 5120
C_QN, C_QD, C_MISC, C_K = 5632, 6144, 6656, 6784
N_PROJ = 7296
K_PAIRS = 4
P_CMP, P_SLC, P_WIN, P_DSA = range(K_PAIRS)
ZI_Q, ZI_K, ZI_W, N_HP = 0, 512, 640, 768


def _cparams(sem):
    return pltpu.CompilerParams(dimension_semantics=sem, vmem_limit_bytes=VMEM_LIMIT)


def _dot(a, b):
    return jnp.dot(a, b, preferred_element_type=F32)


def _dot_nt(a, b):
    return lax.dot_general(a, b, (((1,), (1,)), ((), ())), preferred_element_type=F32)


def _dot_tn(a, b):
    return lax.dot_general(a, b, (((0,), (0,)), ((), ())), preferred_element_type=F32)


def _gelu_tanh(x):
    return 0.5 * x * (1.0 + jnp.tanh(np.sqrt(2.0 / np.pi).astype(np.float32) * (x + 0.044715 * (x * x * x))))


def _sigmoid(x):
    return 1.0 / (1.0 + jnp.exp(-x))


def _rms_mod(x, g, scale, shift):
    ms = jnp.mean(x * x, axis=-1, keepdims=True)
    return (x * lax.rsqrt(ms + EPS) * g) * (1.0 + scale) + shift


def _ada_kernel(c_ref, w_ref, b_ref, o_ref):
    c = c_ref[...]
    s = (c * _sigmoid(c)).astype(BF16)
    o_ref[...] = _dot(s, w_ref[...].astype(BF16)) + b_ref[...]


def _ada(c, w, b):
    m, d = c.shape
    n = w.shape[1]
    tn = 1536
    return pl.pallas_call(
        _ada_kernel,
        grid=(n // tn,),
        in_specs=[pl.BlockSpec((m, d), lambda j: (0, 0)),
                  pl.BlockSpec((d, tn), lambda j: (0, j)),
                  pl.BlockSpec((1, tn), lambda j: (0, j))],
        out_specs=pl.BlockSpec((m, tn), lambda j: (0, j)),
        out_shape=jax.ShapeDtypeStruct((m, n), F32),
        compiler_params=_cparams(("parallel",)),
        name="ada",
    )(c, w, b.reshape(1, n))


def _inproj_kernel(x_ref, g_ref, sh_ref, sc_ref, w_ref, o_ref):
    h = _rms_mod(x_ref[0], g_ref[...], sc_ref[0], sh_ref[0])
    o_ref[0] = _dot(h.astype(BF16), w_ref[...])


def _inproj(x, g, shift, scale, w, tm, tn):
    b, t, d = x.shape
    n = w.shape[1]
    tmod = tm if shift.shape[1] == t else 1
    mod_map = (lambda j, bi, i: (bi, i, 0)) if shift.shape[1] == t else (lambda j, bi, i: (bi, 0, 0))
    return pl.pallas_call(
        _inproj_kernel,
        grid=(n // tn, b, t // tm),
        in_specs=[pl.BlockSpec((1, tm, d), lambda j, bi, i: (bi, i, 0)),
                  pl.BlockSpec((1, d), lambda j, bi, i: (0, 0)),
                  pl.BlockSpec((1, tmod, d), mod_map),
                  pl.BlockSpec((1, tmod, d), mod_map),
                  pl.BlockSpec((d, tn), lambda j, bi, i: (0, j))],
        out_specs=pl.BlockSpec((1, tm, tn), lambda j, bi, i: (bi, i, j)),
        out_shape=jax.ShapeDtypeStruct((b, t, n), F32),
        compiler_params=_cparams(("parallel", "parallel", "parallel")),
        name="inproj",
    )(x, g.reshape(1, d), shift, scale, w)


def _split_bf16(x):
    hi = x.astype(BF16)
    return hi, (x - hi.astype(F32)).astype(BF16)


def _inproj_hp_kernel(x_ref, g_ref, sh_ref, sc_ref, wh_ref, wl_ref, o_ref):
    h_hi, h_lo = _split_bf16(_rms_mod(x_ref[0], g_ref[...], sc_ref[0], sh_ref[0]))
    o_ref[0] = _dot(h_hi, wh_ref[...]) + (_dot(h_hi, wl_ref[...]) + _dot(h_lo, wh_ref[...]))


def _inproj_hp(x, g, shift, scale, w_hi, w_lo, tm):
    b, t, d = x.shape
    n = w_hi.shape[1]
    tmod = tm if shift.shape[1] == t else 1
    mod_map = (lambda bi, i: (bi, i, 0)) if shift.shape[1] == t else (lambda bi, i: (bi, 0, 0))
    return pl.pallas_call(
        _inproj_hp_kernel,
        grid=(b, t // tm),
        in_specs=[pl.BlockSpec((1, tm, d), lambda bi, i: (bi, i, 0)),
                  pl.BlockSpec((1, d), lambda bi, i: (0, 0)),
                  pl.BlockSpec((1, tmod, d), mod_map),
                  pl.BlockSpec((1, tmod, d), mod_map),
                  pl.BlockSpec((d, n), lambda bi, i: (0, 0)),
                  pl.BlockSpec((d, n), lambda bi, i: (0, 0))],
        out_specs=pl.BlockSpec((1, tm, n), lambda bi, i: (bi, i, 0)),
        out_shape=jax.ShapeDtypeStruct((b, t, n), F32),
        compiler_params=_cparams(("parallel", "parallel")),
        name="inproj_hp",
    )(x, g.reshape(1, d), shift, scale, w_hi, w_lo)


def _merge_kernel(zm_ref, oa_ref, oc_ref, or_ref, wb_ref, wo_ref, x_ref, gate_ref, o_ref):
    bw = wb_ref.shape[1]
    d = x_ref.shape[-1]
    outs = (oa_ref[0, :, 0:bw], oa_ref[0, :, bw:2 * bw], oc_ref[0], or_ref[0])
    mixed = None
    for n, o in enumerate(outs):
        br = _dot(o.astype(BF16), wb_ref[n])
        term = _sigmoid(zm_ref[0, :, n * d:(n + 1) * d]) * br
        mixed = term if mixed is None else mixed + term
    y = _dot(mixed.astype(BF16), wo_ref[...])
    o_ref[0] = x_ref[0] + gate_ref[0] * y


def _merge(z, o_att, o_c, o_r, w_branch, w_out, x, gate, tm):
    b, t, d = x.shape
    nb, bw, _ = w_branch.shape
    tmod = tm if gate.shape[1] == t else 1
    mod_map = (lambda bi, i: (bi, i, 0)) if gate.shape[1] == t else (lambda bi, i: (bi, 0, 0))
    return pl.pallas_call(
        _merge_kernel,
        grid=(b, t // tm),
        in_specs=[pl.BlockSpec((1, tm, nb * d), lambda bi, i: (bi, i, C_MERGE // (nb * d))),
                  pl.BlockSpec((1, tm, 2 * bw), lambda bi, i: (bi, i, 0)),
                  pl.BlockSpec((1, tm, bw), lambda bi, i: (bi, i, 0)),
                  pl.BlockSpec((1, tm, bw), lambda bi, i: (bi, i, 0)),
                  pl.BlockSpec((nb, bw, d), lambda bi, i: (0, 0, 0)),
                  pl.BlockSpec((d, d), lambda bi, i: (0, 0)),
                  pl.BlockSpec((1, tm, d), lambda bi, i: (bi, i, 0)),
                  pl.BlockSpec((1, tmod, d), mod_map)],
        out_specs=pl.BlockSpec((1, tm, d), lambda bi, i: (bi, i, 0)),
        out_shape=jax.ShapeDtypeStruct((b, t, d), F32),
        compiler_params=_cparams(("parallel", "parallel")),
        name="merge",
    )(z, o_att, o_c, o_r, w_branch, w_out, x, gate)


def _ffn_kernel(x_ref, g_ref, sh_ref, sc_ref, gate_ref, wg_ref, wu_ref, wd_ref, fg_ref, o_ref,
                h_scr, acc_scr, *, final_norm):
    k = pl.program_id(2)

    @pl.when(k == 0)
    def _():
        h_scr[...] = _rms_mod(x_ref[0], g_ref[...], sc_ref[0], sh_ref[0]).astype(BF16)
        acc_scr[...] = jnp.zeros_like(acc_scr)

    h = h_scr[...]
    gt = _dot(h, wg_ref[...])
    up = _dot(h, wu_ref[...])
    act = (gt * _sigmoid(gt)) * up
    acc_scr[...] += _dot(act.astype(BF16), wd_ref[...])

    @pl.when(k == pl.num_programs(2) - 1)
    def _():
        y = x_ref[0] + gate_ref[0] * acc_scr[...]
        if final_norm:
            ms = jnp.mean(y * y, axis=-1, keepdims=True)
            y = y * lax.rsqrt(ms + EPS) * fg_ref[...]
        o_ref[0] = y


def _ffn(x, g, shift, scale, gate, w_in, w_out, final_g, final_norm, tm, tf):
    b, t, d = x.shape
    ff = w_out.shape[0]
    nk = ff // tf
    tmod = tm if gate.shape[1] == t else 1
    mod_map = (lambda bi, i, k: (bi, i, 0)) if gate.shape[1] == t else (lambda bi, i, k: (bi, 0, 0))
    return pl.pallas_call(
        functools.partial(_ffn_kernel, final_norm=final_norm),
        grid=(b, t // tm, nk),
        in_specs=[pl.BlockSpec((1, tm, d), lambda bi, i, k: (bi, i, 0)),
                  pl.BlockSpec((1, d), lambda bi, i, k: (0, 0)),
                  pl.BlockSpec((1, tmod, d), mod_map),
                  pl.BlockSpec((1, tmod, d), mod_map),
                  pl.BlockSpec((1, tmod, d), mod_map),
                  pl.BlockSpec((d, tf), lambda bi, i, k: (0, k)),
                  pl.BlockSpec((d, tf), lambda bi, i, k: (0, nk + k)),
                  pl.BlockSpec((tf, d), lambda bi, i, k: (k, 0)),
                  pl.BlockSpec((1, d), lambda bi, i, k: (0, 0))],
        out_specs=pl.BlockSpec((1, tm, d), lambda bi, i, k: (bi, i, 0)),
        out_shape=jax.ShapeDtypeStruct((b, t, d), F32),
        scratch_shapes=[pltpu.VMEM((tm, d), BF16), pltpu.VMEM((tm, d), F32)],
        compiler_params=_cparams(("parallel", "parallel", "arbitrary")),
        name="ffn",
    )(x, g.reshape(1, d), shift, scale, gate, w_in, w_in, w_out, final_g.reshape(1, d))


def _cmp_kernel(r_ref, pa_ref, pb_ref, w1a_ref, w1b_ref, w2_ref, ob_ref, ot_ref):
    n16 = r_ref.shape[2]
    comp = []
    for z in range(2):
        r = r_ref[0, z]
        a = _dot((r + pa_ref[z]).astype(BF16), w1a_ref[z])
        b = _dot((r + pb_ref[z]).astype(BF16), w1b_ref[z])
        hid = _gelu_tanh(a + pltpu.roll(b, n16 - 1, axis=0))
        comp.append(_dot(hid.astype(BF16), w2_ref[z]))
    pair = jnp.concatenate(comp, axis=1)
    ob_ref[0] = pair.astype(BF16)
    ot_ref[0] = pair.T.astype(BF16)


def _nsa_compress(r, pos_a, pos_b, w1a, w1b, w2):
    b, _, n16, kd = r.shape
    hid = w1a.shape[-1]
    return pl.pallas_call(
        _cmp_kernel,
        grid=(b,),
        in_specs=[pl.BlockSpec((1, 2, n16, kd), lambda bi: (bi, 0, 0, 0)),
                  pl.BlockSpec((2, 1, kd), lambda bi: (0, 0, 0)),
                  pl.BlockSpec((2, 1, kd), lambda bi: (0, 0, 0)),
                  pl.BlockSpec((2, kd, hid), lambda bi: (0, 0, 0)),
                  pl.BlockSpec((2, kd, hid), lambda bi: (0, 0, 0)),
                  pl.BlockSpec((2, hid, HEAD_DIM), lambda bi: (0, 0, 0))],
        out_specs=[pl.BlockSpec((1, n16, 2 * HEAD_DIM), lambda bi: (bi, 0, 0)),
                   pl.BlockSpec((1, 2 * HEAD_DIM, n16), lambda bi: (bi, 0, 0))],
        out_shape=[jax.ShapeDtypeStruct((b, n16, 2 * HEAD_DIM), BF16),
                   jax.ShapeDtypeStruct((b, 2 * HEAD_DIM, n16), BF16)],
        compiler_params=_cparams(("parallel",)),
        name="nsa_compress",
    )(r, pos_a, pos_b, w1a, w1b, w2)


CK = 256
AUX_POS = 32


def _attn_aux(t):
    pos = np.arange(t)
    aux = np.zeros((t, LANE), np.float32)
    aux[pos, pos // NSA_SLC_LEN] = 1.0
    aux[:, AUX_POS] = pos % CK
    aux[:, AUX_POS + 1] = pos // CK
    return jnp.asarray(aux, BF16)


def _kprep_kernel(z_ref, aux_ref, ob_ref, ot_ref):
    x = z_ref[0]
    ob_ref[0, 0] = jnp.concatenate([x.astype(BF16), aux_ref[...]], axis=1)
    for c in range(x.shape[0] // CK):
        ot_ref[0, 0, c] = x[c * CK:(c + 1) * CK].T[HEAD_DIM:2 * HEAD_DIM].astype(BF16)


def _kprep(z):
    b, t, _ = z.shape
    tk = _pick(t, (1024, 512, CK))
    return pl.pallas_call(
        _kprep_kernel,
        grid=(b, K_PAIRS, t // tk),
        in_specs=[pl.BlockSpec((1, tk, LANE), lambda bi, p, c: (bi, c, C_K // LANE + p)),
                  pl.BlockSpec((tk, LANE), lambda bi, p, c: (c, 0))],
        out_specs=[pl.BlockSpec((1, 1, tk, 2 * LANE), lambda bi, p, c: (bi, p, c, 0)),
                   pl.BlockSpec((1, 1, tk // CK, HEAD_DIM, CK), lambda bi, p, c: (bi, p, c, 0, 0))],
        out_shape=[jax.ShapeDtypeStruct((b, K_PAIRS, t, 2 * LANE), BF16),
                   jax.ShapeDtypeStruct((b, K_PAIRS, t // CK, HEAD_DIM, CK), BF16)],
        compiler_params=_cparams(("parallel", "parallel", "parallel")),
        name="kprep",
    )(z, _attn_aux(t))


def _kprep_idx_kernel(z_ref, o_ref):
    x = z_ref[0]
    hi = x.astype(BF16).astype(F32)
    o_ref[0] = jnp.concatenate([hi + pltpu.roll(x - hi, HEAD_DIM, axis=1), hi], axis=1).astype(BF16)


def _kprep_idx(zi):
    b, t, _ = zi.shape
    tt = _pick(t, (512, 256, 128))
    return pl.pallas_call(
        _kprep_idx_kernel,
        grid=(b, t // tt),
        in_specs=[pl.BlockSpec((1, tt, LANE), lambda bi, i: (bi, i, ZI_K // LANE))],
        out_specs=pl.BlockSpec((1, tt, 2 * LANE), lambda bi, i: (bi, i, 0)),
        out_shape=jax.ShapeDtypeStruct((b, t, 2 * LANE), BF16),
        compiler_params=_cparams(("parallel", "parallel")),
        name="kprep_idx",
    )(zi)


INT_MIN = -2 ** 31


def _sortable_key(v):
    v = jnp.where(v == 0.0, 0.0, v)
    u = lax.bitcast_convert_type(v, I32)
    return jnp.where(u < 0, u ^ 0x7FFFFFFF, u)


_KEY_HALF_NEG = int(np.array(0.5 * NEG, np.float32).view(np.int32) ^ 0x7FFFFFFF)


def _lane_consts(q0):
    lane = lax.broadcasted_iota(I32, (1, N_HEADS * QB), 1)
    hl = lane >> 7
    q_pos = q0 + (lane & (QB - 1))
    slope = jnp.where(hl == 0, 2.0 ** -2, jnp.where(hl == 1, 2.0 ** -4, jnp.where(hl == 2, 2.0 ** -6, 2.0 ** -8)))
    return q_pos, slope.astype(F32)


def _tile_heads(x):
    return jnp.concatenate([x] * N_HEADS, axis=1)


def _place_heads(tiles):
    lane = lax.broadcasted_iota(I32, (QB, LANE), 1)
    out = []
    for t in range(N_HEADS // 2):
        out.append(jnp.where(lane < HEAD_DIM, pltpu.roll(tiles[2 * t], HEAD_DIM, axis=1), tiles[2 * t + 1]))
    return jnp.concatenate(out, axis=1)


def _attn_kernel(qn_ref, qd_ref, misc_ref, qi_ref, wi_ref, kb_ref, kt_ref, ki_ref, cb_ref, ct_ref, c2s_ref, o_ref,
                 key_scr, dsel_scr, *, seq_len, topk):
    t = seq_len
    q0 = pl.program_id(1) * QB
    nc = (q0 + QB + CK - 1) // CK
    hq = N_HEADS * QB
    scale = HEAD_DIM ** -0.5

    def stack_q(ref):
        return jnp.concatenate([ref[0, :, h * LANE:(h + 1) * LANE] for h in range(N_HEADS)], axis=0) * scale

    qn, qd = stack_q(qn_ref).astype(BF16), stack_q(qd_ref).astype(BF16)
    qi = stack_q(qi_ref)
    qi_cat = jnp.concatenate([(qi + pltpu.roll(qi, HEAD_DIM, axis=1)).astype(BF16),
                              (qi - qi.astype(BF16).astype(F32)).astype(BF16)], axis=1)
    misc_t = misc_ref[0].T
    wi_t = wi_ref[0].T
    q_pos, slope = _lane_consts(q0)

    n16 = cb_ref.shape[1]
    sc = _dot_nt(cb_ref[0], qn)
    cmp_end = lax.broadcasted_iota(I32, (n16, hq), 0) * NSA_CMP_STRIDE + (NSA_CMP_LEN - 1)
    d = q_pos - cmp_end
    mask = d >= 0
    sm = jnp.where(mask, sc - slope * d.astype(F32), NEG)
    e = jnp.where(mask, jnp.exp(sm - jnp.max(sm, axis=0, keepdims=True)), 0.0)
    l = jnp.sum(e, axis=0, keepdims=True)
    p = e / jnp.where(l > 0.0, l, 1.0)
    o_cmp_t = _dot(ct_ref[0], p.astype(BF16))
    psum_t = p[:, 0:QB]
    for h in range(1, N_HEADS):
        psum_t = psum_t + p[:, h * QB:(h + 1) * QB]
    imp_t = jnp.dot(c2s_ref[...], psum_t, preferred_element_type=F32, precision=lax.Precision.HIGHEST)

    nslc = c2s_ref.shape[0]
    jrow = lax.broadcasted_iota(I32, (nslc, QB), 0)
    qp = q0 + lax.broadcasted_iota(I32, (nslc, QB), 1)
    cur = qp >> 6
    adm = jrow * NSA_SLC_LEN <= qp
    forced = (jrow == 0) | (jrow == cur) | (jrow == cur - 1)
    score = jnp.where(adm, imp_t + jnp.where(forced, FORCE_BONUS, 0.0), NEG)
    rank = jnp.zeros((nslc, QB), I32)
    for j in range(nslc):
        row = score[j:j + 1, :]
        rank = rank + jnp.where((row > score) | ((row == score) & (jrow > j)), 1, 0)
    sel_bias = jnp.where((rank < min(NSA_TOPN, nslc)) & (score > 0.5 * NEG), 0.0, NEG)

    w_rows = [wi_t[h:h + 1, :] * (N_HEADS ** -0.5) for h in range(N_HEADS)]

    npair = (nc + 1) // 2

    def idx_body(c, carry):
        rows = pl.ds(pl.multiple_of(c * (2 * CK), 2 * CK), 2 * CK)
        lg = jnp.maximum(_dot_nt(ki_ref[0, rows, :], qi_cat), 0.0)
        idx = lg[:, 0:QB] * w_rows[0]
        for h in range(1, N_HEADS):
            idx = idx + lg[:, h * QB:(h + 1) * QB] * w_rows[h]
        kpos = c * (2 * CK) + lax.broadcasted_iota(I32, (2 * CK, QB), 0)
        causal = kpos <= q0 + lax.broadcasted_iota(I32, (2 * CK, QB), 1)
        key_scr[rows, :] = _sortable_key(jnp.where(causal, idx, NEG))
        return carry

    lax.fori_loop(0, npair, idx_body, 0)

    def count(pred_fn):
        def body(c, acc):
            rows = pl.ds(pl.multiple_of(c * (2 * CK), 2 * CK), 2 * CK)
            return acc + jnp.sum(jnp.where(pred_fn(key_scr[rows, :]), 1, 0), axis=0, keepdims=True)
        return lax.fori_loop(0, npair, body, jnp.zeros((1, QB), I32))

    def bit_body(i, ans):
        cand = ans + lax.shift_left(jnp.int32(1), 31 - i)
        return jnp.where(count(lambda k: k >= cand) >= topk, cand, ans)

    thr = lax.fori_loop(0, 32, bit_body, jnp.full((1, QB), INT_MIN, I32))
    need = (topk - count(lambda k: k > thr)).astype(F32)

    tri = jnp.where(lax.broadcasted_iota(I32, (CK, CK), 1) < lax.broadcasted_iota(I32, (CK, CK), 0), 1.0, 0.0).astype(BF16)

    def tie_body(c, running):
        rows = pl.ds(pl.multiple_of(c * CK, CK), CK)
        k = key_scr[rows, :]
        eq = k == thr
        eqf = jnp.where(eq, 1.0, 0.0)
        before = _dot(tri, eqf.astype(BF16)) + running
        sel = ((k > thr) | (eq & (before < need))) & (k > _KEY_HALF_NEG)
        dsel_scr[rows, :] = jnp.where(sel, 0.0, NEG)
        return running + jnp.sum(eqf, axis=0, keepdims=True)

    lax.fori_loop(0, nc, tie_body, jnp.zeros((1, QB), F32))

    sel_bias_t = jnp.concatenate([sel_bias, jnp.zeros((LANE - nslc, QB), F32)], axis=0).T
    lane = lax.broadcasted_iota(I32, (QB, LANE), 1)
    q_slc, q_dsa = [], []
    for h in range(N_HEADS):
        pos_cols = jnp.where(lane == AUX_POS, SLOPES[h], jnp.where(lane == AUX_POS + 1, SLOPES[h] * CK, 0.0))
        q_slc.append(sel_bias_t + pos_cols)
        q_dsa.append(pos_cols)
    qn_cat = jnp.concatenate([qn, jnp.concatenate(q_slc, axis=0).astype(BF16)], axis=1)
    qd_cat = jnp.concatenate([qd, jnp.concatenate(q_dsa, axis=0).astype(BF16)], axis=1)
    def online(state, s, vt):
        m_old, l_old, acc_old = state
        m_new = jnp.maximum(m_old, jnp.max(s, axis=0, keepdims=True))
        alpha = jnp.exp(m_old - m_new)
        e_ = jnp.exp(s - m_new)
        return m_new, alpha * l_old + jnp.sum(e_, axis=0, keepdims=True), alpha * acc_old + _dot(vt, e_.astype(BF16))

    def scores(c, diagonal):
        rows = pl.ds(pl.multiple_of(c * CK, CK), CK)
        s_slc = _dot_nt(kb_ref[0, P_SLC, rows, :], qn_cat)
        if diagonal:
            kpos = c * CK + lax.broadcasted_iota(I32, (CK, hq), 0)
            s_slc = jnp.where(kpos <= q_pos, s_slc, NEG)
        return s_slc, _dot_nt(kb_ref[0, P_DSA, rows, :], qd_cat) + _tile_heads(dsel_scr[rows, :])

    def flash_chunks(chunks, state, diagonal=False):
        sc = [scores(c, diagonal) for c in chunks]
        st_slc, st_dsa = state
        for c, (s_slc, s_dsa) in zip(chunks, sc):
            st_slc = online(st_slc, s_slc, kt_ref[0, P_SLC, c])
            st_dsa = online(st_dsa, s_dsa, kt_ref[0, P_DSA, c])
        return st_slc, st_dsa

    init = (jnp.full((1, hq), NEG, F32), jnp.zeros((1, hq), F32), jnp.zeros((HEAD_DIM, hq), F32))
    n_full = nc - 1
    state = lax.fori_loop(0, n_full // 2, lambda i, st: flash_chunks((2 * i, 2 * i + 1), st), (init, init))
    state = lax.cond(n_full % 2 == 1, lambda st: flash_chunks((n_full - 1,), st), lambda st: st, state)
    state = flash_chunks((nc - 1,), state, diagonal=True)

    def finish(st):
        m_, l_, acc = st
        return acc * jnp.where(m_ > 0.5 * NEG, 1.0 / l_, 0.0)

    o_slc_t, o_dsa_t = finish(state[0]), finish(state[1])

    wk = min(NSA_WINDOW + QB, t)
    ws = pl.multiple_of(jnp.clip(q0 - NSA_WINDOW, 0, t - wk), QB)
    kw = kb_ref[0, P_WIN, pl.ds(ws, wk), 0:LANE]
    dw = q_pos - (ws + lax.broadcasted_iota(I32, (wk, hq), 0))
    mask = (dw >= 0) & (dw <= NSA_WINDOW)
    sm = jnp.where(mask, _dot_nt(kw, qn) - slope * dw.astype(F32), NEG)
    mw = jnp.max(sm, axis=0, keepdims=True)
    e = jnp.exp(sm - mw)
    lw = jnp.sum(e, axis=0, keepdims=True)
    o_win_t = _dot_tn(kw, e.astype(BF16)) * jnp.where(mw > 0.5 * NEG, 1.0 / lw, 0.0)

    gates = _sigmoid(misc_t[0:3 * N_HEADS, :])
    nsa_tiles, dsa_tiles = [], []
    top = jnp.zeros((HEAD_DIM, QB), F32)
    vrows = slice(HEAD_DIM, 2 * HEAD_DIM)
    for h in range(N_HEADS):
        cols = slice(h * QB, (h + 1) * QB)
        on = (gates[3 * h:3 * h + 1, :] * o_cmp_t[vrows, cols] + gates[3 * h + 1:3 * h + 2, :] * o_slc_t[:, cols]
              + gates[3 * h + 2:3 * h + 3, :] * o_win_t[vrows, cols])
        nsa_tiles.append(jnp.concatenate([top, on], axis=0).T)
        dsa_tiles.append(jnp.concatenate([top, o_dsa_t[:, cols]], axis=0).T)
    o_ref[0, :, 0:N_HEADS * HEAD_DIM] = _place_heads(nsa_tiles)
    o_ref[0, :, N_HEADS * HEAD_DIM:2 * N_HEADS * HEAD_DIM] = _place_heads(dsa_tiles)


def _cmp_to_slc_t(n16, n_slc):
    start = np.arange(n16) * NSA_CMP_STRIDE
    bstart = np.arange(n_slc) * NSA_SLC_LEN
    ov = np.minimum(start[:, None] + NSA_CMP_LEN, bstart[None, :] + NSA_SLC_LEN) - np.maximum(start[:, None], bstart[None, :])
    return (np.clip(ov, 0, None) / NSA_CMP_LEN).T.astype(np.float32)


def _prompt_attention(z, zi, kb, kt, ki, cb, ct):
    b, t, _ = z.shape
    n16 = cb.shape[1]
    n_slc = -(-t // NSA_SLC_LEN)
    assert n_slc <= AUX_POS and t // CK <= 256 and t % (2 * CK) == 0
    hq = N_HEADS * QB
    qw = N_HEADS * LANE
    topk = min(DSA_TOPK, t // 4)
    c2s = jnp.asarray(_cmp_to_slc_t(n16, n_slc))
    qspec = lambda col: pl.BlockSpec((1, QB, qw), lambda bi, i: (bi, i, col // qw))
    tile = lambda col: pl.BlockSpec((1, QB, LANE), lambda bi, i: (bi, i, col // LANE))
    return pl.pallas_call(
        functools.partial(_attn_kernel, seq_len=t, topk=topk),
        grid=(b, t // QB),
        in_specs=[qspec(C_QN), qspec(C_QD), tile(C_MISC), qspec(ZI_Q), tile(ZI_W),
                  pl.BlockSpec((1, K_PAIRS, t, 2 * LANE), lambda bi, i: (bi, 0, 0, 0)),
                  pl.BlockSpec((1, K_PAIRS, t // CK, HEAD_DIM, CK), lambda bi, i: (bi, 0, 0, 0, 0)),
                  pl.BlockSpec((1, t, 2 * LANE), lambda bi, i: (bi, 0, 0)),
                  pl.BlockSpec((1, n16, LANE), lambda bi, i: (bi, 0, 0)),
                  pl.BlockSpec((1, LANE, n16), lambda bi, i: (bi, 0, 0)),
                  pl.BlockSpec((n_slc, n16), lambda bi, i: (0, 0))],
        out_specs=pl.BlockSpec((1, QB, 2 * N_HEADS * HEAD_DIM), lambda bi, i: (bi, i, 0)),
        out_shape=jax.ShapeDtypeStruct((b, t, 2 * N_HEADS * HEAD_DIM), F32),
        scratch_shapes=[pltpu.VMEM((t, QB), I32), pltpu.VMEM((t, QB), F32)],
        compiler_params=_cparams(("parallel", "parallel")),
        name="prompt_attention",
    )(z, z, z, zi, zi, kb, kt, ki, cb, ct, c2s)


def _layer_norm(v, g, b):
    mu = jnp.mean(v, axis=-1, keepdims=True)
    var = jnp.mean(jnp.square(v - mu), axis=-1, keepdims=True)
    return (v - mu) * lax.rsqrt(var + LN_EPS) * g + b


def _gmlp_kernel(z_ref, lng_ref, lnb_ref, ws_ref, bs_ref, o_ref):
    tc = z_ref.shape[1]
    c = ws_ref.shape[1]
    tril = lax.broadcasted_iota(I32, (c, c), 1) <= lax.broadcasted_iota(I32, (c, c), 0)
    lane_g = lax.broadcasted_iota(I32, (1, GMLP_WIDTH), 1) >> 6
    ws = [jnp.where(tril, ws_ref[g], 0.0).astype(BF16) for g in range(GMLP_GROUPS)]
    for ci in range(tc // c):
        zg = _gelu_tanh(z_ref[0, ci * c:(ci + 1) * c, :])
        u = zg[:, 0:GMLP_WIDTH]
        v = _layer_norm(zg[:, GMLP_WIDTH:2 * GMLP_WIDTH], lng_ref[...], lnb_ref[...]).astype(BF16)
        s = bs_ref[...]
        for g in range(GMLP_GROUPS):
            s = s + jnp.where(lane_g == g, _dot(ws[g], v), 0.0)
        o_ref[0, ci * c:(ci + 1) * c, :] = u * s


def _gmlp(z, ln_g, ln_b, w_s, b_s, tc):
    b, t, _ = z.shape
    c = w_s.shape[1]
    bs_exp = jnp.repeat(b_s.T, HEAD_DIM, axis=1)
    return pl.pallas_call(
        _gmlp_kernel,
        grid=(b, t // tc),
        in_specs=[pl.BlockSpec((1, tc, 2 * GMLP_WIDTH), lambda bi, i: (bi, i, C_GMLP // (2 * GMLP_WIDTH))),
                  pl.BlockSpec((1, GMLP_WIDTH), lambda bi, i: (0, 0)),
                  pl.BlockSpec((1, GMLP_WIDTH), lambda bi, i: (0, 0)),
                  pl.BlockSpec((GMLP_GROUPS, c, c), lambda bi, i: (0, 0, 0)),
                  pl.BlockSpec((c, GMLP_WIDTH), lambda bi, i: (0, 0))],
        out_specs=pl.BlockSpec((1, tc, GMLP_WIDTH), lambda bi, i: (bi, i, 0)),
        out_shape=jax.ShapeDtypeStruct((b, t, GMLP_WIDTH), F32),
        compiler_params=_cparams(("parallel", "parallel")),
        name="gmlp",
    )(z, ln_g.reshape(1, -1), ln_b.reshape(1, -1), w_s, bs_exp)


def _head_sum(x):
    lane_h = lax.broadcasted_iota(I32, (1, x.shape[-1]), 1) >> 6
    out = jnp.zeros_like(x)
    for h in range(x.shape[-1] // HEAD_DIM):
        msk = lane_h == h
        out = out + jnp.where(msk, jnp.sum(jnp.where(msk, x, 0.0), axis=-1, keepdims=True), 0.0)
    return out


def _softplus(x):
    return jnp.maximum(x, 0.0) + jnp.log1p(jnp.exp(-jnp.abs(x)))


def _rwkv_features(f, w0, w2, a0, a2, g2, kkw, ka, rk):
    wd = RWKV_WIDTH
    r, k, v = f[:, 0:wd], f[:, wd:2 * wd], f[:, 2 * wd:3 * wd]
    wl, al, gl = f[:, 3 * wd:3 * wd + 64], f[:, 3 * wd + 64:3 * wd + 128], f[:, 3 * wd + 128:3 * wd + 256]
    w_log = -_softplus(-(w0 + _dot(jnp.tanh(wl).astype(BF16), w2))) - 0.5
    log_decay = -jnp.exp(w_log)
    a = _sigmoid(a0 + _dot(al.astype(BF16), a2))
    g = _dot(_sigmoid(gl).astype(BF16), g2)
    kk = k * kkw
    kk = kk * lax.rsqrt(_head_sum(kk * kk) + 1e-12)
    k = k * (1.0 + (a - 1.0) * ka)
    bonus = _head_sum(r * k * rk) * v
    return r, log_decay, k, v, kk, kk * a, g, bonus


RWKV_CHUNK = 64


def _rwkv_pre_kernel(f_ref, prev_ref, shift_ref, mu_ref, w0_ref, w2_ref, a0_ref, a2_ref, g2_ref, kkw_ref, ka_ref, rk_ref,
                     kh_o, rh_o, kb_o, bb_o, kbe_o, bbe_o, v_o, pe_o, g_o, bonus_o, *, chunk):
    feat = f_ref[0]
    tm = feat.shape[0]
    first = jnp.where(pl.program_id(1) == 0, shift_ref[0], prev_ref[0, 7:8, :])
    prev = jnp.where(lax.broadcasted_iota(I32, (tm, 1), 0) == 0, first, pltpu.roll(feat, 1, axis=0))
    f = feat + mu_ref[...] * (prev - feat)
    r, lw, k, v, kk, be, g, bonus = _rwkv_features(f, w0_ref[...], w2_ref[...], a0_ref[...], a2_ref[...], g2_ref[...],
                                                   kkw_ref[...], ka_ref[...], rk_ref[...])
    row = lax.broadcasted_iota(I32, (tm, tm), 0)
    col = lax.broadcasted_iota(I32, (tm, tm), 1)
    tri = jnp.where((col >= (row // chunk) * chunk) & (col <= row), 1.0, 0.0).astype(BF16)
    p1 = lw.astype(BF16)
    r1 = lw - p1.astype(F32)
    p2 = r1.astype(BF16)
    p3 = (r1 - p2.astype(F32)).astype(BF16)
    cum = _dot(tri, p1) + (_dot(tri, p2) + _dot(tri, p3))
    cum_end = jnp.concatenate([jnp.broadcast_to(cum[c * chunk + chunk - 1:(c + 1) * chunk, :], (chunk, cum.shape[1]))
                               for c in range(tm // chunk)], axis=0)
    down, to_end = jnp.exp(-cum), jnp.exp(cum_end - cum)
    for o, x in ((kh_o, kk * jnp.exp(cum - lw)), (rh_o, r * jnp.exp(cum)), (kb_o, k * down), (bb_o, be * down),
                 (kbe_o, k * to_end), (bbe_o, be * to_end), (v_o, v), (pe_o, jnp.exp(cum_end)), (g_o, g), (bonus_o, bonus)):
        o[0] = x.astype(o.dtype)


def _rwkv_pre(z, shift_prev, lp, tm):
    b, t, _ = z.shape
    wd = RWKV_WIDTH
    pw = 4 * wd
    vec = lambda n: pl.BlockSpec((1, n), lambda bi, i: (0, 0))
    mat = lambda r, c: pl.BlockSpec((r, c), lambda bi, i: (0, 0))
    fl = jax.ShapeDtypeStruct((b, t, wd), F32)
    hl = jax.ShapeDtypeStruct((b, t, wd), BF16)
    fspec = pl.BlockSpec((1, tm, wd), lambda bi, i: (bi, i, 0))
    return pl.pallas_call(
        functools.partial(_rwkv_pre_kernel, chunk=min(RWKV_CHUNK, t)),
        grid=(b, t // tm),
        in_specs=[pl.BlockSpec((1, tm, pw), lambda bi, i: (bi, i, C_RWKV // pw)),
                  pl.BlockSpec((1, 8, pw), lambda bi, i: (bi, jnp.maximum(i * (tm // 8) - 1, 0), C_RWKV // pw)),
                  pl.BlockSpec((1, 1, pw), lambda bi, i: (bi, 0, 0)),
                  vec(pw), vec(wd), mat(64, wd), vec(wd), mat(64, wd), mat(128, wd), vec(wd), vec(wd), vec(wd)],
        out_specs=[fspec] * 10,
        out_shape=[hl] * 7 + [fl] * 3,
        compiler_params=_cparams(("parallel", "parallel")),
        name="rwkv_pre",
    )(z, z, shift_prev, lp['rwkv_mu'].reshape(1, pw), lp['rwkv_w0'].reshape(1, wd), lp['rwkv_w2'].astype(BF16),
      lp['rwkv_a0'].reshape(1, wd), lp['rwkv_a2'].astype(BF16), lp['rwkv_g2'].astype(BF16),
      lp['rwkv_kk'].reshape(1, wd), lp['rwkv_ka'].reshape(1, wd), lp['rwkv_rk'].reshape(1, wd))


def _hdot(a, b):
    return jnp.dot(a, b, preferred_element_type=F32, precision=lax.Precision.HIGHEST)


def _hdot_nt(a, b):
    return lax.dot_general(a, b, (((1,), (1,)), ((), ())), preferred_element_type=F32, precision=lax.Precision.HIGHEST)


def _hdot_tn(a, b):
    return lax.dot_general(a, b, (((0,), (0,)), ((), ())), preferred_element_type=F32, precision=lax.Precision.HIGHEST)


def _group_norm_heads(y):
    mu = jnp.mean(y, axis=-1, keepdims=True)
    var = jnp.mean(jnp.square(y - mu), axis=-1, keepdims=True)
    return (y - mu) * lax.rsqrt(var + RWKV_GN_EPS)


def _rwkv_scan_kernel(kh_ref, rh_ref, kb_ref, bb_ref, kbe_ref, bbe_ref, v_ref, pe_ref, g_ref, bonus_ref, lnw_ref, lnb_ref,
                      s0_ref, o_ref, s_ref):
    @pl.when(pl.program_id(1) == 0)
    def _():
        s_ref[...] = s0_ref[...]

    bt, c, wd = kh_ref.shape
    n = N_HEADS * c
    row = lax.broadcasted_iota(I32, (n, wd), 0)
    col = lax.broadcasted_iota(I32, (n, wd), 1)
    same_head = (row // c) == (col // HEAD_DIM)
    tpos, jpos = row % c, col % HEAD_DIM
    strict, incl = same_head & (jpos < tpos), same_head & (jpos <= tpos)
    eye = jnp.where(same_head & (jpos == tpos), 1.0, 0.0)
    bf = lambda x: x.astype(BF16)
    block_diag = lambda ref, i: jnp.where(same_head, jnp.concatenate([ref[i]] * N_HEADS, axis=0), jnp.zeros((), BF16))
    for i in range(bt):
        kh, rh, kb, bb, kbe, bbe, v = (block_diag(ref, i) for ref in (kh_ref, rh_ref, kb_ref, bb_ref, kbe_ref, bbe_ref, v_ref))
        g1 = _dot_nt(jnp.concatenate([kh, rh], axis=0), jnp.concatenate([bb, kb], axis=0))
        a_kb = jnp.where(strict, g1[0:n, 0:n], 0.0)
        a_kk = jnp.where(strict, g1[0:n, n:2 * n], 0.0)
        a_rb = jnp.where(incl, g1[n:2 * n, 0:n], 0.0)
        a_rk = jnp.where(incl, g1[n:2 * n, n:2 * n], 0.0)
        s0 = s_ref[i]
        s0b = bf(s0)
        z = _dot_nt(kh, s0b) + _dot(bf(a_kk), v)
        y = _dot_nt(rh, s0b) + _dot(bf(a_rk), v)
        p = bf(-a_kb)
        tinv = eye - a_kb
        p = bf(_dot(p, p))
        for _ in range(int(np.log2(c)) - 2):
            both = _dot(jnp.concatenate([bf(tinv), p], axis=0), p)
            tinv = tinv + both[0:n]
            p = bf(both[n:2 * n])
        tinv = tinv + _dot(bf(tinv), p)
        u = bf(-_dot(bf(tinv), bf(z)))
        y = y + _dot(bf(a_rb), u)
        s_ref[i] = s0 * pe_ref[i, 0:1, :] + _dot_tn(v, kbe) + _dot_tn(u, bbe)
        y_flat = y[0:c]
        for h in range(1, N_HEADS):
            y_flat = y_flat + y[h * c:(h + 1) * c]
        mu = _head_sum(y_flat) * (1.0 / HEAD_DIM)
        yc = y_flat - mu
        yn = yc * lax.rsqrt(_head_sum(yc * yc) * (1.0 / HEAD_DIM) + RWKV_GN_EPS)
        o_ref[i] = (yn * lnw_ref[...] + lnb_ref[...] + bonus_ref[i]) * g_ref[i]


def _rwkv_scan(kh, rh, kb, bb, kbe, bbe, v, pe, g, bonus, ln_w, ln_b, s0):
    b, t, wd = kh.shape
    c = min(RWKV_CHUNK, t)
    assert c == HEAD_DIM
    bt = _pick(b, (4, 2, 1))
    fspec = pl.BlockSpec((bt, c, wd), lambda bi, i: (bi, i, 0))
    sspec = pl.BlockSpec((bt, wd, wd), lambda bi, i: (bi, 0, 0))
    vec = pl.BlockSpec((1, wd), lambda bi, i: (0, 0))
    return pl.pallas_call(
        _rwkv_scan_kernel,
        grid=(b // bt, t // c),
        in_specs=[fspec] * 10 + [vec, vec, sspec],
        out_specs=[fspec, sspec],
        out_shape=[jax.ShapeDtypeStruct((b, t, wd), F32), jax.ShapeDtypeStruct((b, wd, wd), F32)],
        compiler_params=_cparams(("parallel", "arbitrary")),
        name="rwkv_scan",
    )(kh, rh, kb, bb, kbe, bbe, v, pe, g, bonus, ln_w.reshape(1, wd), ln_b.reshape(1, wd), s0)


COL_QN, COL_QD, COL_QI = 0, 4, 8
COL_KSLC, COL_VSLC, COL_KWIN, COL_VWIN, COL_KD, COL_VD, COL_KI = 12, 13, 14, 15, 16, 17, 18
SLOPES = tuple(2.0 ** (-8.0 * (h + 1) / N_HEADS) for h in range(N_HEADS))


def _qcols(cols, first):
    return [jnp.broadcast_to(cols[:, first + h:first + h + 1] * (HEAD_DIM ** -0.5), (HEAD_DIM, LANE)) for h in range(N_HEADS)]


def _col_dot(mat_t, qb):
    return jnp.sum(mat_t * qb, axis=0, keepdims=True)


def _sidx_kernel(pt_ref, cols_ref, misc_ref, *refs, n_pages, page):
    pages, o_ref = refs[:n_pages], refs[n_pages]
    cols = cols_ref[0]
    qi = _qcols(cols, COL_QI)
    w = [misc_ref[0, 0:1, 12 + h:13 + h] * (N_HEADS ** -0.5) for h in range(N_HEADS)]

    def index_of(kt, qs):
        idx = None
        for h in range(N_HEADS):
            term = jnp.maximum(_col_dot(kt, qs[h]), 0.0) * w[h]
            idx = term if idx is None else idx + term
        return idx

    for p in range(n_pages):
        o_ref[0, 0:1, p * page:(p + 1) * page] = index_of(pages[p][0, 0, 0], qi)
    idx_self = index_of(cols[:, COL_KI:COL_KI + 1], [q[:, 0:1] for q in qi])
    lane = lax.broadcasted_iota(I32, (1, LANE), 1)
    o_ref[0, 0:1, n_pages * page:n_pages * page + LANE] = jnp.where(lane == 0, idx_self, NEG)


def _sample_index(page_table, cols, misc, cache_d, layer):
    b, n_pages = page_table.shape
    page = cache_d.shape[-1]
    width = n_pages * page + LANE
    kern = functools.partial(_sidx_kernel, n_pages=n_pages, page=page)
    page_spec = lambda p: pl.BlockSpec((1, 1, 1, HEAD_DIM, page), lambda bi, pt: (layer, pt[bi, p], 2, 0, 0))
    gs = pltpu.PrefetchScalarGridSpec(
        num_scalar_prefetch=1, grid=(b,),
        in_specs=[pl.BlockSpec((1, HEAD_DIM, LANE), lambda bi, pt: (bi, 0, 0)),
                  pl.BlockSpec((1, 8, LANE), lambda bi, pt: (bi, 0, 0))] + [page_spec(p) for p in range(n_pages)],
        out_specs=pl.BlockSpec((1, 1, width), lambda bi, pt: (bi, 0, 0)))
    return pl.pallas_call(kern, grid_spec=gs, out_shape=jax.ShapeDtypeStruct((b, 1, width), F32),
                          compiler_params=_cparams(("parallel",)), name="sample_index",
                          )(page_table, cols, misc, *([cache_d] * n_pages))


def _stopk_kernel(idx_ref, o_ref, *, topk):
    key = _sortable_key(idx_ref[...])
    rows, width = key.shape

    def bit_body(i, ans):
        cand = ans + lax.shift_left(jnp.int32(1), 31 - i)
        cnt = jnp.sum(jnp.where(key >= cand, 1, 0), axis=1, keepdims=True)
        return jnp.where(cnt >= topk, cand, ans)

    thr = lax.fori_loop(0, 32, bit_body, jnp.full((rows, 1), INT_MIN, I32))
    need = (topk - jnp.sum(jnp.where(key > thr, 1, 0), axis=1, keepdims=True)).astype(F32)
    tri = jnp.where(lax.broadcasted_iota(I32, (LANE, LANE), 0) < lax.broadcasted_iota(I32, (LANE, LANE), 1), 1.0, 0.0).astype(BF16)
    running = jnp.zeros((rows, 1), F32)
    for c in range(width // LANE):
        k = key[:, c * LANE:(c + 1) * LANE]
        eq = k == thr
        eqf = jnp.where(eq, 1.0, 0.0)
        before = _dot(eqf.astype(BF16), tri) + running
        sel = ((k > thr) | (eq & (before < need))) & (k > _KEY_HALF_NEG)
        o_ref[:, c * LANE:(c + 1) * LANE] = jnp.where(sel, 1.0, 0.0)
        running = running + jnp.sum(eqf, axis=1, keepdims=True)


def _sample_topk(idx, topk):
    return pl.pallas_call(functools.partial(_stopk_kernel, topk=topk),
                          out_shape=jax.ShapeDtypeStruct(idx.shape, F32),
                          compiler_params=pltpu.CompilerParams(vmem_limit_bytes=VMEM_LIMIT), name="sample_topk")(idx)


SEQ_PER_STEP = 2


def _sattn_kernel(pt_ref, qn_ref, qd_ref, misc_ref, kslc_ref, kwin_ref, kdsa_ref, dsel_ref, win_ref, w1a_ref, w1b_ref,
                  posa_ref, posb_ref, w2_ref, c2s_ref, exp_ref, *refs, n_pages, page, group):
    o_ref, win_o_ref, xt_scr = refs[2 * group * n_pages:2 * group * n_pages + 3]
    for g in range(group):
        one = pl.ds(g, 1)
        _sattn_one(qn_ref.at[one], qd_ref.at[one], misc_ref.at[one], kslc_ref.at[one], kwin_ref.at[one], kdsa_ref.at[one],
                   dsel_ref.at[one], win_ref.at[:, one], w1a_ref, w1b_ref, posa_ref, posb_ref, w2_ref, c2s_ref, exp_ref,
                   refs[g * n_pages:(g + 1) * n_pages], refs[(group + g) * n_pages:(group + g + 1) * n_pages],
                   o_ref.at[one], win_o_ref.at[one], xt_scr.at[g], n_pages, page)


def _sattn_one(qn_ref, qd_ref, misc_ref, kslc_ref, kwin_ref, kdsa_ref, dsel_ref, win_ref, w1a_ref, w1b_ref, posa_ref,
               posb_ref, w2_ref, c2s_ref, exp_ref, nsa_pages, dsa_pages, o_ref, win_o_ref, xt_scr, n_pages, page):
    past = n_pages * page
    q_pos = past
    rows8 = lax.broadcasted_iota(I32, (8, LANE), 0)
    head_row = rows8[:, 0:1]
    live = head_row < N_HEADS

    def head_rows(ref):
        out = jnp.zeros((8, LANE), F32)
        for h in range(N_HEADS):
            out = jnp.where(rows8 == h, ref[0, :, h * LANE:(h + 1) * LANE], out)
        return out * (HEAD_DIM ** -0.5)

    def per_head(vals):
        out = jnp.zeros((8, 1), F32)
        for h in range(N_HEADS):
            out = jnp.where(head_row == h, vals[h], out)
        return out

    qn, qd = head_rows(qn_ref), head_rows(qd_ref)
    slope = per_head(SLOPES)
    gates = _sigmoid(misc_ref[0, :, 0:3 * N_HEADS])
    gate = [per_head([gates[:, 3 * h + c:3 * h + c + 1] for h in range(N_HEADS)]) for c in range(3)]

    def attend(q, k_t, v_t, mask, dist, new_pair, self_mask):
        sm = jnp.where(mask, _dot(q[:, 0:HEAD_DIM].astype(BF16), k_t) - slope * dist.astype(F32), NEG)
        s_self = jnp.where(self_mask, jnp.sum(q * new_pair, axis=1, keepdims=True), NEG)
        mx = jnp.maximum(s_self, jnp.max(sm, axis=1, keepdims=True))
        e = jnp.where(mask, jnp.exp(sm - mx), 0.0)
        e_self = jnp.where(self_mask, jnp.exp(s_self - mx), 0.0)
        l = e_self + jnp.sum(e, axis=1, keepdims=True)
        o = _dot_nt(e.astype(BF16), v_t) + e_self * new_pair[:, HEAD_DIM:2 * HEAD_DIM]
        return o / jnp.where(l > 0.0, l, 1.0)

    for p in range(n_pages):
        xt_scr[p * page:(p + 1) * page, :] = nsa_pages[p][0, 0, 0:2].reshape(2 * HEAD_DIM, page).T
    n16 = past // NSA_CMP_STRIDE
    x_all = jnp.concatenate([xt_scr[pl.ds(r, n16, stride=NSA_CMP_STRIDE), :] for r in range(NSA_CMP_STRIDE)], axis=1)
    a = _dot((x_all + posa_ref[...]).astype(BF16), w1a_ref[...])
    b = _dot((x_all + posb_ref[...]).astype(BF16), w1b_ref[...])
    hid = _gelu_tanh(a + pltpu.roll(b, n16 - 1, axis=0))
    comp = _dot(hid.astype(BF16), w2_ref[...]).astype(BF16)

    blk = lax.broadcasted_iota(I32, (1, n16), 1)
    d_cmp = q_pos - (blk * NSA_CMP_STRIDE + NSA_CMP_LEN - 1)
    m_cmp = (d_cmp >= 0) & live
    sm = jnp.where(m_cmp, _dot_nt(qn.astype(BF16), comp) - slope * d_cmp.astype(F32), NEG)
    e = jnp.where(m_cmp, jnp.exp(sm - jnp.max(sm, axis=1, keepdims=True)), 0.0)
    l = jnp.sum(e, axis=1, keepdims=True)
    pr = e / jnp.where(l > 0.0, l, 1.0)
    o_cmp = _dot(pr.astype(BF16), comp)[:, HEAD_DIM:2 * HEAD_DIM]
    psum = jnp.sum(pr, axis=0, keepdims=True)

    n_slc = past // NSA_SLC_LEN + 1
    imp = _hdot(jnp.broadcast_to(psum, (8, n16)), c2s_ref[...])[0:1]
    j = lax.broadcasted_iota(I32, (1, LANE), 1)
    cur = q_pos // NSA_SLC_LEN
    forced = (j == 0) | (j == cur) | (j == cur - 1)
    score = jnp.where((j * NSA_SLC_LEN <= q_pos) & (j < n_slc), imp + jnp.where(forced, FORCE_BONUS, 0.0), NEG)
    srow = jnp.broadcast_to(score, (LANE, LANE))
    scol = srow.T
    jp = lax.broadcasted_iota(I32, (LANE, LANE), 0)
    jj = lax.broadcasted_iota(I32, (LANE, LANE), 1)
    rank = jnp.sum(jnp.where((scol > srow) | ((scol == srow) & (jp < jj)), 1, 0), axis=0, keepdims=True)
    sel = jnp.where((rank < min(NSA_TOPN, n_slc)) & (score > 0.5 * NEG), 1.0, 0.0)
    sel_pos = _dot(jnp.broadcast_to(sel, (8, LANE)).astype(BF16), exp_ref[...])[0:1]

    cat = lambda refs, comp_i: jnp.concatenate([r[0, 0, comp_i] for r in refs], axis=1).astype(BF16)
    dist = q_pos - lax.broadcasted_iota(I32, (1, past), 1)
    m_slc = (sel_pos[:, 0:past] > 0.5) & live
    m_dsa = (dsel_ref[0, :, 0:past] > 0.5) & live
    self_slc = (sel_pos[:, past:past + 1] > 0.5) & live
    self_dsa = (dsel_ref[0, :, past:past + 1] > 0.5) & live
    wb = win_ref.shape[-1]
    d_win = q_pos - (past - wb + lax.broadcasted_iota(I32, (1, wb), 1))
    m_win = (d_win >= 0) & (d_win <= NSA_WINDOW) & (d_win <= q_pos) & live
    o_slc = attend(qn, cat(nsa_pages, 2), cat(nsa_pages, 3), m_slc, dist, kslc_ref[0], self_slc)
    o_win = attend(qn, win_ref[0, 0, 0].astype(BF16), win_ref[0, 0, 1].astype(BF16), m_win, d_win, kwin_ref[0], live)
    o_dsa = attend(qd, cat(dsa_pages, 0), cat(dsa_pages, 1), m_dsa, dist, kdsa_ref[0], self_dsa)
    o_nsa = gate[0] * o_cmp + gate[1] * o_slc + gate[2] * o_win
    o_ref[0] = jnp.concatenate([o_nsa, o_dsa], axis=1)

    new_cols = jnp.broadcast_to(kwin_ref[0], (LANE, LANE)).T
    lane_w = lax.broadcasted_iota(I32, (HEAD_DIM, wb), 1)
    for c in range(2):
        win_o_ref[0, c] = jnp.where(lane_w == wb - 1, new_cols[c * HEAD_DIM:(c + 1) * HEAD_DIM, 0:1],
                                    pltpu.roll(win_ref[0, 0, c], wb - 1, axis=1))


def _sample_attention(page_table, z, dsel, win_t, cmp_w, cache_n, cache_d, layer):
    b, n_pages = page_table.shape
    page = cache_n.shape[-1]
    past = n_pages * page
    wb = win_t.shape[-1]
    n16 = past // NSA_CMP_STRIDE
    w1a, w1b, pos_a, pos_b, w2 = cmp_w
    n_slc = past // NSA_SLC_LEN + 1
    c2s = np.zeros((n16, LANE), np.float32)
    c2s[:, :n_slc] = _cmp_to_slc_t(n16, n_slc).T
    expand = (np.arange(past + LANE)[None, :] // NSA_SLC_LEN == np.arange(LANE)[:, None]) & (np.arange(past + LANE)[None, :] <= past)
    grp = _pick(b, (SEQ_PER_STEP, 1))
    kern = functools.partial(_sattn_kernel, n_pages=n_pages, page=page, group=grp)
    full = lambda shape: pl.BlockSpec(shape, lambda bi, pt: (0,) * len(shape))
    nspec = lambda g, p: pl.BlockSpec((1, 1, 4, HEAD_DIM, page), lambda bi, pt: (layer, pt[grp * bi + g, p], 0, 0, 0))
    dspec = lambda g, p: pl.BlockSpec((1, 1, 2, HEAD_DIM, page), lambda bi, pt: (layer, pt[grp * bi + g, p], 0, 0, 0))
    pages = [(g, p) for g in range(grp) for p in range(n_pages)]
    qw = N_HEADS * LANE
    zblk = lambda width, col: pl.BlockSpec((grp, 1, width), lambda bi, pt: (bi, 0, col // width))
    gs = pltpu.PrefetchScalarGridSpec(
        num_scalar_prefetch=1, grid=(b // grp,),
        in_specs=[zblk(qw, C_QN), zblk(qw, C_QD), zblk(LANE, C_MISC), zblk(LANE, C_K + P_SLC * LANE),
                  zblk(LANE, C_K + P_WIN * LANE), zblk(LANE, C_K + P_DSA * LANE),
                  pl.BlockSpec((grp, 1, past + LANE), lambda bi, pt: (bi, 0, 0)),
                  pl.BlockSpec((1, grp, 2, HEAD_DIM, wb), lambda bi, pt: (layer, bi, 0, 0, 0)),
                  full(w1a.shape), full(w1b.shape), full(pos_a.shape), full(pos_b.shape), full(w2.shape),
                  full(c2s.shape), full(expand.shape)]
                 + [nspec(g, p) for g, p in pages] + [dspec(g, p) for g, p in pages],
        out_specs=[pl.BlockSpec((grp, 8, LANE), lambda bi, pt: (bi, 0, 0)),
                   pl.BlockSpec((grp, 2, HEAD_DIM, wb), lambda bi, pt: (bi, 0, 0, 0))],
        scratch_shapes=[pltpu.VMEM((grp, past, 2 * HEAD_DIM), F32)])
    return pl.pallas_call(
        kern, grid_spec=gs,
        out_shape=[jax.ShapeDtypeStruct((b, 8, LANE), F32), jax.ShapeDtypeStruct((b, 2, HEAD_DIM, wb), F32)],
        compiler_params=_cparams(("parallel",)), name="sample_attention",
    )(page_table, z, z, z, z, z, z, dsel, win_t, w1a, w1b, pos_a, pos_b, w2, jnp.asarray(c2s), jnp.asarray(expand, BF16),
      *([cache_n] * (grp * n_pages)), *([cache_d] * (grp * n_pages)))


def _smix_kernel(zg_ref, zr_ref, shift_ref, s_ref, lng_ref, lnb_ref, gw_ref, gb_ref, mu_ref, w0_ref, w2_ref, a0_ref,
                 a2_ref, g2_ref, kkw_ref, ka_ref, rk_ref, lnw_ref, lnb2_ref, oc_ref, vc_ref, or_ref, so_ref):
    bt = zg_ref.shape[0]
    zg = _gelu_tanh(zg_ref[...])
    v = _layer_norm(zg[:, GMLP_WIDTH:2 * GMLP_WIDTH], lng_ref[...], lnb_ref[...])
    vc_ref[...] = v
    oc_ref[...] = zg[:, 0:GMLP_WIDTH] * (v * gw_ref[...] + gb_ref[...])

    feat = zr_ref[...]
    f = feat + mu_ref[...] * (shift_ref[...] - feat)
    r, lw, k, vv, kk, be, g, bonus = _rwkv_features(f, w0_ref[...], w2_ref[...], a0_ref[...], a2_ref[...], g2_ref[...],
                                                    kkw_ref[...], ka_ref[...], rk_ref[...])
    w = jnp.exp(lw)
    lane_t = lax.broadcasted_iota(I32, (LANE, LANE), 1)
    y_cols = [jnp.zeros((LANE, LANE), F32) for _ in range(N_HEADS // 2)]
    for i in range(bt):
        for hp in range(N_HEADS // 2):
            v_colb = jnp.broadcast_to(vv[i:i + 1, hp * LANE:(hp + 1) * LANE], (LANE, LANE)).T
            ys = []
            for hh in range(2):
                h = 2 * hp + hh
                sl = slice(h * HEAD_DIM, (h + 1) * HEAD_DIM)
                s0 = s_ref[i, h]
                sa = -jnp.sum(s0 * kk[i:i + 1, sl], axis=1, keepdims=True)
                s1 = (s0 * w[i:i + 1, sl] + sa * be[i:i + 1, sl]
                      + v_colb[hh * HEAD_DIM:(hh + 1) * HEAD_DIM, 0:HEAD_DIM] * k[i:i + 1, sl])
                so_ref[i, h] = s1
                ys.append(jnp.sum(s1 * r[i:i + 1, sl], axis=1, keepdims=True))
            y_cols[hp] = jnp.where(lane_t == i, jnp.concatenate(ys, axis=0), y_cols[hp])
    y = jnp.concatenate([yc.T[0:bt] for yc in y_cols], axis=1)
    mu = _head_sum(y) * (1.0 / HEAD_DIM)
    yc = y - mu
    yn = yc * lax.rsqrt(_head_sum(yc * yc) * (1.0 / HEAD_DIM) + RWKV_GN_EPS)
    or_ref[...] = (yn * lnw_ref[...] + lnb2_ref[...] + bonus) * g


def _sample_mixers(z, shift_prev, wkv_prev, lp, bt):
    b = z.shape[0]
    wd = RWKV_WIDTH
    pw = 4 * wd
    gw = jnp.repeat(lp['gmlp_ws'][:, 0, 0], HEAD_DIM).reshape(1, GMLP_WIDTH)
    gb = jnp.repeat(lp['gmlp_bs'][:, 0], HEAD_DIM).reshape(1, GMLP_WIDTH)
    vec = lambda n: pl.BlockSpec((1, n), lambda i: (0, 0))
    mat = lambda r, c: pl.BlockSpec((r, c), lambda i: (0, 0))
    row = lambda n: pl.BlockSpec((bt, n), lambda i: (i, 0))
    sspec = pl.BlockSpec((bt, N_HEADS, HEAD_DIM, HEAD_DIM), lambda i: (i, 0, 0, 0))
    fl = jax.ShapeDtypeStruct((b, wd), F32)
    return pl.pallas_call(
        _smix_kernel,
        grid=(b // bt,),
        in_specs=[pl.BlockSpec((bt, 2 * GMLP_WIDTH), lambda i: (i, C_GMLP // (2 * GMLP_WIDTH))),
                  pl.BlockSpec((bt, pw), lambda i: (i, C_RWKV // pw)),
                  row(pw), sspec, vec(wd), vec(wd), vec(wd), vec(wd),
                  vec(pw), vec(wd), mat(64, wd), vec(wd), mat(64, wd), mat(128, wd), vec(wd), vec(wd), vec(wd), vec(wd), vec(wd)],
        out_specs=[row(wd), row(wd), row(wd), sspec],
        out_shape=[fl, fl, fl, jax.ShapeDtypeStruct(wkv_prev.shape, F32)],
        compiler_params=_cparams(("parallel",)),
        name="sample_mixers",
    )(z, z, shift_prev, wkv_prev, lp['gmlp_ln_g'].reshape(1, wd), lp['gmlp_ln_b'].reshape(1, wd), gw, gb,
      lp['rwkv_mu'].reshape(1, pw), lp['rwkv_w0'].reshape(1, wd), lp['rwkv_w2'].astype(BF16),
      lp['rwkv_a0'].reshape(1, wd), lp['rwkv_a2'].astype(BF16), lp['rwkv_g2'].astype(BF16),
      lp['rwkv_kk'].reshape(1, wd), lp['rwkv_ka'].reshape(1, wd), lp['rwkv_rk'].reshape(1, wd),
      lp['rwkv_ln_w'].reshape(1, wd), lp['rwkv_ln_b'].reshape(1, wd))


_W_IN_COLS = (('q_nsa', 256), ('kv_nsa', 384), ('g_nsa', 12), ('q_dsa', 256), ('kv_dsa', 128), ('q_idx', 256),
              ('k_idx', 64), ('w_idx', 4), ('gmlp', 512), ('rwkv', 1024), ('merge', 4096))


def _proj_weights(w_in):
    d = w_in.shape[0]
    parts, off = {}, 0
    for name, width in _W_IN_COLS:
        parts[name] = w_in[:, off:off + width]
        off += width

    def pad_heads(w):
        w4 = w.reshape(d, N_HEADS, HEAD_DIM)
        return jnp.concatenate([w4, jnp.zeros_like(w4)], axis=-1).reshape(d, N_HEADS * LANE)

    zeros = lambda n: jnp.zeros((d, n), w_in.dtype)
    w = jnp.concatenate([parts['merge'], parts['rwkv'], parts['gmlp'], pad_heads(parts['q_nsa']), pad_heads(parts['q_dsa']),
                         parts['g_nsa'], zeros(LANE - 12), parts['kv_nsa'], parts['kv_dsa']], axis=1)
    w_idx = jnp.concatenate([pad_heads(parts['q_idx']), parts['k_idx'], zeros(LANE - HEAD_DIM),
                             parts['w_idx'], zeros(LANE - N_HEADS)], axis=1)
    return w.astype(BF16), _split_bf16(w_idx)


def _compress_weights_rows(w1, w2, pos):
    half = NSA_CMP_STRIDE * HEAD_DIM
    return (w1[:, :half].astype(BF16), w1[:, half:].astype(BF16),
            pos[:, :NSA_CMP_STRIDE].reshape(2, 1, half), pos[:, NSA_CMP_STRIDE:].reshape(2, 1, half), w2.astype(BF16))


def _compress_weights_pairs(w1, w2, pos):
    hid = w1.shape[-1]
    w1r = w1.reshape(2, 2, NSA_CMP_STRIDE, HEAD_DIM, hid)
    posr = pos.reshape(2, 2, NSA_CMP_STRIDE, HEAD_DIM)
    bigs, poss = [], []
    for half in range(2):
        big = jnp.zeros((NSA_CMP_STRIDE, 2, HEAD_DIM, 2, hid), w1.dtype)
        for c in range(2):
            big = big.at[:, c, :, c, :].set(w1r[c, half])
        bigs.append(big.reshape(NSA_CMP_STRIDE * 2 * HEAD_DIM, 2 * hid).astype(BF16))
        poss.append(jnp.transpose(posr[:, half], (1, 0, 2)).reshape(1, NSA_CMP_STRIDE * 2 * HEAD_DIM))
    w2b = jnp.zeros((2, hid, 2, HEAD_DIM), w2.dtype)
    for c in range(2):
        w2b = w2b.at[c, :, c, :].set(w2[c])
    return bigs[0], bigs[1], poss[0], poss[1], w2b.reshape(2 * hid, 2 * HEAD_DIM).astype(BF16)


def _pick(n, cands):
    for c in cands:
        if n % c == 0:
            return c
    return n


def kernel(x_prompt, x_sample, cache_nsa, cache_dsa, state_nsa_win, state_rwkv_shift, state_rwkv_wkv, page_table,
           c_prompt, c_sample, w_ada, b_ada, norm_mix_g, norm_ffn_g, w_in, nsa_cmp_w1, nsa_cmp_w2, nsa_cmp_pos,
           gmlp_ln_g, gmlp_ln_b, gmlp_ws, gmlp_bs, rwkv_mu, rwkv_w0, rwkv_w2, rwkv_a0, rwkv_a2, rwkv_g2, rwkv_kk,
           rwkv_ka, rwkv_rk, rwkv_ln_w, rwkv_ln_b, w_branch, w_out, w_ffn_in, w_ffn_out, final_norm_g):
    depth = w_ada.shape[0]
    bp, t, d = x_prompt.shape
    bs = x_sample.shape[0]
    assert x_sample.shape[1] == 1 and t % CK == 0 and t % (8 * NSA_CMP_STRIDE) == 0
    cache_n = jnp.transpose(cache_nsa, (0, 1, 3, 4, 2))
    cache_d = jnp.transpose(cache_dsa, (0, 1, 3, 4, 2))
    win_t = jnp.transpose(state_nsa_win, (0, 1, 3, 4, 2))
    past = page_table.shape[1] * cache_nsa.shape[2]
    c_all = jnp.concatenate([c_prompt, c_sample], axis=0)
    xp, xs = x_prompt, x_sample.reshape(1, bs, d)
    tm_p = _pick(t, (512, 256, 128))
    tm_f = _pick(t, (1024, 512, 256, 128))
    tf = _pick(w_ffn_out.shape[1], (256, 128))
    tn = _pick(N_PROJ, (2432,))
    outs = {k: [] for k in ('rows_n_p', 'rows_n_s', 'rows_d_p', 'rows_d_s', 'win_p', 'win_s', 'v_s', 'shift_p', 'shift_s',
                            'wkv_p', 'wkv_s')}
    for l in range(depth):
        lp = {'gmlp_ln_g': gmlp_ln_g[l], 'gmlp_ln_b': gmlp_ln_b[l], 'gmlp_ws': gmlp_ws[l], 'gmlp_bs': gmlp_bs[l],
              'rwkv_mu': rwkv_mu[l], 'rwkv_w0': rwkv_w0[l], 'rwkv_w2': rwkv_w2[l], 'rwkv_a0': rwkv_a0[l],
              'rwkv_a2': rwkv_a2[l], 'rwkv_g2': rwkv_g2[l], 'rwkv_kk': rwkv_kk[l], 'rwkv_ka': rwkv_ka[l],
              'rwkv_rk': rwkv_rk[l], 'rwkv_ln_w': rwkv_ln_w[l], 'rwkv_ln_b': rwkv_ln_b[l]}
        last = l == depth - 1
        mod = _ada(c_all, w_ada[l], b_ada[l]).reshape(bp + bs, 6, d)
        mod_p = [mod[:bp, i:i + 1] for i in range(6)]
        mod_s = [mod[bp:, i][None] for i in range(6)]
        w_proj, (wi_hi, wi_lo) = _proj_weights(w_in[l])
        wb, wo = w_branch[l].astype(BF16), w_out[l].astype(BF16)
        wfi, wfo = w_ffn_in[l].astype(BF16), w_ffn_out[l].astype(BF16)

        zp = _inproj(xp, norm_mix_g[l], mod_p[0], mod_p[1], w_proj, tm_p, tn)
        zi = _inproj_hp(xp, norm_mix_g[l], mod_p[0], mod_p[1], wi_hi, wi_lo, tm_p)
        outs['rows_n_p'].append(zp[..., C_K:C_K + 4 * HEAD_DIM].reshape(bp, t, 4, HEAD_DIM))
        outs['rows_d_p'].append(jnp.concatenate([zp[..., C_K + 3 * LANE:C_K + 4 * LANE], zi[..., ZI_K:ZI_K + HEAD_DIM]],
                                                axis=-1).reshape(bp, t, 3, HEAD_DIM))
        wn = min(NSA_WINDOW, t)
        outs['win_p'].append(zp[:, t - wn:, C_K + 2 * LANE:C_K + 3 * LANE].reshape(bp, wn, 2, HEAD_DIM))
        outs['shift_p'].append(zp[:, t - 1, C_RWKV:C_RWKV + 4 * RWKV_WIDTH])
        n16 = t // NSA_CMP_STRIDE
        r = jnp.stack([zp[..., C_K:C_K + HEAD_DIM].reshape(bp, n16, NSA_CMP_STRIDE * HEAD_DIM),
                       zp[..., C_K + HEAD_DIM:C_K + 2 * HEAD_DIM].reshape(bp, n16, NSA_CMP_STRIDE * HEAD_DIM)], axis=1)
        w1a, w1b, pos_a, pos_b, w2 = _compress_weights_rows(nsa_cmp_w1[l], nsa_cmp_w2[l], nsa_cmp_pos[l])
        cb, ct = _nsa_compress(r, pos_a, pos_b, w1a, w1b, w2)
        kb, kt = _kprep(zp)
        o_att = _prompt_attention(zp, zi, kb, kt, _kprep_idx(zi), cb, ct)
        o_c = _gmlp(zp, lp['gmlp_ln_g'], lp['gmlp_ln_b'], lp['gmlp_ws'], lp['gmlp_bs'], _pick(t, (512, 256, 128)))
        pre = _rwkv_pre(zp, jnp.zeros((bp, 1, 4 * RWKV_WIDTH), F32), lp, _pick(t, (256, 128)))
        o_r, wkv_bd = _rwkv_scan(*pre, lp['rwkv_ln_w'], lp['rwkv_ln_b'], jnp.zeros((bp, RWKV_WIDTH, RWKV_WIDTH), F32))
        outs['wkv_p'].append(jnp.stack([wkv_bd[:, h * HEAD_DIM:(h + 1) * HEAD_DIM, h * HEAD_DIM:(h + 1) * HEAD_DIM]
                                        for h in range(N_HEADS)], axis=1))
        xp = _merge(zp, o_att, o_c, o_r, wb, wo, xp, mod_p[2], tm_p)
        xp = _ffn(xp, norm_ffn_g[l], mod_p[3], mod_p[4], mod_p[5], wfi, wfo, final_norm_g, last, tm_f, tf)

        zs = _inproj(xs, norm_mix_g[l], mod_s[0], mod_s[1], w_proj, bs, tn)
        z2 = zs[0]
        zi2 = _inproj_hp(xs, norm_mix_g[l], mod_s[0], mod_s[1], wi_hi, wi_lo, bs)[0]
        k_idx_new = zi2[:, ZI_K:ZI_K + HEAD_DIM]
        outs['rows_n_s'].append(z2[:, C_K:C_K + 4 * HEAD_DIM].reshape(bs, 1, 4, HEAD_DIM))
        outs['rows_d_s'].append(jnp.concatenate([z2[:, C_K + 3 * LANE:C_K + 4 * LANE], k_idx_new],
                                                axis=-1).reshape(bs, 1, 3, HEAD_DIM))
        outs['shift_s'].append(z2[:, C_RWKV:C_RWKV + 4 * RWKV_WIDTH])
        heads = lambda zz, c0: zz[:, c0:c0 + N_HEADS * LANE].reshape(bs, N_HEADS, LANE)[:, :, :HEAD_DIM]
        vecs = jnp.concatenate([heads(z2, C_QN), heads(z2, C_QD), heads(zi2, ZI_Q),
                                z2[:, C_K + LANE:C_K + 4 * LANE].reshape(bs, 6, HEAD_DIM), k_idx_new[:, None]], axis=1)
        cols = jnp.transpose(jnp.pad(vecs, ((0, 0), (0, LANE - vecs.shape[1]), (0, 0))), (0, 2, 1))
        misc_row = jnp.concatenate([z2[:, C_MISC:C_MISC + 3 * N_HEADS], zi2[:, ZI_W:ZI_W + N_HEADS],
                                    jnp.zeros((bs, LANE - 4 * N_HEADS), F32)], axis=1)
        misc = jnp.broadcast_to(misc_row[:, None], (bs, 8, LANE))
        idx = _sample_index(page_table, cols, misc, cache_d, l)
        width = idx.shape[-1]
        dsel = _sample_topk(idx.reshape(bs, width), min(DSA_TOPK, (past + 1) // 4)).reshape(bs, 1, width)
        cmp_w = _compress_weights_pairs(nsa_cmp_w1[l], nsa_cmp_w2[l], nsa_cmp_pos[l])
        o_rows, win_new = _sample_attention(page_table, zs.reshape(bs, 1, N_PROJ), dsel, win_t, cmp_w, cache_n, cache_d, l)
        o_heads = jnp.concatenate([o_rows[:, 0:N_HEADS, 0:HEAD_DIM].reshape(bs, N_HEADS * HEAD_DIM),
                                   o_rows[:, 0:N_HEADS, HEAD_DIM:2 * HEAD_DIM].reshape(bs, N_HEADS * HEAD_DIM)], axis=1)
        outs['win_s'].append(jnp.transpose(win_new, (0, 3, 1, 2)))
        o_cs, v_cs, o_rs, wkv_s = _sample_mixers(z2, state_rwkv_shift[l], state_rwkv_wkv[l], lp, 8)
        outs['v_s'].append(v_cs.reshape(bs, 1, GMLP_WIDTH))
        outs['wkv_s'].append(wkv_s)
        xs = _merge(zs, o_heads[None], o_cs[None], o_rs[None], wb, wo, xs, mod_s[2], bs)
        xs = _ffn(xs, norm_ffn_g[l], mod_s[3], mod_s[4], mod_s[5], wfi, wfo, final_norm_g, last, bs, tf)

    st = lambda k: jnp.stack(outs[k])
    return (xp, xs.reshape(bs, 1, d), st('rows_n_p'), st('rows_n_s'), st('rows_d_p'), st('rows_d_s'), st('win_p'),
            st('win_s'), st('v_s'), st('shift_p'), st('shift_s'), st('wkv_p'), st('wkv_s'))
```

```python
import functools

import numpy as np
import jax
import jax.numpy as jnp
from jax import lax
from jax.experimental import pallas as pl
from jax.experimental.pallas import tpu as pltpu

F32 = jnp.float32
BF16 = jnp.bfloat16
I32 = jnp.int32

HEAD_DIM = 64
N_HEADS = 4
NSA_CMP_LEN = 32
NSA_CMP_STRIDE = 16
NSA_SLC_LEN = 64
NSA_TOPN = 8
NSA_WINDOW = 512
DSA_TOPK = 256
GMLP_GROUPS = 4
GMLP_WIDTH = GMLP_GROUPS * HEAD_DIM
CHUNK = 128
RWKV_WIDTH = N_HEADS * HEAD_DIM
RWKV_GN_EPS = 64e-5
QB = 128
EPS = 1e-6
LN_EPS = 1e-5
NEG = -1e30
FORCE_BONUS = 1e4
LANE = 128
VMEM_LIMIT = 56 * 1024 * 1024

C_MERGE, C_RWKV, C_GMLP = 0, 4096, 5120
C_QN, C_QD, C_MISC, C_K = 5632, 6144, 6656, 6784
N_PROJ = 7296
K_PAIRS = 4
P_CMP, P_SLC, P_WIN, P_DSA = range(K_PAIRS)
ZI_Q, ZI_K, ZI_W, N_HP = 0, 512, 640, 768


def _cparams(sem):
    return pltpu.CompilerParams(dimension_semantics=sem, vmem_limit_bytes=VMEM_LIMIT)


def _dot(a, b):
    return jnp.dot(a, b, preferred_element_type=F32)


def _dot_nt(a, b):
    return lax.dot_general(a, b, (((1,), (1,)), ((), ())), preferred_element_type=F32)


def _dot_tn(a, b):
    return lax.dot_general(a, b, (((0,), (0,)), ((), ())), preferred_element_type=F32)


def _gelu_tanh(x):
    return 0.5 * x * (1.0 + jnp.tanh(np.sqrt(2.0 / np.pi).astype(np.float32) * (x + 0.044715 * (x * x * x))))


def _sigmoid(x):
    return 0.5 * jnp.tanh(0.5 * x) + 0.5


def _rms_mod(x, g, scale, shift):
    ms = jnp.mean(x * x, axis=-1, keepdims=True)
    return (x * lax.rsqrt(ms + EPS) * g) * (1.0 + scale) + shift


def _ada_kernel(c_ref, w_ref, b_ref, o_ref):
    c = c_ref[...]
    s = (c * _sigmoid(c)).astype(BF16)
    o_ref[...] = _dot(s, w_ref[...].astype(BF16)) + b_ref[...]


def _ada(c, w, b):
    m, d = c.shape
    n = w.shape[1]
    tn = 1536
    return pl.pallas_call(
        _ada_kernel,
        grid=(n // tn,),
        in_specs=[pl.BlockSpec((m, d), lambda j: (0, 0)),
                  pl.BlockSpec((d, tn), lambda j: (0, j)),
                  pl.BlockSpec((1, tn), lambda j: (0, j))],
        out_specs=pl.BlockSpec((m, tn), lambda j: (0, j)),
        out_shape=jax.ShapeDtypeStruct((m, n), F32),
        compiler_params=_cparams(("parallel",)),
        name="ada",
    )(c, w, b.reshape(1, n))


def _inproj_kernel(x_ref, g_ref, sh_ref, sc_ref, w_ref, o_ref):
    h = _rms_mod(x_ref[0], g_ref[...], sc_ref[0], sh_ref[0])
    o_ref[0] = _dot(h.astype(BF16), w_ref[...])


def _inproj(x, g, shift, scale, w, tm, tn):
    b, t, d = x.shape
    n = w.shape[1]
    tmod = tm if shift.shape[1] == t else 1
    mod_map = (lambda j, bi, i: (bi, i, 0)) if shift.shape[1] == t else (lambda j, bi, i: (bi, 0, 0))
    return pl.pallas_call(
        _inproj_kernel,
        grid=(n // tn, b, t // tm),
        in_specs=[pl.BlockSpec((1, tm, d), lambda j, bi, i: (bi, i, 0)),
                  pl.BlockSpec((1, d), lambda j, bi, i: (0, 0)),
                  pl.BlockSpec((1, tmod, d), mod_map),
                  pl.BlockSpec((1, tmod, d), mod_map),
                  pl.BlockSpec((d, tn), lambda j, bi, i: (0, j))],
        out_specs=pl.BlockSpec((1, tm, tn), lambda j, bi, i: (bi, i, j)),
        out_shape=jax.ShapeDtypeStruct((b, t, n), F32),
        compiler_params=_cparams(("parallel", "parallel", "parallel")),
        name="inproj",
    )(x, g.reshape(1, d), shift, scale, w)


def _split_bf16(x):
    hi = x.astype(BF16)
    return hi, (x - hi.astype(F32)).astype(BF16)


def _inproj_hp_kernel(x_ref, g_ref, sh_ref, sc_ref, wh_ref, wl_ref, o_ref):
    h_hi, h_lo = _split_bf16(_rms_mod(x_ref[0], g_ref[...], sc_ref[0], sh_ref[0]))
    o_ref[0] = _dot(h_hi, wh_ref[...]) + (_dot(h_hi, wl_ref[...]) + _dot(h_lo, wh_ref[...]))


def _inproj_hp(x, g, shift, scale, w_hi, w_lo, tm):
    b, t, d = x.shape
    n = w_hi.shape[1]
    tmod = tm if shift.shape[1] == t else 1
    mod_map = (lambda bi, i: (bi, i, 0)) if shift.shape[1] == t else (lambda bi, i: (bi, 0, 0))
    return pl.pallas_call(
        _inproj_hp_kernel,
        grid=(b, t // tm),
        in_specs=[pl.BlockSpec((1, tm, d), lambda bi, i: (bi, i, 0)),
                  pl.BlockSpec((1, d), lambda bi, i: (0, 0)),
                  pl.BlockSpec((1, tmod, d), mod_map),
                  pl.BlockSpec((1, tmod, d), mod_map),
                  pl.BlockSpec((d, n), lambda bi, i: (0, 0)),
                  pl.BlockSpec((d, n), lambda bi, i: (0, 0))],
        out_specs=pl.BlockSpec((1, tm, n), lambda bi, i: (bi, i, 0)),
        out_shape=jax.ShapeDtypeStruct((b, t, n), F32),
        compiler_params=_cparams(("parallel", "parallel")),
        name="inproj_hp",
    )(x, g.reshape(1, d), shift, scale, w_hi, w_lo)


def _merge_kernel(zm_ref, oa_ref, oc_ref, or_ref, wb_ref, wo_ref, x_ref, gate_ref, o_ref):
    bw = wb_ref.shape[1]
    d = x_ref.shape[-1]
    outs = (oa_ref[0, :, 0:bw], oa_ref[0, :, bw:2 * bw], oc_ref[0], or_ref[0])
    mixed = None
    for n, o in enumerate(outs):
        br = _dot(o.astype(BF16), wb_ref[n])
        term = _sigmoid(zm_ref[0, :, n * d:(n + 1) * d]) * br
        mixed = term if mixed is None else mixed + term
    y = _dot(mixed.astype(BF16), wo_ref[...])
    o_ref[0] = x_ref[0] + gate_ref[0] * y


def _merge(z, o_att, o_c, o_r, w_branch, w_out, x, gate, tm):
    b, t, d = x.shape
    nb, bw, _ = w_branch.shape
    tmod = tm if gate.shape[1] == t else 1
    mod_map = (lambda bi, i: (bi, i, 0)) if gate.shape[1] == t else (lambda bi, i: (bi, 0, 0))
    return pl.pallas_call(
        _merge_kernel,
        grid=(b, t // tm),
        in_specs=[pl.BlockSpec((1, tm, nb * d), lambda bi, i: (bi, i, C_MERGE // (nb * d))),
                  pl.BlockSpec((1, tm, 2 * bw), lambda bi, i: (bi, i, 0)),
                  pl.BlockSpec((1, tm, bw), lambda bi, i: (bi, i, 0)),
                  pl.BlockSpec((1, tm, bw), lambda bi, i: (bi, i, 0)),
                  pl.BlockSpec((nb, bw, d), lambda bi, i: (0, 0, 0)),
                  pl.BlockSpec((d, d), lambda bi, i: (0, 0)),
                  pl.BlockSpec((1, tm, d), lambda bi, i: (bi, i, 0)),
                  pl.BlockSpec((1, tmod, d), mod_map)],
        out_specs=pl.BlockSpec((1, tm, d), lambda bi, i: (bi, i, 0)),
        out_shape=jax.ShapeDtypeStruct((b, t, d), F32),
        compiler_params=_cparams(("parallel", "parallel")),
        name="merge",
    )(z, o_att, o_c, o_r, w_branch, w_out, x, gate)


def _ffn_kernel(x_ref, g_ref, sh_ref, sc_ref, gate_ref, wg_ref, wu_ref, wd_ref, fg_ref, o_ref,
                h_scr, acc_scr, *, final_norm):
    k = pl.program_id(2)

    @pl.when(k == 0)
    def _():
        h_scr[...] = _rms_mod(x_ref[0], g_ref[...], sc_ref[0], sh_ref[0]).astype(BF16)
        acc_scr[...] = jnp.zeros_like(acc_scr)

    h = h_scr[...]
    gt = _dot(h, wg_ref[...])
    up = _dot(h, wu_ref[...])
    act = (gt * _sigmoid(gt)) * up
    acc_scr[...] += _dot(act.astype(BF16), wd_ref[...])

    @pl.when(k == pl.num_programs(2) - 1)
    def _():
        y = x_ref[0] + gate_ref[0] * acc_scr[...]
        if final_norm:
            ms = jnp.mean(y * y, axis=-1, keepdims=True)
            y = y * lax.rsqrt(ms + EPS) * fg_ref[...]
        o_ref[0] = y


def _ffn(x, g, shift, scale, gate, w_in, w_out, final_g, final_norm, tm, tf):
    b, t, d = x.shape
    ff = w_out.shape[0]
    nk = ff // tf
    tmod = tm if gate.shape[1] == t else 1
    mod_map = (lambda bi, i, k: (bi, i, 0)) if gate.shape[1] == t else (lambda bi, i, k: (bi, 0, 0))
    return pl.pallas_call(
        functools.partial(_ffn_kernel, final_norm=final_norm),
        grid=(b, t // tm, nk),
        in_specs=[pl.BlockSpec((1, tm, d), lambda bi, i, k: (bi, i, 0)),
                  pl.BlockSpec((1, d), lambda bi, i, k: (0, 0)),
                  pl.BlockSpec((1, tmod, d), mod_map),
                  pl.BlockSpec((1, tmod, d), mod_map),
                  pl.BlockSpec((1, tmod, d), mod_map),
                  pl.BlockSpec((d, tf), lambda bi, i, k: (0, k)),
                  pl.BlockSpec((d, tf), lambda bi, i, k: (0, nk + k)),
                  pl.BlockSpec((tf, d), lambda bi, i, k: (k, 0)),
                  pl.BlockSpec((1, d), lambda bi, i, k: (0, 0))],
        out_specs=pl.BlockSpec((1, tm, d), lambda bi, i, k: (bi, i, 0)),
        out_shape=jax.ShapeDtypeStruct((b, t, d), F32),
        scratch_shapes=[pltpu.VMEM((tm, d), BF16), pltpu.VMEM((tm, d), F32)],
        compiler_params=_cparams(("parallel", "parallel", "arbitrary")),
        name="ffn",
    )(x, g.reshape(1, d), shift, scale, gate, w_in, w_in, w_out, final_g.reshape(1, d))


def _cmp_kernel(r_ref, pa_ref, pb_ref, w1a_ref, w1b_ref, w2_ref, ob_ref, ot_ref):
    n16 = r_ref.shape[2]
    comp = []
    for z in range(2):
        r = r_ref[0, z]
        a = _dot((r + pa_ref[z]).astype(BF16), w1a_ref[z])
        b = _dot((r + pb_ref[z]).astype(BF16), w1b_ref[z])
        hid = _gelu_tanh(a + pltpu.roll(b, n16 - 1, axis=0))
        comp.append(_dot(hid.astype(BF16), w2_ref[z]))
    pair = jnp.concatenate(comp, axis=1)
    ob_ref[0] = pair.astype(BF16)
    ot_ref[0] = pair.T.astype(BF16)


def _nsa_compress(r, pos_a, pos_b, w1a, w1b, w2):
    b, _, n16, kd = r.shape
    hid = w1a.shape[-1]
    return pl.pallas_call(
        _cmp_kernel,
        grid=(b,),
        in_specs=[pl.BlockSpec((1, 2, n16, kd), lambda bi: (bi, 0, 0, 0)),
                  pl.BlockSpec((2, 1, kd), lambda bi: (0, 0, 0)),
                  pl.BlockSpec((2, 1, kd), lambda bi: (0, 0, 0)),
                  pl.BlockSpec((2, kd, hid), lambda bi: (0, 0, 0)),
                  pl.BlockSpec((2, kd, hid), lambda bi: (0, 0, 0)),
                  pl.BlockSpec((2, hid, HEAD_DIM), lambda bi: (0, 0, 0))],
        out_specs=[pl.BlockSpec((1, n16, 2 * HEAD_DIM), lambda bi: (bi, 0, 0)),
                   pl.BlockSpec((1, 2 * HEAD_DIM, n16), lambda bi: (bi, 0, 0))],
        out_shape=[jax.ShapeDtypeStruct((b, n16, 2 * HEAD_DIM), BF16),
                   jax.ShapeDtypeStruct((b, 2 * HEAD_DIM, n16), BF16)],
        compiler_params=_cparams(("parallel",)),
        name="nsa_compress",
    )(r, pos_a, pos_b, w1a, w1b, w2)


CK = 256
AUX_POS = 32


def _attn_aux(t):
    pos = np.arange(t)
    aux = np.zeros((t, LANE), np.float32)
    aux[pos, pos // NSA_SLC_LEN] = 1.0
    aux[:, AUX_POS] = pos % CK
    aux[:, AUX_POS + 1] = pos // CK
    return jnp.asarray(aux, BF16)


def _kprep_kernel(z_ref, aux_ref, ob_ref, ot_ref):
    x = z_ref[0]
    ob_ref[0, 0] = jnp.concatenate([x.astype(BF16), aux_ref[...]], axis=1)
    for c in range(x.shape[0] // CK):
        ot_ref[0, 0, c] = x[c * CK:(c + 1) * CK].T[HEAD_DIM:2 * HEAD_DIM].astype(BF16)


def _kprep(z):
    b, t, _ = z.shape
    tk = _pick(t, (1024, 512, CK))
    return pl.pallas_call(
        _kprep_kernel,
        grid=(b, K_PAIRS, t // tk),
        in_specs=[pl.BlockSpec((1, tk, LANE), lambda bi, p, c: (bi, c, C_K // LANE + p)),
                  pl.BlockSpec((tk, LANE), lambda bi, p, c: (c, 0))],
        out_specs=[pl.BlockSpec((1, 1, tk, 2 * LANE), lambda bi, p, c: (bi, p, c, 0)),
                   pl.BlockSpec((1, 1, tk // CK, HEAD_DIM, CK), lambda bi, p, c: (bi, p, c, 0, 0))],
        out_shape=[jax.ShapeDtypeStruct((b, K_PAIRS, t, 2 * LANE), BF16),
                   jax.ShapeDtypeStruct((b, K_PAIRS, t // CK, HEAD_DIM, CK), BF16)],
        compiler_params=_cparams(("parallel", "parallel", "parallel")),
        name="kprep",
    )(z, _attn_aux(t))


def _kprep_idx_kernel(z_ref, o_ref):
    x = z_ref[0]
    hi = x.astype(BF16).astype(F32)
    o_ref[0] = jnp.concatenate([hi + pltpu.roll(x - hi, HEAD_DIM, axis=1), hi], axis=1).astype(BF16)


def _kprep_idx(zi):
    b, t, _ = zi.shape
    tt = _pick(t, (512, 256, 128))
    return pl.pallas_call(
        _kprep_idx_kernel,
        grid=(b, t // tt),
        in_specs=[pl.BlockSpec((1, tt, LANE), lambda bi, i: (bi, i, ZI_K // LANE))],
        out_specs=pl.BlockSpec((1, tt, 2 * LANE), lambda bi, i: (bi, i, 0)),
        out_shape=jax.ShapeDtypeStruct((b, t, 2 * LANE), BF16),
        compiler_params=_cparams(("parallel", "parallel")),
        name="kprep_idx",
    )(zi)


INT_MIN = -2 ** 31


def _sortable_key(v):
    v = jnp.where(v == 0.0, 0.0, v)
    u = lax.bitcast_convert_type(v, I32)
    return jnp.where(u < 0, u ^ 0x7FFFFFFF, u)


_KEY_HALF_NEG = int(np.array(0.5 * NEG, np.float32).view(np.int32) ^ 0x7FFFFFFF)


def _lane_consts(q0):
    lane = lax.broadcasted_iota(I32, (1, N_HEADS * QB), 1)
    hl = lane >> 7
    q_pos = q0 + (lane & (QB - 1))
    slope = jnp.where(hl == 0, 2.0 ** -2, jnp.where(hl == 1, 2.0 ** -4, jnp.where(hl == 2, 2.0 ** -6, 2.0 ** -8)))
    return q_pos, slope.astype(F32)


def _tile_heads(x):
    return jnp.concatenate([x] * N_HEADS, axis=1)


def _place_heads(tiles):
    lane = lax.broadcasted_iota(I32, (QB, LANE), 1)
    out = []
    for t in range(N_HEADS // 2):
        out.append(jnp.where(lane < HEAD_DIM, pltpu.roll(tiles[2 * t], HEAD_DIM, axis=1), tiles[2 * t + 1]))
    return jnp.concatenate(out, axis=1)


def _attn_kernel(qn_ref, qd_ref, misc_ref, qi_ref, wi_ref, kb_ref, kt_ref, ki_ref, cb_ref, ct_ref, c2s_ref, o_ref,
                 key_scr, dsel_scr, *, seq_len, topk):
    t = seq_len
    q0 = pl.program_id(1) * QB
    nc = (q0 + QB + CK - 1) // CK
    hq = N_HEADS * QB
    scale = HEAD_DIM ** -0.5

    def stack_q(ref):
        return jnp.concatenate([ref[0, :, h * LANE:(h + 1) * LANE] for h in range(N_HEADS)], axis=0) * scale

    qn, qd = stack_q(qn_ref).astype(BF16), stack_q(qd_ref).astype(BF16)
    qi = stack_q(qi_ref)
    qi_cat = jnp.concatenate([(qi + pltpu.roll(qi, HEAD_DIM, axis=1)).astype(BF16),
                              (qi - qi.astype(BF16).astype(F32)).astype(BF16)], axis=1)
    misc_t = misc_ref[0].T
    wi_t = wi_ref[0].T
    q_pos, slope = _lane_consts(q0)

    n16 = cb_ref.shape[1]
    sc = _dot_nt(cb_ref[0], qn)
    cmp_end = lax.broadcasted_iota(I32, (n16, hq), 0) * NSA_CMP_STRIDE + (NSA_CMP_LEN - 1)
    d = q_pos - cmp_end
    mask = d >= 0
    sm = jnp.where(mask, sc - slope * d.astype(F32), NEG)
    e = jnp.where(mask, jnp.exp(sm - jnp.max(sm, axis=0, keepdims=True)), 0.0)
    l = jnp.sum(e, axis=0, keepdims=True)
    p = e / jnp.where(l > 0.0, l, 1.0)
    o_cmp_t = _dot(ct_ref[0], p.astype(BF16))
    psum_t = p[:, 0:QB]
    for h in range(1, N_HEADS):
        psum_t = psum_t + p[:, h * QB:(h + 1) * QB]
    imp_t = jnp.dot(c2s_ref[...], psum_t, preferred_element_type=F32, precision=lax.Precision.HIGHEST)

    nslc = c2s_ref.shape[0]
    jrow = lax.broadcasted_iota(I32, (nslc, QB), 0)
    qp = q0 + lax.broadcasted_iota(I32, (nslc, QB), 1)
    cur = qp >> 6
    adm = jrow * NSA_SLC_LEN <= qp
    forced = (jrow == 0) | (jrow == cur) | (jrow == cur - 1)
    score = jnp.where(adm, imp_t + jnp.where(forced, FORCE_BONUS, 0.0), NEG)
    rank = jnp.zeros((nslc, QB), I32)
    for j in range(nslc):
        row = score[j:j + 1, :]
        rank = rank + jnp.where((row > score) | ((row == score) & (jrow > j)), 1, 0)
    sel_bias = jnp.where((rank < min(NSA_TOPN, nslc)) & (score > 0.5 * NEG), 0.0, NEG)

    w_rows = [wi_t[h:h + 1, :] * (N_HEADS ** -0.5) for h in range(N_HEADS)]

    npair = (nc + 1) // 2

    def idx_body(c, carry):
        rows = pl.ds(pl.multiple_of(c * (2 * CK), 2 * CK), 2 * CK)
        lg = jnp.maximum(_dot_nt(ki_ref[0, rows, :], qi_cat), 0.0)
        idx = lg[:, 0:QB] * w_rows[0]
        for h in range(1, N_HEADS):
            idx = idx + lg[:, h * QB:(h + 1) * QB] * w_rows[h]
        kpos = c * (2 * CK) + lax.broadcasted_iota(I32, (2 * CK, QB), 0)
        causal = kpos <= q0 + lax.broadcasted_iota(I32, (2 * CK, QB), 1)
        key_scr[rows, :] = _sortable_key(jnp.where(causal, idx, NEG))
        return carry

    lax.fori_loop(0, npair, idx_body, 0)

    def search(n_spans):
        def count(pred_fn):
            acc = jnp.zeros((1, QB), I32)
            for c in range(n_spans):
                acc = acc + jnp.sum(jnp.where(pred_fn(key_scr[c * 2 * CK:(c + 1) * 2 * CK, :]), 1, 0), axis=0, keepdims=True)
            return acc

        def bit_body(i, ans):
            cand = ans + lax.shift_left(jnp.int32(1), 31 - i)
            return jnp.where(count(lambda k: k >= cand) >= topk, cand, ans)

        thr_ = lax.fori_loop(0, 32, bit_body, jnp.full((1, QB), INT_MIN, I32))
        return thr_, (topk - count(lambda k: k > thr_)).astype(F32)

    thr, need = lax.switch(npair - 1, [functools.partial(search, n) for n in range(1, t // (2 * CK) + 1)])

    tri = jnp.where(lax.broadcasted_iota(I32, (CK, CK), 1) < lax.broadcasted_iota(I32, (CK, CK), 0), 1.0, 0.0).astype(BF16)

    def tie_body(c, running):
        rows = pl.ds(pl.multiple_of(c * CK, CK), CK)
        k = key_scr[rows, :]
        eq = k == thr
        eqf = jnp.where(eq, 1.0, 0.0)
        before = _dot(tri, eqf.astype(BF16)) + running
        sel = ((k > thr) | (eq & (before < need))) & (k > _KEY_HALF_NEG)
        dsel_scr[rows, :] = jnp.where(sel, 0.0, NEG)
        return running + jnp.sum(eqf, axis=0, keepdims=True)

    lax.fori_loop(0, nc, tie_body, jnp.zeros((1, QB), F32))

    sel_bias_t = jnp.concatenate([sel_bias, jnp.zeros((LANE - nslc, QB), F32)], axis=0).T
    lane = lax.broadcasted_iota(I32, (QB, LANE), 1)
    q_slc, q_dsa = [], []
    for h in range(N_HEADS):
        pos_cols = jnp.where(lane == AUX_POS, SLOPES[h], jnp.where(lane == AUX_POS + 1, SLOPES[h] * CK, 0.0))
        q_slc.append(sel_bias_t + pos_cols)
        q_dsa.append(pos_cols)
    qn_cat = jnp.concatenate([qn, jnp.concatenate(q_slc, axis=0).astype(BF16)], axis=1)
    qd_cat = jnp.concatenate([qd, jnp.concatenate(q_dsa, axis=0).astype(BF16)], axis=1)
    def online(state, s, vt):
        m_old, l_old, acc_old = state
        m_new = jnp.maximum(m_old, jnp.max(s, axis=0, keepdims=True))
        alpha = jnp.exp(m_old - m_new)
        e_ = jnp.exp(s - m_new)
        return m_new, alpha * l_old + jnp.sum(e_, axis=0, keepdims=True), alpha * acc_old + _dot(vt, e_.astype(BF16))

    def scores(c, diagonal):
        rows = pl.ds(pl.multiple_of(c * CK, CK), CK)
        s_slc = _dot_nt(kb_ref[0, P_SLC, rows, :], qn_cat)
        if diagonal:
            kpos = c * CK + lax.broadcasted_iota(I32, (CK, hq), 0)
            s_slc = jnp.where(kpos <= q_pos, s_slc, NEG)
        return s_slc, _dot_nt(kb_ref[0, P_DSA, rows, :], qd_cat) + _tile_heads(dsel_scr[rows, :])

    def flash_chunks(chunks, state, diagonal=None):
        diagonal = diagonal or (False,) * len(chunks)
        sc = [scores(c, d) for c, d in zip(chunks, diagonal)]
        st_slc, st_dsa = state
        for c, (s_slc, s_dsa) in zip(chunks, sc):
            st_slc = online(st_slc, s_slc, kt_ref[0, P_SLC, c])
            st_dsa = online(st_dsa, s_dsa, kt_ref[0, P_DSA, c])
        return st_slc, st_dsa

    init = (jnp.full((1, hq), NEG, F32), jnp.zeros((1, hq), F32), jnp.zeros((HEAD_DIM, hq), F32))
    n_full = nc - 1
    state = lax.fori_loop(0, n_full // 2, lambda i, st: flash_chunks((2 * i, 2 * i + 1), st), (init, init))
    state = lax.cond(n_full % 2 == 1,
                     lambda st: flash_chunks((nc - 2, nc - 1), st, diagonal=(False, True)),
                     lambda st: flash_chunks((nc - 1,), st, diagonal=(True,)), state)

    def finish(st):
        m_, l_, acc = st
        return acc * jnp.where(m_ > 0.5 * NEG, 1.0 / l_, 0.0)

    o_slc_t, o_dsa_t = finish(state[0]), finish(state[1])

    wk = min(NSA_WINDOW + QB, t)
    ws = pl.multiple_of(jnp.clip(q0 - NSA_WINDOW, 0, t - wk), QB)
    kw = kb_ref[0, P_WIN, pl.ds(ws, wk), :]
    dw = q_pos - (ws + lax.broadcasted_iota(I32, (wk, hq), 0))
    qw_cat = jnp.concatenate([qn, qd_cat[:, LANE:2 * LANE]], axis=1)
    sm = jnp.where((dw >= 0) & (dw <= NSA_WINDOW), _dot_nt(kw, qw_cat), NEG)
    mw = jnp.max(sm, axis=0, keepdims=True)
    e = jnp.exp(sm - mw)
    lw = jnp.sum(e, axis=0, keepdims=True)
    o_win_t = _dot_tn(kw[:, 0:LANE], e.astype(BF16)) * jnp.where(mw > 0.5 * NEG, 1.0 / lw, 0.0)

    gates = _sigmoid(misc_t[0:3 * N_HEADS, :])
    nsa_tiles, dsa_tiles = [], []
    top = jnp.zeros((HEAD_DIM, QB), F32)
    vrows = slice(HEAD_DIM, 2 * HEAD_DIM)
    for h in range(N_HEADS):
        cols = slice(h * QB, (h + 1) * QB)
        on = (gates[3 * h:3 * h + 1, :] * o_cmp_t[vrows, cols] + gates[3 * h + 1:3 * h + 2, :] * o_slc_t[:, cols]
              + gates[3 * h + 2:3 * h + 3, :] * o_win_t[vrows, cols])
        nsa_tiles.append(jnp.concatenate([top, on], axis=0).T)
        dsa_tiles.append(jnp.concatenate([top, o_dsa_t[:, cols]], axis=0).T)
    o_ref[0, :, 0:N_HEADS * HEAD_DIM] = _place_heads(nsa_tiles)
    o_ref[0, :, N_HEADS * HEAD_DIM:2 * N_HEADS * HEAD_DIM] = _place_heads(dsa_tiles)


def _cmp_to_slc_t(n16, n_slc):
    start = np.arange(n16) * NSA_CMP_STRIDE
    bstart = np.arange(n_slc) * NSA_SLC_LEN
    ov = np.minimum(start[:, None] + NSA_CMP_LEN, bstart[None, :] + NSA_SLC_LEN) - np.maximum(start[:, None], bstart[None, :])
    return (np.clip(ov, 0, None) / NSA_CMP_LEN).T.astype(np.float32)


def _prompt_attention(z, zi, kb, kt, ki, cb, ct):
    b, t, _ = z.shape
    n16 = cb.shape[1]
    n_slc = -(-t // NSA_SLC_LEN)
    assert n_slc <= AUX_POS and t // CK <= 256 and t % (2 * CK) == 0
    hq = N_HEADS * QB
    qw = N_HEADS * LANE
    topk = min(DSA_TOPK, t // 4)
    c2s = jnp.asarray(_cmp_to_slc_t(n16, n_slc))
    qspec = lambda col: pl.BlockSpec((1, QB, qw), lambda bi, i: (bi, i, col // qw))
    tile = lambda col: pl.BlockSpec((1, QB, LANE), lambda bi, i: (bi, i, col // LANE))
    return pl.pallas_call(
        functools.partial(_attn_kernel, seq_len=t, topk=topk),
        grid=(b, t // QB),
        in_specs=[qspec(C_QN), qspec(C_QD), tile(C_MISC), qspec(ZI_Q), tile(ZI_W),
                  pl.BlockSpec((1, K_PAIRS, t, 2 * LANE), lambda bi, i: (bi, 0, 0, 0)),
                  pl.BlockSpec((1, K_PAIRS, t // CK, HEAD_DIM, CK), lambda bi, i: (bi, 0, 0, 0, 0)),
                  pl.BlockSpec((1, t, 2 * LANE), lambda bi, i: (bi, 0, 0)),
                  pl.BlockSpec((1, n16, LANE), lambda bi, i: (bi, 0, 0)),
                  pl.BlockSpec((1, LANE, n16), lambda bi, i: (bi, 0, 0)),
                  pl.BlockSpec((n_slc, n16), lambda bi, i: (0, 0))],
        out_specs=pl.BlockSpec((1, QB, 2 * N_HEADS * HEAD_DIM), lambda bi, i: (bi, i, 0)),
        out_shape=jax.ShapeDtypeStruct((b, t, 2 * N_HEADS * HEAD_DIM), F32),
        scratch_shapes=[pltpu.VMEM((t, QB), I32), pltpu.VMEM((t, QB), F32)],
        compiler_params=_cparams(("parallel", "parallel")),
        name="prompt_attention",
    )(z, z, z, zi, zi, kb, kt, ki, cb, ct, c2s)


def _layer_norm(v, g, b):
    mu = jnp.mean(v, axis=-1, keepdims=True)
    var = jnp.mean(jnp.square(v - mu), axis=-1, keepdims=True)
    return (v - mu) * lax.rsqrt(var + LN_EPS) * g + b


def _gmlp_kernel(z_ref, lng_ref, lnb_ref, ws_ref, bs_ref, o_ref):
    tc = z_ref.shape[1]
    c = ws_ref.shape[1]
    tril = lax.broadcasted_iota(I32, (c, c), 1) <= lax.broadcasted_iota(I32, (c, c), 0)
    lane_g = lax.broadcasted_iota(I32, (1, GMLP_WIDTH), 1) >> 6
    ws = [jnp.where(tril, ws_ref[g], 0.0).astype(BF16) for g in range(GMLP_GROUPS)]
    for ci in range(tc // c):
        zg = _gelu_tanh(z_ref[0, ci * c:(ci + 1) * c, :])
        u = zg[:, 0:GMLP_WIDTH]
        v = _layer_norm(zg[:, GMLP_WIDTH:2 * GMLP_WIDTH], lng_ref[...], lnb_ref[...]).astype(BF16)
        s = bs_ref[...]
        for g in range(GMLP_GROUPS):
            s = s + jnp.where(lane_g == g, _dot(ws[g], v), 0.0)
        o_ref[0, ci * c:(ci + 1) * c, :] = u * s


def _gmlp(z, ln_g, ln_b, w_s, b_s, tc):
    b, t, _ = z.shape
    c = w_s.shape[1]
    bs_exp = jnp.repeat(b_s.T, HEAD_DIM, axis=1)
    return pl.pallas_call(
        _gmlp_kernel,
        grid=(b, t // tc),
        in_specs=[pl.BlockSpec((1, tc, 2 * GMLP_WIDTH), lambda bi, i: (bi, i, C_GMLP // (2 * GMLP_WIDTH))),
                  pl.BlockSpec((1, GMLP_WIDTH), lambda bi, i: (0, 0)),
                  pl.BlockSpec((1, GMLP_WIDTH), lambda bi, i: (0, 0)),
                  pl.BlockSpec((GMLP_GROUPS, c, c), lambda bi, i: (0, 0, 0)),
                  pl.BlockSpec((c, GMLP_WIDTH), lambda bi, i: (0, 0))],
        out_specs=pl.BlockSpec((1, tc, GMLP_WIDTH), lambda bi, i: (bi, i, 0)),
        out_shape=jax.ShapeDtypeStruct((b, t, GMLP_WIDTH), F32),
        compiler_params=_cparams(("parallel", "parallel")),
        name="gmlp",
    )(z, ln_g.reshape(1, -1), ln_b.reshape(1, -1), w_s, bs_exp)


def _head_sum(x):
    lane_h = lax.broadcasted_iota(I32, (1, x.shape[-1]), 1) >> 6
    out = jnp.zeros_like(x)
    for h in range(x.shape[-1] // HEAD_DIM):
        msk = lane_h == h
        out = out + jnp.where(msk, jnp.sum(jnp.where(msk, x, 0.0), axis=-1, keepdims=True), 0.0)
    return out


def _softplus(x):
    return jnp.maximum(x, 0.0) + jnp.log1p(jnp.exp(-jnp.abs(x)))


def _rwkv_features(f, w0, w2, a0, a2, g2, kkw, ka, rk):
    wd = RWKV_WIDTH
    r, k, v = f[:, 0:wd], f[:, wd:2 * wd], f[:, 2 * wd:3 * wd]
    wl, al, gl = f[:, 3 * wd:3 * wd + 64], f[:, 3 * wd + 64:3 * wd + 128], f[:, 3 * wd + 128:3 * wd + 256]
    w_log = -_softplus(-(w0 + _dot(jnp.tanh(wl).astype(BF16), w2))) - 0.5
    log_decay = -jnp.exp(w_log)
    a = _sigmoid(a0 + _dot(al.astype(BF16), a2))
    g = _dot(_sigmoid(gl).astype(BF16), g2)
    kk = k * kkw
    kk = kk * lax.rsqrt(_head_sum(kk * kk) + 1e-12)
    k = k * (1.0 + (a - 1.0) * ka)
    bonus = _head_sum(r * k * rk) * v
    return r, log_decay, k, v, kk, kk * a, g, bonus


RWKV_CHUNK = 64


def _rwkv_pre_kernel(f_ref, prev_ref, shift_ref, mu_ref, w0_ref, w2_ref, a0_ref, a2_ref, g2_ref, kkw_ref, ka_ref, rk_ref,
                     kh_o, rh_o, kb_o, bb_o, kbe_o, bbe_o, v_o, pe_o, g_o, bonus_o, *, chunk):
    feat = f_ref[0]
    tm = feat.shape[0]
    first = jnp.where(pl.program_id(1) == 0, shift_ref[0], prev_ref[0, 7:8, :])
    prev = jnp.where(lax.broadcasted_iota(I32, (tm, 1), 0) == 0, first, pltpu.roll(feat, 1, axis=0))
    f = feat + mu_ref[...] * (prev - feat)
    r, lw, k, v, kk, be, g, bonus = _rwkv_features(f, w0_ref[...], w2_ref[...], a0_ref[...], a2_ref[...], g2_ref[...],
                                                   kkw_ref[...], ka_ref[...], rk_ref[...])
    row = lax.broadcasted_iota(I32, (tm, tm), 0)
    col = lax.broadcasted_iota(I32, (tm, tm), 1)
    tri = jnp.where((col >= (row // chunk) * chunk) & (col <= row), 1.0, 0.0).astype(BF16)
    p1 = lw.astype(BF16)
    r1 = lw - p1.astype(F32)
    p2 = r1.astype(BF16)
    p3 = (r1 - p2.astype(F32)).astype(BF16)
    cum = _dot(tri, p1) + (_dot(tri, p2) + _dot(tri, p3))
    cum_end = jnp.concatenate([jnp.broadcast_to(cum[c * chunk + chunk - 1:(c + 1) * chunk, :], (chunk, cum.shape[1]))
                               for c in range(tm // chunk)], axis=0)
    down, to_end = jnp.exp(-cum), jnp.exp(cum_end - cum)
    for o, x in ((kh_o, kk * jnp.exp(cum - lw)), (rh_o, r * jnp.exp(cum)), (kb_o, k * down), (bb_o, be * down),
                 (kbe_o, k * to_end), (bbe_o, be * to_end), (v_o, v), (pe_o, jnp.exp(cum_end)), (g_o, g), (bonus_o, bonus)):
        o[0] = x.astype(o.dtype)


def _rwkv_pre(z, shift_prev, lp, tm):
    b, t, _ = z.shape
    wd = RWKV_WIDTH
    pw = 4 * wd
    vec = lambda n: pl.BlockSpec((1, n), lambda bi, i: (0, 0))
    mat = lambda r, c: pl.BlockSpec((r, c), lambda bi, i: (0, 0))
    fl = jax.ShapeDtypeStruct((b, t, wd), F32)
    hl = jax.ShapeDtypeStruct((b, t, wd), BF16)
    fspec = pl.BlockSpec((1, tm, wd), lambda bi, i: (bi, i, 0))
    return pl.pallas_call(
        functools.partial(_rwkv_pre_kernel, chunk=min(RWKV_CHUNK, t)),
        grid=(b, t // tm),
        in_specs=[pl.BlockSpec((1, tm, pw), lambda bi, i: (bi, i, C_RWKV // pw)),
                  pl.BlockSpec((1, 8, pw), lambda bi, i: (bi, jnp.maximum(i * (tm // 8) - 1, 0), C_RWKV // pw)),
                  pl.BlockSpec((1, 1, pw), lambda bi, i: (bi, 0, 0)),
                  vec(pw), vec(wd), mat(64, wd), vec(wd), mat(64, wd), mat(128, wd), vec(wd), vec(wd), vec(wd)],
        out_specs=[fspec] * 10,
        out_shape=[hl] * 7 + [fl] * 3,
        compiler_params=_cparams(("parallel", "parallel")),
        name="rwkv_pre",
    )(z, z, shift_prev, lp['rwkv_mu'].reshape(1, pw), lp['rwkv_w0'].reshape(1, wd), lp['rwkv_w2'].astype(BF16),
      lp['rwkv_a0'].reshape(1, wd), lp['rwkv_a2'].astype(BF16), lp['rwkv_g2'].astype(BF16),
      lp['rwkv_kk'].reshape(1, wd), lp['rwkv_ka'].reshape(1, wd), lp['rwkv_rk'].reshape(1, wd))


def _hdot(a, b):
    return jnp.dot(a, b, preferred_element_type=F32, precision=lax.Precision.HIGHEST)


def _rwkv_scan_kernel(kh_ref, rh_ref, kb_ref, bb_ref, kbe_ref, bbe_ref, v_ref, pe_ref, g_ref, bonus_ref, lnw_ref, lnb_ref,
                      s0_ref, o_ref, s_ref):
    @pl.when(pl.program_id(1) == 0)
    def _():
        s_ref[...] = s0_ref[...]

    bt, c, wd = kh_ref.shape
    n = N_HEADS * c
    row = lax.broadcasted_iota(I32, (n, wd), 0)
    col = lax.broadcasted_iota(I32, (n, wd), 1)
    same_head = (row // c) == (col // HEAD_DIM)
    tpos, jpos = row % c, col % HEAD_DIM
    strict, incl = same_head & (jpos < tpos), same_head & (jpos <= tpos)
    eye = jnp.where(same_head & (jpos == tpos), 1.0, 0.0)
    bf = lambda x: x.astype(BF16)
    block_diag = lambda ref, i: jnp.where(same_head, jnp.concatenate([ref[i]] * N_HEADS, axis=0), jnp.zeros((), BF16))
    for i in range(bt):
        kh, rh, kb, bb, kbe, bbe, v = (block_diag(ref, i) for ref in (kh_ref, rh_ref, kb_ref, bb_ref, kbe_ref, bbe_ref, v_ref))
        g1 = _dot_nt(jnp.concatenate([kh, rh], axis=0), jnp.concatenate([bb, kb], axis=0))
        a_kb = jnp.where(strict, g1[0:n, 0:n], 0.0)
        a_kk = jnp.where(strict, g1[0:n, n:2 * n], 0.0)
        a_rb = jnp.where(incl, g1[n:2 * n, 0:n], 0.0)
        a_rk = jnp.where(incl, g1[n:2 * n, n:2 * n], 0.0)
        s0 = s_ref[i]
        zy = _dot(jnp.concatenate([jnp.concatenate([kh, bf(a_kk)], axis=1), jnp.concatenate([rh, bf(a_rk)], axis=1)], axis=0),
                  jnp.concatenate([bf(s0), v], axis=0))
        z, y = zy[0:n], zy[n:2 * n]
        p = bf(-a_kb)
        tinv = eye - a_kb
        p = bf(_dot(p, p))
        for _ in range(int(np.log2(c)) - 2):
            tinv, p = tinv + _dot(bf(tinv), p), bf(_dot(p, p))
        tinv = tinv + _dot(bf(tinv), p)
        u = bf(-_dot(bf(tinv), bf(z)))
        y = y + _dot(bf(a_rb), u)
        pe_rows = pe_ref[i]
        pe_col = jnp.concatenate([pe_rows] * (LANE // c), axis=0).T
        s_ref[i] = (s0 * jnp.concatenate([pe_col] * (wd // LANE), axis=1)
                    + _dot_tn(jnp.concatenate([kbe, bbe], axis=0), jnp.concatenate([v, u], axis=0)))
        y_flat = y[0:c]
        for h in range(1, N_HEADS):
            y_flat = y_flat + y[h * c:(h + 1) * c]
        mu = _head_sum(y_flat) * (1.0 / HEAD_DIM)
        yc = y_flat - mu
        yn = yc * lax.rsqrt(_head_sum(yc * yc) * (1.0 / HEAD_DIM) + RWKV_GN_EPS)
        o_ref[i] = (yn * lnw_ref[...] + lnb_ref[...] + bonus_ref[i]) * g_ref[i]


def _rwkv_scan(kh, rh, kb, bb, kbe, bbe, v, pe, g, bonus, ln_w, ln_b, s0):
    b, t, wd = kh.shape
    c = min(RWKV_CHUNK, t)
    assert c == HEAD_DIM
    bt = _pick(b, (4, 2, 1))
    fspec = pl.BlockSpec((bt, c, wd), lambda bi, i: (bi, i, 0))
    sspec = pl.BlockSpec((bt, wd, wd), lambda bi, i: (bi, 0, 0))
    vec = pl.BlockSpec((1, wd), lambda bi, i: (0, 0))
    return pl.pallas_call(
        _rwkv_scan_kernel,
        grid=(b // bt, t // c),
        in_specs=[fspec] * 10 + [vec, vec, sspec],
        out_specs=[fspec, sspec],
        out_shape=[jax.ShapeDtypeStruct((b, t, wd), F32), jax.ShapeDtypeStruct((b, wd, wd), F32)],
        compiler_params=_cparams(("parallel", "arbitrary")),
        name="rwkv_scan",
    )(kh, rh, kb, bb, kbe, bbe, v, pe, g, bonus, ln_w.reshape(1, wd), ln_b.reshape(1, wd), s0)


SLOPES = tuple(2.0 ** (-8.0 * (h + 1) / N_HEADS) for h in range(N_HEADS))
SEQ_PER_STEP = 2


def _head_rows(tiles):
    rows8 = lax.broadcasted_iota(I32, (8, LANE), 0)
    out = jnp.zeros((8, LANE), F32)
    for h in range(N_HEADS):
        out = jnp.where(rows8 == h, tiles[:, h * LANE:(h + 1) * LANE], out)
    return out * (HEAD_DIM ** -0.5)


def _per_head(vals):
    head_row = lax.broadcasted_iota(I32, (8, 1), 0)
    out = jnp.zeros((8, 1), F32)
    for h in range(N_HEADS):
        out = jnp.where(head_row == h, vals[h], out)
    return out


def _sidx_kernel(pt_ref, qi_ref, kn_ref, w_ref, *refs, n_pages, page, group):
    o_ref = refs[group * n_pages]
    past = n_pages * page
    lane = lax.broadcasted_iota(I32, (1, LANE), 1)
    for g in range(group):
        pages = refs[g * n_pages:(g + 1) * n_pages]
        q = _head_rows(qi_ref[g])
        w = _per_head([w_ref[g, :, h:h + 1] * (N_HEADS ** -0.5) for h in range(N_HEADS)])
        q_hi, q_lo = _split_bf16(q[:, 0:HEAD_DIM])
        k_hi, k_lo = _split_bf16(jnp.concatenate([r[0, 0, 0] for r in pages], axis=1))
        logits = _dot(q_hi, k_hi) + (_dot(q_hi, k_lo) + _dot(q_lo, k_hi))
        o_ref[g, :, 0:past] = jnp.sum(jnp.maximum(logits, 0.0) * w, axis=0, keepdims=True)
        self_logit = jnp.sum(q * kn_ref[g], axis=1, keepdims=True)
        idx_self = jnp.sum(jnp.maximum(self_logit, 0.0) * w, axis=0, keepdims=True)
        o_ref[g, :, past:past + LANE] = jnp.where(lane == 0, idx_self, NEG)


def _sample_index(page_table, zi, cache_d, layer):
    b, n_pages = page_table.shape
    page = cache_d.shape[-1]
    width = n_pages * page + LANE
    grp = _pick(b, (SEQ_PER_STEP, 1))
    kern = functools.partial(_sidx_kernel, n_pages=n_pages, page=page, group=grp)
    page_spec = lambda g, p: pl.BlockSpec((1, 1, 1, HEAD_DIM, page), lambda bi, pt: (layer, pt[grp * bi + g, p], 2, 0, 0))
    zblk = lambda width_, col: pl.BlockSpec((grp, 1, width_), lambda bi, pt: (bi, 0, col // width_))
    gs = pltpu.PrefetchScalarGridSpec(
        num_scalar_prefetch=1, grid=(b // grp,),
        in_specs=[zblk(N_HEADS * LANE, ZI_Q), zblk(LANE, ZI_K), zblk(LANE, ZI_W)]
                 + [page_spec(g, p) for g in range(grp) for p in range(n_pages)],
        out_specs=pl.BlockSpec((grp, 1, width), lambda bi, pt: (bi, 0, 0)))
    return pl.pallas_call(kern, grid_spec=gs, out_shape=jax.ShapeDtypeStruct((b, 1, width), F32),
                          compiler_params=_cparams(("parallel",)), name="sample_index",
                          )(page_table, zi, zi, zi, *([cache_d] * (grp * n_pages)))


def _stopk_kernel(idx_ref, o_ref, *, topk):
    key = _sortable_key(idx_ref[...])
    rows, width = key.shape

    def bit_body(i, ans):
        cand = ans + lax.shift_left(jnp.int32(1), 31 - i)
        cnt = jnp.sum(jnp.where(key >= cand, 1, 0), axis=1, keepdims=True)
        return jnp.where(cnt >= topk, cand, ans)

    thr = lax.fori_loop(0, 32, bit_body, jnp.full((rows, 1), INT_MIN, I32))
    need = (topk - jnp.sum(jnp.where(key > thr, 1, 0), axis=1, keepdims=True)).astype(F32)
    tri = jnp.where(lax.broadcasted_iota(I32, (LANE, LANE), 0) < lax.broadcasted_iota(I32, (LANE, LANE), 1), 1.0, 0.0).astype(BF16)
    running = jnp.zeros((rows, 1), F32)
    for c in range(width // LANE):
        k = key[:, c * LANE:(c + 1) * LANE]
        eq = k == thr
        eqf = jnp.where(eq, 1.0, 0.0)
        before = _dot(eqf.astype(BF16), tri) + running
        sel = ((k > thr) | (eq & (before < need))) & (k > _KEY_HALF_NEG)
        o_ref[:, c * LANE:(c + 1) * LANE] = jnp.where(sel, 1.0, 0.0)
        running = running + jnp.sum(eqf, axis=1, keepdims=True)


def _sample_topk(idx, topk):
    return pl.pallas_call(functools.partial(_stopk_kernel, topk=topk),
                          out_shape=jax.ShapeDtypeStruct(idx.shape, F32),
                          compiler_params=pltpu.CompilerParams(vmem_limit_bytes=VMEM_LIMIT), name="sample_topk")(idx)


def _sattn_kernel(pt_ref, qn_ref, qd_ref, misc_ref, kslc_ref, kwin_ref, kdsa_ref, dsel_ref, win_ref, w1a_ref, w1b_ref,
                  posa_ref, posb_ref, w2_ref, c2s_ref, exp_ref, *refs, n_pages, page, group):
    o_ref, win_o_ref, xt_scr = refs[2 * group * n_pages:2 * group * n_pages + 3]
    for g in range(group):
        one = pl.ds(g, 1)
        _sattn_one(qn_ref.at[one], qd_ref.at[one], misc_ref.at[one], kslc_ref.at[one], kwin_ref.at[one], kdsa_ref.at[one],
                   dsel_ref.at[one], win_ref.at[:, one], w1a_ref, w1b_ref, posa_ref, posb_ref, w2_ref, c2s_ref, exp_ref,
                   refs[g * n_pages:(g + 1) * n_pages], refs[(group + g) * n_pages:(group + g + 1) * n_pages],
                   o_ref.at[one], win_o_ref.at[one], xt_scr.at[g], n_pages, page)


def _sattn_one(qn_ref, qd_ref, misc_ref, kslc_ref, kwin_ref, kdsa_ref, dsel_ref, win_ref, w1a_ref, w1b_ref, posa_ref,
               posb_ref, w2_ref, c2s_ref, exp_ref, nsa_pages, dsa_pages, o_ref, win_o_ref, xt_scr, n_pages, page):
    past = n_pages * page
    q_pos = past
    live = lax.broadcasted_iota(I32, (8, 1), 0) < N_HEADS
    qn, qd = _head_rows(qn_ref[0]), _head_rows(qd_ref[0])
    slope = _per_head(SLOPES)
    gates = _sigmoid(misc_ref[0, :, 0:3 * N_HEADS])
    gate = [_per_head([gates[:, 3 * h + c:3 * h + c + 1] for h in range(N_HEADS)]) for c in range(3)]

    def attend(q, k_t, v_t, mask, dist, new_pair, self_mask):
        sm = jnp.where(mask, _dot(q[:, 0:HEAD_DIM].astype(BF16), k_t) - slope * dist.astype(F32), NEG)
        s_self = jnp.where(self_mask, jnp.sum(q * new_pair, axis=1, keepdims=True), NEG)
        mx = jnp.maximum(s_self, jnp.max(sm, axis=1, keepdims=True))
        e = jnp.where(mask, jnp.exp(sm - mx), 0.0)
        e_self = jnp.where(self_mask, jnp.exp(s_self - mx), 0.0)
        l = e_self + jnp.sum(e, axis=1, keepdims=True)
        o = _dot_nt(e.astype(BF16), v_t) + e_self * new_pair[:, HEAD_DIM:2 * HEAD_DIM]
        return o / jnp.where(l > 0.0, l, 1.0)

    for p in range(n_pages):
        xt_scr[p * page:(p + 1) * page, :] = nsa_pages[p][0, 0, 0:2].reshape(2 * HEAD_DIM, page).T
    n16 = past // NSA_CMP_STRIDE
    x_all = jnp.concatenate([xt_scr[pl.ds(r, n16, stride=NSA_CMP_STRIDE), :] for r in range(NSA_CMP_STRIDE)], axis=1)
    a = _dot((x_all + posa_ref[...]).astype(BF16), w1a_ref[...])
    b = _dot((x_all + posb_ref[...]).astype(BF16), w1b_ref[...])
    hid = _gelu_tanh(a + pltpu.roll(b, n16 - 1, axis=0))
    comp = _dot(hid.astype(BF16), w2_ref[...]).astype(BF16)

    blk = lax.broadcasted_iota(I32, (1, n16), 1)
    d_cmp = q_pos - (blk * NSA_CMP_STRIDE + NSA_CMP_LEN - 1)
    m_cmp = (d_cmp >= 0) & live
    sm = jnp.where(m_cmp, _dot_nt(qn.astype(BF16), comp) - slope * d_cmp.astype(F32), NEG)
    e = jnp.where(m_cmp, jnp.exp(sm - jnp.max(sm, axis=1, keepdims=True)), 0.0)
    l = jnp.sum(e, axis=1, keepdims=True)
    pr = e / jnp.where(l > 0.0, l, 1.0)
    o_cmp = _dot(pr.astype(BF16), comp)[:, HEAD_DIM:2 * HEAD_DIM]
    psum = jnp.sum(pr, axis=0, keepdims=True)

    n_slc = past // NSA_SLC_LEN + 1
    imp = _hdot(jnp.broadcast_to(psum, (8, n16)), c2s_ref[...])[0:1]
    j = lax.broadcasted_iota(I32, (1, LANE), 1)
    cur = q_pos // NSA_SLC_LEN
    forced = (j == 0) | (j == cur) | (j == cur - 1)
    score = jnp.where((j * NSA_SLC_LEN <= q_pos) & (j < n_slc), imp + jnp.where(forced, FORCE_BONUS, 0.0), NEG)
    srow = jnp.broadcast_to(score, (LANE, LANE))
    scol = srow.T
    jp = lax.broadcasted_iota(I32, (LANE, LANE), 0)
    jj = lax.broadcasted_iota(I32, (LANE, LANE), 1)
    rank = jnp.sum(jnp.where((scol > srow) | ((scol == srow) & (jp < jj)), 1, 0), axis=0, keepdims=True)
    sel = jnp.where((rank < min(NSA_TOPN, n_slc)) & (score > 0.5 * NEG), 1.0, 0.0)
    sel_pos = _dot(jnp.broadcast_to(sel, (8, LANE)).astype(BF16), exp_ref[...])[0:1]

    cat = lambda refs, comp_i: jnp.concatenate([r[0, 0, comp_i] for r in refs], axis=1).astype(BF16)
    dist = q_pos - lax.broadcasted_iota(I32, (1, past), 1)
    m_slc = (sel_pos[:, 0:past] > 0.5) & live
    m_dsa = (dsel_ref[0, :, 0:past] > 0.5) & live
    self_slc = (sel_pos[:, past:past + 1] > 0.5) & live
    self_dsa = (dsel_ref[0, :, past:past + 1] > 0.5) & live
    wb = win_ref.shape[-1]
    d_win = q_pos - (past - wb + lax.broadcasted_iota(I32, (1, wb), 1))
    m_win = (d_win >= 0) & (d_win <= NSA_WINDOW) & (d_win <= q_pos) & live
    o_slc = attend(qn, cat(nsa_pages, 2), cat(nsa_pages, 3), m_slc, dist, kslc_ref[0], self_slc)
    o_win = attend(qn, win_ref[0, 0, 0].astype(BF16), win_ref[0, 0, 1].astype(BF16), m_win, d_win, kwin_ref[0], live)
    o_dsa = attend(qd, cat(dsa_pages, 0), cat(dsa_pages, 1), m_dsa, dist, kdsa_ref[0], self_dsa)
    o_nsa = gate[0] * o_cmp + gate[1] * o_slc + gate[2] * o_win
    o_ref[0] = jnp.concatenate([o_nsa, o_dsa], axis=1)

    new_cols = jnp.broadcast_to(kwin_ref[0], (LANE, LANE)).T
    lane_w = lax.broadcasted_iota(I32, (HEAD_DIM, wb), 1)
    for c in range(2):
        win_o_ref[0, c] = jnp.where(lane_w == wb - 1, new_cols[c * HEAD_DIM:(c + 1) * HEAD_DIM, 0:1],
                                    pltpu.roll(win_ref[0, 0, c], wb - 1, axis=1))


def _sample_attention(page_table, z, dsel, win_t, cmp_w, cache_n, cache_d, layer):
    b, n_pages = page_table.shape
    page = cache_n.shape[-1]
    past = n_pages * page
    wb = win_t.shape[-1]
    n16 = past // NSA_CMP_STRIDE
    w1a, w1b, pos_a, pos_b, w2 = cmp_w
    n_slc = past // NSA_SLC_LEN + 1
    c2s = np.zeros((n16, LANE), np.float32)
    c2s[:, :n_slc] = _cmp_to_slc_t(n16, n_slc).T
    expand = (np.arange(past + LANE)[None, :] // NSA_SLC_LEN == np.arange(LANE)[:, None]) & (np.arange(past + LANE)[None, :] <= past)
    grp = _pick(b, (SEQ_PER_STEP, 1))
    kern = functools.partial(_sattn_kernel, n_pages=n_pages, page=page, group=grp)
    full = lambda shape: pl.BlockSpec(shape, lambda bi, pt: (0,) * len(shape))
    nspec = lambda g, p: pl.BlockSpec((1, 1, 4, HEAD_DIM, page), lambda bi, pt: (layer, pt[grp * bi + g, p], 0, 0, 0))
    dspec = lambda g, p: pl.BlockSpec((1, 1, 2, HEAD_DIM, page), lambda bi, pt: (layer, pt[grp * bi + g, p], 0, 0, 0))
    pages = [(g, p) for g in range(grp) for p in range(n_pages)]
    qw = N_HEADS * LANE
    zblk = lambda width, col: pl.BlockSpec((grp, 1, width), lambda bi, pt: (bi, 0, col // width))
    gs = pltpu.PrefetchScalarGridSpec(
        num_scalar_prefetch=1, grid=(b // grp,),
        in_specs=[zblk(qw, C_QN), zblk(qw, C_QD), zblk(LANE, C_MISC), zblk(LANE, C_K + P_SLC * LANE),
                  zblk(LANE, C_K + P_WIN * LANE), zblk(LANE, C_K + P_DSA * LANE),
                  pl.BlockSpec((grp, 1, past + LANE), lambda bi, pt: (bi, 0, 0)),
                  pl.BlockSpec((1, grp, 2, HEAD_DIM, wb), lambda bi, pt: (layer, bi, 0, 0, 0)),
                  full(w1a.shape), full(w1b.shape), full(pos_a.shape), full(pos_b.shape), full(w2.shape),
                  full(c2s.shape), full(expand.shape)]
                 + [nspec(g, p) for g, p in pages] + [dspec(g, p) for g, p in pages],
        out_specs=[pl.BlockSpec((grp, 8, LANE), lambda bi, pt: (bi, 0, 0)),
                   pl.BlockSpec((grp, 2, HEAD_DIM, wb), lambda bi, pt: (bi, 0, 0, 0))],
        scratch_shapes=[pltpu.VMEM((grp, past, 2 * HEAD_DIM), F32)])
    return pl.pallas_call(
        kern, grid_spec=gs,
        out_shape=[jax.ShapeDtypeStruct((b, 8, LANE), F32), jax.ShapeDtypeStruct((b, 2, HEAD_DIM, wb), F32)],
        compiler_params=_cparams(("parallel",)), name="sample_attention",
    )(page_table, z, z, z, z, z, z, dsel, win_t, w1a, w1b, pos_a, pos_b, w2, jnp.asarray(c2s), jnp.asarray(expand, BF16),
      *([cache_n] * (grp * n_pages)), *([cache_d] * (grp * n_pages)))


def _smix_kernel(zg_ref, zr_ref, shift_ref, s_ref, lng_ref, lnb_ref, gw_ref, gb_ref, mu_ref, w0_ref, w2_ref, a0_ref,
                 a2_ref, g2_ref, kkw_ref, ka_ref, rk_ref, lnw_ref, lnb2_ref, oc_ref, vc_ref, or_ref, so_ref):
    bt = zg_ref.shape[0]
    zg = _gelu_tanh(zg_ref[...])
    v = _layer_norm(zg[:, GMLP_WIDTH:2 * GMLP_WIDTH], lng_ref[...], lnb_ref[...])
    vc_ref[...] = v
    oc_ref[...] = zg[:, 0:GMLP_WIDTH] * (v * gw_ref[...] + gb_ref[...])

    feat = zr_ref[...]
    f = feat + mu_ref[...] * (shift_ref[...] - feat)
    r, lw, k, vv, kk, be, g, bonus = _rwkv_features(f, w0_ref[...], w2_ref[...], a0_ref[...], a2_ref[...], g2_ref[...],
                                                    kkw_ref[...], ka_ref[...], rk_ref[...])
    w = jnp.exp(lw)
    lane_t = lax.broadcasted_iota(I32, (LANE, LANE), 1)
    y_cols = [jnp.zeros((LANE, LANE), F32) for _ in range(N_HEADS // 2)]
    for i in range(bt):
        for hp in range(N_HEADS // 2):
            v_colb = jnp.broadcast_to(vv[i:i + 1, hp * LANE:(hp + 1) * LANE], (LANE, LANE)).T
            ys = []
            for hh in range(2):
                h = 2 * hp + hh
                sl = slice(h * HEAD_DIM, (h + 1) * HEAD_DIM)
                s0 = s_ref[i, h]
                sa = -jnp.sum(s0 * kk[i:i + 1, sl], axis=1, keepdims=True)
                s1 = (s0 * w[i:i + 1, sl] + sa * be[i:i + 1, sl]
                      + v_colb[hh * HEAD_DIM:(hh + 1) * HEAD_DIM, 0:HEAD_DIM] * k[i:i + 1, sl])
                so_ref[i, h] = s1
                ys.append(jnp.sum(s1 * r[i:i + 1, sl], axis=1, keepdims=True))
            y_cols[hp] = jnp.where(lane_t == i, jnp.concatenate(ys, axis=0), y_cols[hp])
    y = jnp.concatenate([yc.T[0:bt] for yc in y_cols], axis=1)
    mu = _head_sum(y) * (1.0 / HEAD_DIM)
    yc = y - mu
    yn = yc * lax.rsqrt(_head_sum(yc * yc) * (1.0 / HEAD_DIM) + RWKV_GN_EPS)
    or_ref[...] = (yn * lnw_ref[...] + lnb2_ref[...] + bonus) * g


def _sample_mixers(z, shift_prev, wkv_prev, lp, bt):
    b = z.shape[0]
    wd = RWKV_WIDTH
    pw = 4 * wd
    gw = jnp.repeat(lp['gmlp_ws'][:, 0, 0], HEAD_DIM).reshape(1, GMLP_WIDTH)
    gb = jnp.repeat(lp['gmlp_bs'][:, 0], HEAD_DIM).reshape(1, GMLP_WIDTH)
    vec = lambda n: pl.BlockSpec((1, n), lambda i: (0, 0))
    mat = lambda r, c: pl.BlockSpec((r, c), lambda i: (0, 0))
    row = lambda n: pl.BlockSpec((bt, n), lambda i: (i, 0))
    sspec = pl.BlockSpec((bt, N_HEADS, HEAD_DIM, HEAD_DIM), lambda i: (i, 0, 0, 0))
    fl = jax.ShapeDtypeStruct((b, wd), F32)
    return pl.pallas_call(
        _smix_kernel,
        grid=(b // bt,),
        in_specs=[pl.BlockSpec((bt, 2 * GMLP_WIDTH), lambda i: (i, C_GMLP // (2 * GMLP_WIDTH))),
                  pl.BlockSpec((bt, pw), lambda i: (i, C_RWKV // pw)),
                  row(pw), sspec, vec(wd), vec(wd), vec(wd), vec(wd),
                  vec(pw), vec(wd), mat(64, wd), vec(wd), mat(64, wd), mat(128, wd), vec(wd), vec(wd), vec(wd), vec(wd), vec(wd)],
        out_specs=[row(wd), row(wd), row(wd), sspec],
        out_shape=[fl, fl, fl, jax.ShapeDtypeStruct(wkv_prev.shape, F32)],
        compiler_params=_cparams(("parallel",)),
        name="sample_mixers",
    )(z, z, shift_prev, wkv_prev, lp['gmlp_ln_g'].reshape(1, wd), lp['gmlp_ln_b'].reshape(1, wd), gw, gb,
      lp['rwkv_mu'].reshape(1, pw), lp['rwkv_w0'].reshape(1, wd), lp['rwkv_w2'].astype(BF16),
      lp['rwkv_a0'].reshape(1, wd), lp['rwkv_a2'].astype(BF16), lp['rwkv_g2'].astype(BF16),
      lp['rwkv_kk'].reshape(1, wd), lp['rwkv_ka'].reshape(1, wd), lp['rwkv_rk'].reshape(1, wd),
      lp['rwkv_ln_w'].reshape(1, wd), lp['rwkv_ln_b'].reshape(1, wd))


_W_IN_COLS = (('q_nsa', 256), ('kv_nsa', 384), ('g_nsa', 12), ('q_dsa', 256), ('kv_dsa', 128), ('q_idx', 256),
              ('k_idx', 64), ('w_idx', 4), ('gmlp', 512), ('rwkv', 1024), ('merge', 4096))


def _proj_weights(w_in):
    d = w_in.shape[0]
    parts, off = {}, 0
    for name, width in _W_IN_COLS:
        parts[name] = w_in[:, off:off + width]
        off += width

    def pad_heads(w):
        w4 = w.reshape(d, N_HEADS, HEAD_DIM)
        return jnp.concatenate([w4, jnp.zeros_like(w4)], axis=-1).reshape(d, N_HEADS * LANE)

    zeros = lambda n: jnp.zeros((d, n), w_in.dtype)
    w = jnp.concatenate([parts['merge'], parts['rwkv'], parts['gmlp'], pad_heads(parts['q_nsa']), pad_heads(parts['q_dsa']),
                         parts['g_nsa'], zeros(LANE - 12), parts['kv_nsa'], parts['kv_dsa']], axis=1)
    w_idx = jnp.concatenate([pad_heads(parts['q_idx']), parts['k_idx'], zeros(LANE - HEAD_DIM),
                             parts['w_idx'], zeros(LANE - N_HEADS)], axis=1)
    return w.astype(BF16), _split_bf16(w_idx)


def _compress_weights_rows(w1, w2, pos):
    half = NSA_CMP_STRIDE * HEAD_DIM
    return (w1[:, :half].astype(BF16), w1[:, half:].astype(BF16),
            pos[:, :NSA_CMP_STRIDE].reshape(2, 1, half), pos[:, NSA_CMP_STRIDE:].reshape(2, 1, half), w2.astype(BF16))


def _compress_weights_pairs(w1, w2, pos):
    hid = w1.shape[-1]
    w1r = w1.reshape(2, 2, NSA_CMP_STRIDE, HEAD_DIM, hid)
    posr = pos.reshape(2, 2, NSA_CMP_STRIDE, HEAD_DIM)
    bigs, poss = [], []
    for half in range(2):
        big = jnp.zeros((NSA_CMP_STRIDE, 2, HEAD_DIM, 2, hid), w1.dtype)
        for c in range(2):
            big = big.at[:, c, :, c, :].set(w1r[c, half])
        bigs.append(big.reshape(NSA_CMP_STRIDE * 2 * HEAD_DIM, 2 * hid).astype(BF16))
        poss.append(jnp.transpose(posr[:, half], (1, 0, 2)).reshape(1, NSA_CMP_STRIDE * 2 * HEAD_DIM))
    w2b = jnp.zeros((2, hid, 2, HEAD_DIM), w2.dtype)
    for c in range(2):
        w2b = w2b.at[c, :, c, :].set(w2[c])
    return bigs[0], bigs[1], poss[0], poss[1], w2b.reshape(2 * hid, 2 * HEAD_DIM).astype(BF16)


def _pick(n, cands):
    for c in cands:
        if n % c == 0:
            return c
    return n


def kernel(x_prompt, x_sample, cache_nsa, cache_dsa, state_nsa_win, state_rwkv_shift, state_rwkv_wkv, page_table,
           c_prompt, c_sample, w_ada, b_ada, norm_mix_g, norm_ffn_g, w_in, nsa_cmp_w1, nsa_cmp_w2, nsa_cmp_pos,
           gmlp_ln_g, gmlp_ln_b, gmlp_ws, gmlp_bs, rwkv_mu, rwkv_w0, rwkv_w2, rwkv_a0, rwkv_a2, rwkv_g2, rwkv_kk,
           rwkv_ka, rwkv_rk, rwkv_ln_w, rwkv_ln_b, w_branch, w_out, w_ffn_in, w_ffn_out, final_norm_g):
    depth = w_ada.shape[0]
    bp, t, d = x_prompt.shape
    bs = x_sample.shape[0]
    assert x_sample.shape[1] == 1 and t % CK == 0 and t % (8 * NSA_CMP_STRIDE) == 0
    cache_n = jnp.transpose(cache_nsa, (0, 1, 3, 4, 2))
    cache_d = jnp.transpose(cache_dsa, (0, 1, 3, 4, 2))
    win_t = jnp.transpose(state_nsa_win, (0, 1, 3, 4, 2))
    past = page_table.shape[1] * cache_nsa.shape[2]
    c_all = jnp.concatenate([c_prompt, c_sample], axis=0)
    xp, xs = x_prompt, x_sample.reshape(1, bs, d)
    tm_p = _pick(t, (512, 256, 128))
    tm_f = _pick(t, (1024, 512, 256, 128))
    tf = _pick(w_ffn_out.shape[1], (256, 128))
    tn = _pick(N_PROJ, (2432,))
    outs = {k: [] for k in ('rows_n_p', 'rows_n_s', 'rows_d_p', 'rows_d_s', 'win_p', 'win_s', 'v_s', 'shift_p', 'shift_s',
                            'wkv_p', 'wkv_s')}
    for l in range(depth):
        lp = {'gmlp_ln_g': gmlp_ln_g[l], 'gmlp_ln_b': gmlp_ln_b[l], 'gmlp_ws': gmlp_ws[l], 'gmlp_bs': gmlp_bs[l],
              'rwkv_mu': rwkv_mu[l], 'rwkv_w0': rwkv_w0[l], 'rwkv_w2': rwkv_w2[l], 'rwkv_a0': rwkv_a0[l],
              'rwkv_a2': rwkv_a2[l], 'rwkv_g2': rwkv_g2[l], 'rwkv_kk': rwkv_kk[l], 'rwkv_ka': rwkv_ka[l],
              'rwkv_rk': rwkv_rk[l], 'rwkv_ln_w': rwkv_ln_w[l], 'rwkv_ln_b': rwkv_ln_b[l]}
        last = l == depth - 1
        mod = _ada(c_all, w_ada[l], b_ada[l]).reshape(bp + bs, 6, d)
        mod_p = [mod[:bp, i:i + 1] for i in range(6)]
        mod_s = [mod[bp:, i][None] for i in range(6)]
        w_proj, (wi_hi, wi_lo) = _proj_weights(w_in[l])
        wb, wo = w_branch[l].astype(BF16), w_out[l].astype(BF16)
        wfi, wfo = w_ffn_in[l].astype(BF16), w_ffn_out[l].astype(BF16)

        zp = _inproj(xp, norm_mix_g[l], mod_p[0], mod_p[1], w_proj, tm_p, tn)
        zi = _inproj_hp(xp, norm_mix_g[l], mod_p[0], mod_p[1], wi_hi, wi_lo, tm_p)
        outs['rows_n_p'].append(zp[..., C_K:C_K + 4 * HEAD_DIM].reshape(bp, t, 4, HEAD_DIM))
        outs['rows_d_p'].append(jnp.concatenate([zp[..., C_K + 3 * LANE:C_K + 4 * LANE], zi[..., ZI_K:ZI_K + HEAD_DIM]],
                                                axis=-1).reshape(bp, t, 3, HEAD_DIM))
        wn = min(NSA_WINDOW, t)
        outs['win_p'].append(zp[:, t - wn:, C_K + 2 * LANE:C_K + 3 * LANE].reshape(bp, wn, 2, HEAD_DIM))
        outs['shift_p'].append(zp[:, t - 1, C_RWKV:C_RWKV + 4 * RWKV_WIDTH])
        n16 = t // NSA_CMP_STRIDE
        r = jnp.stack([zp[..., C_K:C_K + HEAD_DIM].reshape(bp, n16, NSA_CMP_STRIDE * HEAD_DIM),
                       zp[..., C_K + HEAD_DIM:C_K + 2 * HEAD_DIM].reshape(bp, n16, NSA_CMP_STRIDE * HEAD_DIM)], axis=1)
        w1a, w1b, pos_a, pos_b, w2 = _compress_weights_rows(nsa_cmp_w1[l], nsa_cmp_w2[l], nsa_cmp_pos[l])
        cb, ct = _nsa_compress(r, pos_a, pos_b, w1a, w1b, w2)
        kb, kt = _kprep(zp)
        o_att = _prompt_attention(zp, zi, kb, kt, _kprep_idx(zi), cb, ct)
        o_c = _gmlp(zp, lp['gmlp_ln_g'], lp['gmlp_ln_b'], lp['gmlp_ws'], lp['gmlp_bs'], _pick(t, (512, 256, 128)))
        pre = _rwkv_pre(zp, jnp.zeros((bp, 1, 4 * RWKV_WIDTH), F32), lp, _pick(t, (256, 128)))
        o_r, wkv_bd = _rwkv_scan(*pre, lp['rwkv_ln_w'], lp['rwkv_ln_b'], jnp.zeros((bp, RWKV_WIDTH, RWKV_WIDTH), F32))
        outs['wkv_p'].append(jnp.stack([wkv_bd[:, h * HEAD_DIM:(h + 1) * HEAD_DIM, h * HEAD_DIM:(h + 1) * HEAD_DIM]
                                        for h in range(N_HEADS)], axis=1).swapaxes(-1, -2))
        xp = _merge(zp, o_att, o_c, o_r, wb, wo, xp, mod_p[2], tm_p)
        xp = _ffn(xp, norm_ffn_g[l], mod_p[3], mod_p[4], mod_p[5], wfi, wfo, final_norm_g, last, tm_f, tf)

        zs = _inproj(xs, norm_mix_g[l], mod_s[0], mod_s[1], w_proj, bs, tn)
        z2 = zs[0]
        zi2 = _inproj_hp(xs, norm_mix_g[l], mod_s[0], mod_s[1], wi_hi, wi_lo, bs)[0]
        k_idx_new = zi2[:, ZI_K:ZI_K + HEAD_DIM]
        outs['rows_n_s'].append(z2[:, C_K:C_K + 4 * HEAD_DIM].reshape(bs, 1, 4, HEAD_DIM))
        outs['rows_d_s'].append(jnp.concatenate([z2[:, C_K + 3 * LANE:C_K + 4 * LANE], k_idx_new],
                                                axis=-1).reshape(bs, 1, 3, HEAD_DIM))
        outs['shift_s'].append(z2[:, C_RWKV:C_RWKV + 4 * RWKV_WIDTH])
        idx = _sample_index(page_table, zi2.reshape(bs, 1, N_HP), cache_d, l)
        width = idx.shape[-1]
        dsel = _sample_topk(idx.reshape(bs, width), min(DSA_TOPK, (past + 1) // 4)).reshape(bs, 1, width)
        cmp_w = _compress_weights_pairs(nsa_cmp_w1[l], nsa_cmp_w2[l], nsa_cmp_pos[l])
        o_rows, win_new = _sample_attention(page_table, zs.reshape(bs, 1, N_PROJ), dsel, win_t, cmp_w, cache_n, cache_d, l)
        o_heads = jnp.concatenate([o_rows[:, 0:N_HEADS, 0:HEAD_DIM].reshape(bs, N_HEADS * HEAD_DIM),
                                   o_rows[:, 0:N_HEADS, HEAD_DIM:2 * HEAD_DIM].reshape(bs, N_HEADS * HEAD_DIM)], axis=1)
        outs['win_s'].append(jnp.transpose(win_new, (0, 3, 1, 2)))
        o_cs, v_cs, o_rs, wkv_s = _sample_mixers(z2, state_rwkv_shift[l], state_rwkv_wkv[l], lp, 8)
        outs['v_s'].append(v_cs.reshape(bs, 1, GMLP_WIDTH))
        outs['wkv_s'].append(wkv_s)
        xs = _merge(zs, o_heads[None], o_cs[None], o_rs[None], wb, wo, xs, mod_s[2], bs)
        xs = _ffn(xs, norm_ffn_g[l], mod_s[3], mod_s[4], mod_s[5], wfi, wfo, final_norm_g, last, bs, tf)

    st = lambda k: jnp.stack(outs[k])
    return (xp, xs.reshape(bs, 1, d), st('rows_n_p'), st('rows_n_s'), st('rows_d_p'), st('rows_d_s'), st('win_p'),
            st('win_s'), st('v_s'), st('shift_p'), st('shift_s'), st('wkv_p'), st('wkv_s'))
```

```python
import functools

import numpy as np
import jax
import jax.numpy as jnp
from jax import lax
from jax.experimental import pallas as pl
from jax.experimental.pallas import tpu as pltpu

F32 = jnp.float32
BF16 = jnp.bfloat16
I32 = jnp.int32

HEAD_DIM = 64
N_HEADS = 4
NSA_CMP_LEN = 32
NSA_CMP_STRIDE = 16
NSA_SLC_LEN = 64
NSA_TOPN = 8
NSA_WINDOW = 512
DSA_TOPK = 256
GMLP_GROUPS = 4
GMLP_WIDTH = GMLP_GROUPS * HEAD_DIM
CHUNK = 128
RWKV_WIDTH = N_HEADS * HEAD_DIM
RWKV_GN_EPS = 64e-5
QB = 128
EPS = 1e-6
LN_EPS = 1e-5
NEG = -1e30
FORCE_BONUS = 1e4
LANE = 128
VMEM_LIMIT = 56 * 1024 * 1024

C_MERGE, C_RWKV, C_GMLP = 0, 4096, 5120
C_QN, C_QD, C_MISC, C_K = 5632, 6144, 6656, 6784
N_PROJ = 7296
K_PAIRS = 4
P_CMP, P_SLC, P_WIN, P_DSA = range(K_PAIRS)
ZI_Q, ZI_K, ZI_W, N_HP = 0, 512, 640, 768


def _cparams(sem):
    return pltpu.CompilerParams(dimension_semantics=sem, vmem_limit_bytes=VMEM_LIMIT)


def _dot(a, b):
    return jnp.dot(a, b, preferred_element_type=F32)


def _dot_nt(a, b):
    return lax.dot_general(a, b, (((1,), (1,)), ((), ())), preferred_element_type=F32)


def _dot_tn(a, b):
    return lax.dot_general(a, b, (((0,), (0,)), ((), ())), preferred_element_type=F32)


def _gelu_tanh(x):
    return 0.5 * x * (1.0 + jnp.tanh(np.sqrt(2.0 / np.pi).astype(np.float32) * (x + 0.044715 * (x * x * x))))


def _sigmoid(x):
    return 0.5 * jnp.tanh(0.5 * x) + 0.5


def _rms_mod(x, g, scale, shift):
    ms = jnp.mean(x * x, axis=-1, keepdims=True)
    return (x * lax.rsqrt(ms + EPS) * g) * (1.0 + scale) + shift


def _ada_kernel(c_ref, w_ref, b_ref, o_ref):
    c = c_ref[...]
    s = (c * _sigmoid(c)).astype(BF16)
    o_ref[...] = _dot(s, w_ref[...].astype(BF16)) + b_ref[...]


def _ada(c, w, b):
    m, d = c.shape
    n = w.shape[1]
    tn = 1536
    return pl.pallas_call(
        _ada_kernel,
        grid=(n // tn,),
        in_specs=[pl.BlockSpec((m, d), lambda j: (0, 0)),
                  pl.BlockSpec((d, tn), lambda j: (0, j)),
                  pl.BlockSpec((1, tn), lambda j: (0, j))],
        out_specs=pl.BlockSpec((m, tn), lambda j: (0, j)),
        out_shape=jax.ShapeDtypeStruct((m, n), F32),
        compiler_params=_cparams(("parallel",)),
        name="ada",
    )(c, w, b.reshape(1, n))


def _inproj_kernel(x_ref, g_ref, sh_ref, sc_ref, w_ref, o_ref):
    h = _rms_mod(x_ref[0], g_ref[...], sc_ref[0], sh_ref[0])
    o_ref[0] = _dot(h.astype(BF16), w_ref[...])


def _inproj(x, g, shift, scale, w, tm, tn):
    b, t, d = x.shape
    n = w.shape[1]
    tmod = tm if shift.shape[1] == t else 1
    mod_map = (lambda j, bi, i: (bi, i, 0)) if shift.shape[1] == t else (lambda j, bi, i: (bi, 0, 0))
    return pl.pallas_call(
        _inproj_kernel,
        grid=(n // tn, b, t // tm),
        in_specs=[pl.BlockSpec((1, tm, d), lambda j, bi, i: (bi, i, 0)),
                  pl.BlockSpec((1, d), lambda j, bi, i: (0, 0)),
                  pl.BlockSpec((1, tmod, d), mod_map),
                  pl.BlockSpec((1, tmod, d), mod_map),
                  pl.BlockSpec((d, tn), lambda j, bi, i: (0, j))],
        out_specs=pl.BlockSpec((1, tm, tn), lambda j, bi, i: (bi, i, j)),
        out_shape=jax.ShapeDtypeStruct((b, t, n), F32),
        compiler_params=_cparams(("parallel", "parallel", "parallel")),
        name="inproj",
    )(x, g.reshape(1, d), shift, scale, w)


def _split_bf16(x):
    hi = x.astype(BF16)
    return hi, (x - hi.astype(F32)).astype(BF16)


def _inproj_hp_kernel(x_ref, g_ref, sh_ref, sc_ref, wh_ref, wl_ref, o_ref):
    h_hi, h_lo = _split_bf16(_rms_mod(x_ref[0], g_ref[...], sc_ref[0], sh_ref[0]))
    o_ref[0] = _dot(h_hi, wh_ref[...]) + (_dot(h_hi, wl_ref[...]) + _dot(h_lo, wh_ref[...]))


def _inproj_hp(x, g, shift, scale, w_hi, w_lo, tm):
    b, t, d = x.shape
    n = w_hi.shape[1]
    tmod = tm if shift.shape[1] == t else 1
    mod_map = (lambda bi, i: (bi, i, 0)) if shift.shape[1] == t else (lambda bi, i: (bi, 0, 0))
    return pl.pallas_call(
        _inproj_hp_kernel,
        grid=(b, t // tm),
        in_specs=[pl.BlockSpec((1, tm, d), lambda bi, i: (bi, i, 0)),
                  pl.BlockSpec((1, d), lambda bi, i: (0, 0)),
                  pl.BlockSpec((1, tmod, d), mod_map),
                  pl.BlockSpec((1, tmod, d), mod_map),
                  pl.BlockSpec((d, n), lambda bi, i: (0, 0)),
                  pl.BlockSpec((d, n), lambda bi, i: (0, 0))],
        out_specs=pl.BlockSpec((1, tm, n), lambda bi, i: (bi, i, 0)),
        out_shape=jax.ShapeDtypeStruct((b, t, n), F32),
        compiler_params=_cparams(("parallel", "parallel")),
        name="inproj_hp",
    )(x, g.reshape(1, d), shift, scale, w_hi, w_lo)


def _merge_kernel(zm_ref, oa_ref, oc_ref, or_ref, wb_ref, wo_ref, x_ref, gate_ref, o_ref):
    bw = wb_ref.shape[1]
    d = x_ref.shape[-1]
    outs = (oa_ref[0, :, 0:bw], oa_ref[0, :, bw:2 * bw], oc_ref[0], or_ref[0])
    mixed = None
    for n, o in enumerate(outs):
        br = _dot(o.astype(BF16), wb_ref[n])
        term = _sigmoid(zm_ref[0, :, n * d:(n + 1) * d]) * br
        mixed = term if mixed is None else mixed + term
    y = _dot(mixed.astype(BF16), wo_ref[...])
    o_ref[0] = x_ref[0] + gate_ref[0] * y


def _merge(z, o_att, o_c, o_r, w_branch, w_out, x, gate, tm):
    b, t, d = x.shape
    nb, bw, _ = w_branch.shape
    tmod = tm if gate.shape[1] == t else 1
    mod_map = (lambda bi, i: (bi, i, 0)) if gate.shape[1] == t else (lambda bi, i: (bi, 0, 0))
    return pl.pallas_call(
        _merge_kernel,
        grid=(b, t // tm),
        in_specs=[pl.BlockSpec((1, tm, nb * d), lambda bi, i: (bi, i, C_MERGE // (nb * d))),
                  pl.BlockSpec((1, tm, 2 * bw), lambda bi, i: (bi, i, 0)),
                  pl.BlockSpec((1, tm, bw), lambda bi, i: (bi, i, 0)),
                  pl.BlockSpec((1, tm, bw), lambda bi, i: (bi, i, 0)),
                  pl.BlockSpec((nb, bw, d), lambda bi, i: (0, 0, 0)),
                  pl.BlockSpec((d, d), lambda bi, i: (0, 0)),
                  pl.BlockSpec((1, tm, d), lambda bi, i: (bi, i, 0)),
                  pl.BlockSpec((1, tmod, d), mod_map)],
        out_specs=pl.BlockSpec((1, tm, d), lambda bi, i: (bi, i, 0)),
        out_shape=jax.ShapeDtypeStruct((b, t, d), F32),
        compiler_params=_cparams(("parallel", "parallel")),
        name="merge",
    )(z, o_att, o_c, o_r, w_branch, w_out, x, gate)


def _ffn_kernel(x_ref, g_ref, sh_ref, sc_ref, gate_ref, wg_ref, wu_ref, wd_ref, fg_ref, o_ref,
                h_scr, acc_scr, *, final_norm):
    k = pl.program_id(2)

    @pl.when(k == 0)
    def _():
        h_scr[...] = _rms_mod(x_ref[0], g_ref[...], sc_ref[0], sh_ref[0]).astype(BF16)
        acc_scr[...] = jnp.zeros_like(acc_scr)

    h = h_scr[...]
    gt = _dot(h, wg_ref[...])
    up = _dot(h, wu_ref[...])
    act = (gt * _sigmoid(gt)) * up
    acc_scr[...] += _dot(act.astype(BF16), wd_ref[...])

    @pl.when(k == pl.num_programs(2) - 1)
    def _():
        y = x_ref[0] + gate_ref[0] * acc_scr[...]
        if final_norm:
            ms = jnp.mean(y * y, axis=-1, keepdims=True)
            y = y * lax.rsqrt(ms + EPS) * fg_ref[...]
        o_ref[0] = y


def _ffn(x, g, shift, scale, gate, w_in, w_out, final_g, final_norm, tm, tf):
    b, t, d = x.shape
    ff = w_out.shape[0]
    nk = ff // tf
    tmod = tm if gate.shape[1] == t else 1
    mod_map = (lambda bi, i, k: (bi, i, 0)) if gate.shape[1] == t else (lambda bi, i, k: (bi, 0, 0))
    return pl.pallas_call(
        functools.partial(_ffn_kernel, final_norm=final_norm),
        grid=(b, t // tm, nk),
        in_specs=[pl.BlockSpec((1, tm, d), lambda bi, i, k: (bi, i, 0)),
                  pl.BlockSpec((1, d), lambda bi, i, k: (0, 0)),
                  pl.BlockSpec((1, tmod, d), mod_map),
                  pl.BlockSpec((1, tmod, d), mod_map),
                  pl.BlockSpec((1, tmod, d), mod_map),
                  pl.BlockSpec((d, tf), lambda bi, i, k: (0, k)),
                  pl.BlockSpec((d, tf), lambda bi, i, k: (0, nk + k)),
                  pl.BlockSpec((tf, d), lambda bi, i, k: (k, 0)),
                  pl.BlockSpec((1, d), lambda bi, i, k: (0, 0))],
        out_specs=pl.BlockSpec((1, tm, d), lambda bi, i, k: (bi, i, 0)),
        out_shape=jax.ShapeDtypeStruct((b, t, d), F32),
        scratch_shapes=[pltpu.VMEM((tm, d), BF16), pltpu.VMEM((tm, d), F32)],
        compiler_params=_cparams(("parallel", "parallel", "arbitrary")),
        name="ffn",
    )(x, g.reshape(1, d), shift, scale, gate, w_in, w_in, w_out, final_g.reshape(1, d))


def _cmp_kernel(r_ref, pa_ref, pb_ref, w1a_ref, w1b_ref, w2_ref, ob_ref, ot_ref):
    n16 = r_ref.shape[2]
    comp = []
    for z in range(2):
        r = r_ref[0, z]
        a = _dot((r + pa_ref[z]).astype(BF16), w1a_ref[z])
        b = _dot((r + pb_ref[z]).astype(BF16), w1b_ref[z])
        hid = _gelu_tanh(a + pltpu.roll(b, n16 - 1, axis=0))
        comp.append(_dot(hid.astype(BF16), w2_ref[z]))
    pair = jnp.concatenate(comp, axis=1)
    ob_ref[0] = pair.astype(BF16)
    ot_ref[0] = pair.T.astype(BF16)


def _nsa_compress(r, pos_a, pos_b, w1a, w1b, w2):
    b, _, n16, kd = r.shape
    hid = w1a.shape[-1]
    return pl.pallas_call(
        _cmp_kernel,
        grid=(b,),
        in_specs=[pl.BlockSpec((1, 2, n16, kd), lambda bi: (bi, 0, 0, 0)),
                  pl.BlockSpec((2, 1, kd), lambda bi: (0, 0, 0)),
                  pl.BlockSpec((2, 1, kd), lambda bi: (0, 0, 0)),
                  pl.BlockSpec((2, kd, hid), lambda bi: (0, 0, 0)),
                  pl.BlockSpec((2, kd, hid), lambda bi: (0, 0, 0)),
                  pl.BlockSpec((2, hid, HEAD_DIM), lambda bi: (0, 0, 0))],
        out_specs=[pl.BlockSpec((1, n16, 2 * HEAD_DIM), lambda bi: (bi, 0, 0)),
                   pl.BlockSpec((1, 2 * HEAD_DIM, n16), lambda bi: (bi, 0, 0))],
        out_shape=[jax.ShapeDtypeStruct((b, n16, 2 * HEAD_DIM), BF16),
                   jax.ShapeDtypeStruct((b, 2 * HEAD_DIM, n16), BF16)],
        compiler_params=_cparams(("parallel",)),
        name="nsa_compress",
    )(r, pos_a, pos_b, w1a, w1b, w2)


CK = 256
AUX_POS = 32


def _attn_aux(t):
    pos = np.arange(t)
    aux = np.zeros((t, LANE), np.float32)
    aux[pos, pos // NSA_SLC_LEN] = 1.0
    aux[:, AUX_POS] = pos % CK
    aux[:, AUX_POS + 1] = pos // CK
    return jnp.asarray(aux, BF16)


def _kprep_kernel(z_ref, aux_ref, ob_ref, ot_ref):
    x = z_ref[0]
    ob_ref[0, 0] = jnp.concatenate([x.astype(BF16), aux_ref[...]], axis=1)
    for c in range(x.shape[0] // CK):
        ot_ref[0, 0, c] = x[c * CK:(c + 1) * CK].T[HEAD_DIM:2 * HEAD_DIM].astype(BF16)


def _kprep(z):
    b, t, _ = z.shape
    tk = _pick(t, (1024, 512, CK))
    return pl.pallas_call(
        _kprep_kernel,
        grid=(b, K_PAIRS, t // tk),
        in_specs=[pl.BlockSpec((1, tk, LANE), lambda bi, p, c: (bi, c, C_K // LANE + p)),
                  pl.BlockSpec((tk, LANE), lambda bi, p, c: (c, 0))],
        out_specs=[pl.BlockSpec((1, 1, tk, 2 * LANE), lambda bi, p, c: (bi, p, c, 0)),
                   pl.BlockSpec((1, 1, tk // CK, HEAD_DIM, CK), lambda bi, p, c: (bi, p, c, 0, 0))],
        out_shape=[jax.ShapeDtypeStruct((b, K_PAIRS, t, 2 * LANE), BF16),
                   jax.ShapeDtypeStruct((b, K_PAIRS, t // CK, HEAD_DIM, CK), BF16)],
        compiler_params=_cparams(("parallel", "parallel", "parallel")),
        name="kprep",
    )(z, _attn_aux(t))


def _kprep_idx_kernel(z_ref, o_ref):
    x = z_ref[0]
    hi = x.astype(BF16).astype(F32)
    o_ref[0] = jnp.concatenate([hi + pltpu.roll(x - hi, HEAD_DIM, axis=1), hi], axis=1).astype(BF16)


def _kprep_idx(zi):
    b, t, _ = zi.shape
    tt = _pick(t, (512, 256, 128))
    return pl.pallas_call(
        _kprep_idx_kernel,
        grid=(b, t // tt),
        in_specs=[pl.BlockSpec((1, tt, LANE), lambda bi, i: (bi, i, ZI_K // LANE))],
        out_specs=pl.BlockSpec((1, tt, 2 * LANE), lambda bi, i: (bi, i, 0)),
        out_shape=jax.ShapeDtypeStruct((b, t, 2 * LANE), BF16),
        compiler_params=_cparams(("parallel", "parallel")),
        name="kprep_idx",
    )(zi)


INT_MIN = -2 ** 31


def _sortable_key(v):
    v = jnp.where(v == 0.0, 0.0, v)
    u = lax.bitcast_convert_type(v, I32)
    return jnp.where(u < 0, u ^ 0x7FFFFFFF, u)


_KEY_HALF_NEG = int(np.array(0.5 * NEG, np.float32).view(np.int32) ^ 0x7FFFFFFF)


def _lane_consts(q0):
    lane = lax.broadcasted_iota(I32, (1, N_HEADS * QB), 1)
    hl = lane >> 7
    q_pos = q0 + (lane & (QB - 1))
    slope = jnp.where(hl == 0, 2.0 ** -2, jnp.where(hl == 1, 2.0 ** -4, jnp.where(hl == 2, 2.0 ** -6, 2.0 ** -8)))
    return q_pos, slope.astype(F32)


def _tile_heads(x):
    return jnp.concatenate([x] * N_HEADS, axis=1)


def _place_heads(tiles):
    lane = lax.broadcasted_iota(I32, (QB, LANE), 1)
    out = []
    for t in range(N_HEADS // 2):
        out.append(jnp.where(lane < HEAD_DIM, pltpu.roll(tiles[2 * t], HEAD_DIM, axis=1), tiles[2 * t + 1]))
    return jnp.concatenate(out, axis=1)


def _attn_kernel(qn_ref, qd_ref, misc_ref, qi_ref, wi_ref, kb_ref, kt_ref, ki_ref, cb_ref, ct_ref, c2s_ref, o_ref,
                 key_scr, dsel_scr, *, seq_len, topk):
    t = seq_len
    q0 = pl.program_id(1) * QB
    nc = (q0 + QB + CK - 1) // CK
    hq = N_HEADS * QB
    scale = HEAD_DIM ** -0.5

    def stack_q(ref):
        return jnp.concatenate([ref[0, :, h * LANE:(h + 1) * LANE] for h in range(N_HEADS)], axis=0) * scale

    qn, qd = stack_q(qn_ref).astype(BF16), stack_q(qd_ref).astype(BF16)
    qi = stack_q(qi_ref)
    qi_cat = jnp.concatenate([(qi + pltpu.roll(qi, HEAD_DIM, axis=1)).astype(BF16),
                              (qi - qi.astype(BF16).astype(F32)).astype(BF16)], axis=1)
    misc_t = misc_ref[0].T
    wi_t = wi_ref[0].T
    q_pos, slope = _lane_consts(q0)

    n16 = cb_ref.shape[1]
    sc = _dot_nt(cb_ref[0], qn)
    cmp_end = lax.broadcasted_iota(I32, (n16, hq), 0) * NSA_CMP_STRIDE + (NSA_CMP_LEN - 1)
    d = q_pos - cmp_end
    mask = d >= 0
    sm = jnp.where(mask, sc - slope * d.astype(F32), NEG)
    e = jnp.where(mask, jnp.exp(sm - jnp.max(sm, axis=0, keepdims=True)), 0.0)
    l = jnp.sum(e, axis=0, keepdims=True)
    p = e / jnp.where(l > 0.0, l, 1.0)
    o_cmp_t = _dot(ct_ref[0], p.astype(BF16))
    psum_t = p[:, 0:QB]
    for h in range(1, N_HEADS):
        psum_t = psum_t + p[:, h * QB:(h + 1) * QB]
    imp_t = jnp.dot(c2s_ref[...], psum_t, preferred_element_type=F32, precision=lax.Precision.HIGHEST)

    nslc = c2s_ref.shape[0]
    jrow = lax.broadcasted_iota(I32, (nslc, QB), 0)
    qp = q0 + lax.broadcasted_iota(I32, (nslc, QB), 1)
    cur = qp >> 6
    adm = jrow * NSA_SLC_LEN <= qp
    forced = (jrow == 0) | (jrow == cur) | (jrow == cur - 1)
    score = jnp.where(adm, imp_t + jnp.where(forced, FORCE_BONUS, 0.0), NEG)
    rank = jnp.zeros((nslc, QB), I32)
    for j in range(nslc):
        row = score[j:j + 1, :]
        rank = rank + jnp.where((row > score) | ((row == score) & (jrow > j)), 1, 0)
    sel_bias = jnp.where((rank < min(NSA_TOPN, nslc)) & (score > 0.5 * NEG), 0.0, NEG)

    w_rows = [wi_t[h:h + 1, :] * (N_HEADS ** -0.5) for h in range(N_HEADS)]

    tri = jnp.where(lax.broadcasted_iota(I32, (CK, CK), 1) < lax.broadcasted_iota(I32, (CK, CK), 0), 1.0, 0.0).astype(BF16)

    def dsa_select(n_spans):
        span = 2 * CK
        for c in range(n_spans):
            lg = jnp.maximum(_dot_nt(ki_ref[0, c * span:(c + 1) * span, :], qi_cat), 0.0)
            idx = lg[:, 0:QB] * w_rows[0]
            for h in range(1, N_HEADS):
                idx = idx + lg[:, h * QB:(h + 1) * QB] * w_rows[h]
            kpos = c * span + lax.broadcasted_iota(I32, (span, QB), 0)
            causal = kpos <= q0 + lax.broadcasted_iota(I32, (span, QB), 1)
            key_scr[c * span:(c + 1) * span, :] = _sortable_key(jnp.where(causal, idx, NEG))

        def count(pred_fn):
            acc = jnp.zeros((1, QB), I32)
            for c in range(n_spans):
                acc = acc + jnp.sum(jnp.where(pred_fn(key_scr[c * span:(c + 1) * span, :]), 1, 0), axis=0, keepdims=True)
            return acc

        def bit_body(i, ans):
            cand = ans + lax.shift_left(jnp.int32(1), 31 - i)
            return jnp.where(count(lambda k: k >= cand) >= topk, cand, ans)

        thr = lax.fori_loop(0, 32, bit_body, jnp.full((1, QB), INT_MIN, I32))
        need = (topk - count(lambda k: k > thr)).astype(F32)
        running = jnp.zeros((1, QB), F32)
        for c in range(2 * n_spans):
            k = key_scr[c * CK:(c + 1) * CK, :]
            eq = k == thr
            eqf = jnp.where(eq, 1.0, 0.0)
            before = _dot(tri, eqf.astype(BF16)) + running
            sel = ((k > thr) | (eq & (before < need))) & (k > _KEY_HALF_NEG)
            dsel_scr[c * CK:(c + 1) * CK, :] = jnp.where(sel, 0.0, NEG)
            running = running + jnp.sum(eqf, axis=0, keepdims=True)
        return jnp.int32(0)

    lax.switch((nc + 1) // 2 - 1, [functools.partial(dsa_select, n) for n in range(1, t // (2 * CK) + 1)])

    sel_bias_t = jnp.concatenate([sel_bias, jnp.zeros((LANE - nslc, QB), F32)], axis=0).T
    lane = lax.broadcasted_iota(I32, (QB, LANE), 1)
    q_slc, q_dsa = [], []
    for h in range(N_HEADS):
        pos_cols = jnp.where(lane == AUX_POS, SLOPES[h], jnp.where(lane == AUX_POS + 1, SLOPES[h] * CK, 0.0))
        q_slc.append(sel_bias_t + pos_cols)
        q_dsa.append(pos_cols)
    qn_cat = jnp.concatenate([qn, jnp.concatenate(q_slc, axis=0).astype(BF16)], axis=1)
    qd_cat = jnp.concatenate([qd, jnp.concatenate(q_dsa, axis=0).astype(BF16)], axis=1)
    def online(state, s, vt):
        m_old, l_old, acc_old = state
        m_new = jnp.maximum(m_old, jnp.max(s, axis=0, keepdims=True))
        alpha = jnp.exp(m_old - m_new)
        e_ = jnp.exp(s - m_new)
        return m_new, alpha * l_old + jnp.sum(e_, axis=0, keepdims=True), alpha * acc_old + _dot(vt, e_.astype(BF16))

    def scores(c, diagonal):
        rows = pl.ds(pl.multiple_of(c * CK, CK), CK)
        s_slc = _dot_nt(kb_ref[0, P_SLC, rows, :], qn_cat)
        if diagonal:
            kpos = c * CK + lax.broadcasted_iota(I32, (CK, hq), 0)
            s_slc = jnp.where(kpos <= q_pos, s_slc, NEG)
        return s_slc, _dot_nt(kb_ref[0, P_DSA, rows, :], qd_cat) + _tile_heads(dsel_scr[rows, :])

    def flash_chunks(chunks, state, diagonal=None):
        diagonal = diagonal or (False,) * len(chunks)
        sc = [scores(c, d) for c, d in zip(chunks, diagonal)]
        st_slc, st_dsa = state
        for c, (s_slc, s_dsa) in zip(chunks, sc):
            st_slc = online(st_slc, s_slc, kt_ref[0, P_SLC, c])
            st_dsa = online(st_dsa, s_dsa, kt_ref[0, P_DSA, c])
        return st_slc, st_dsa

    init = (jnp.full((1, hq), NEG, F32), jnp.zeros((1, hq), F32), jnp.zeros((HEAD_DIM, hq), F32))
    n_full = nc - 1
    state = lax.fori_loop(0, n_full // 2, lambda i, st: flash_chunks((2 * i, 2 * i + 1), st), (init, init))
    state = lax.cond(n_full % 2 == 1,
                     lambda st: flash_chunks((nc - 2, nc - 1), st, diagonal=(False, True)),
                     lambda st: flash_chunks((nc - 1,), st, diagonal=(True,)), state)

    def finish(st):
        m_, l_, acc = st
        return acc * jnp.where(m_ > 0.5 * NEG, 1.0 / l_, 0.0)

    o_slc_t, o_dsa_t = finish(state[0]), finish(state[1])

    wk = min(NSA_WINDOW + QB, t)
    ws = pl.multiple_of(jnp.clip(q0 - NSA_WINDOW, 0, t - wk), QB)
    kw = kb_ref[0, P_WIN, pl.ds(ws, wk), :]
    dw = q_pos - (ws + lax.broadcasted_iota(I32, (wk, hq), 0))
    qw_cat = jnp.concatenate([qn, qd_cat[:, LANE:2 * LANE]], axis=1)
    sm = jnp.where((dw >= 0) & (dw <= NSA_WINDOW), _dot_nt(kw, qw_cat), NEG)
    mw = jnp.max(sm, axis=0, keepdims=True)
    e = jnp.exp(sm - mw)
    lw = jnp.sum(e, axis=0, keepdims=True)
    o_win_t = _dot_tn(kw[:, 0:LANE], e.astype(BF16)) * jnp.where(mw > 0.5 * NEG, 1.0 / lw, 0.0)

    gates = _sigmoid(misc_t[0:3 * N_HEADS, :])
    nsa_tiles, dsa_tiles = [], []
    top = jnp.zeros((HEAD_DIM, QB), F32)
    vrows = slice(HEAD_DIM, 2 * HEAD_DIM)
    for h in range(N_HEADS):
        cols = slice(h * QB, (h + 1) * QB)
        on = (gates[3 * h:3 * h + 1, :] * o_cmp_t[vrows, cols] + gates[3 * h + 1:3 * h + 2, :] * o_slc_t[:, cols]
              + gates[3 * h + 2:3 * h + 3, :] * o_win_t[vrows, cols])
        nsa_tiles.append(jnp.concatenate([top, on], axis=0).T)
        dsa_tiles.append(jnp.concatenate([top, o_dsa_t[:, cols]], axis=0).T)
    o_ref[0, :, 0:N_HEADS * HEAD_DIM] = _place_heads(nsa_tiles)
    o_ref[0, :, N_HEADS * HEAD_DIM:2 * N_HEADS * HEAD_DIM] = _place_heads(dsa_tiles)


def _cmp_to_slc_t(n16, n_slc):
    start = np.arange(n16) * NSA_CMP_STRIDE
    bstart = np.arange(n_slc) * NSA_SLC_LEN
    ov = np.minimum(start[:, None] + NSA_CMP_LEN, bstart[None, :] + NSA_SLC_LEN) - np.maximum(start[:, None], bstart[None, :])
    return (np.clip(ov, 0, None) / NSA_CMP_LEN).T.astype(np.float32)


def _prompt_attention(z, zi, kb, kt, ki, cb, ct):
    b, t, _ = z.shape
    n16 = cb.shape[1]
    n_slc = -(-t // NSA_SLC_LEN)
    assert n_slc <= AUX_POS and t // CK <= 256 and t % (2 * CK) == 0
    hq = N_HEADS * QB
    qw = N_HEADS * LANE
    topk = min(DSA_TOPK, t // 4)
    c2s = jnp.asarray(_cmp_to_slc_t(n16, n_slc))
    qspec = lambda col: pl.BlockSpec((1, QB, qw), lambda bi, i: (bi, i, col // qw))
    tile = lambda col: pl.BlockSpec((1, QB, LANE), lambda bi, i: (bi, i, col // LANE))
    return pl.pallas_call(
        functools.partial(_attn_kernel, seq_len=t, topk=topk),
        grid=(b, t // QB),
        in_specs=[qspec(C_QN), qspec(C_QD), tile(C_MISC), qspec(ZI_Q), tile(ZI_W),
                  pl.BlockSpec((1, K_PAIRS, t, 2 * LANE), lambda bi, i: (bi, 0, 0, 0)),
                  pl.BlockSpec((1, K_PAIRS, t // CK, HEAD_DIM, CK), lambda bi, i: (bi, 0, 0, 0, 0)),
                  pl.BlockSpec((1, t, 2 * LANE), lambda bi, i: (bi, 0, 0)),
                  pl.BlockSpec((1, n16, LANE), lambda bi, i: (bi, 0, 0)),
                  pl.BlockSpec((1, LANE, n16), lambda bi, i: (bi, 0, 0)),
                  pl.BlockSpec((n_slc, n16), lambda bi, i: (0, 0))],
        out_specs=pl.BlockSpec((1, QB, 2 * N_HEADS * HEAD_DIM), lambda bi, i: (bi, i, 0)),
        out_shape=jax.ShapeDtypeStruct((b, t, 2 * N_HEADS * HEAD_DIM), F32),
        scratch_shapes=[pltpu.VMEM((t, QB), I32), pltpu.VMEM((t, QB), F32)],
        compiler_params=_cparams(("parallel", "parallel")),
        name="prompt_attention",
    )(z, z, z, zi, zi, kb, kt, ki, cb, ct, c2s)


def _layer_norm(v, g, b):
    mu = jnp.mean(v, axis=-1, keepdims=True)
    var = jnp.mean(jnp.square(v - mu), axis=-1, keepdims=True)
    return (v - mu) * lax.rsqrt(var + LN_EPS) * g + b


def _gmlp_kernel(z_ref, lng_ref, lnb_ref, ws_ref, bs_ref, o_ref):
    tc = z_ref.shape[1]
    c = ws_ref.shape[1]
    tril = lax.broadcasted_iota(I32, (c, c), 1) <= lax.broadcasted_iota(I32, (c, c), 0)
    lane_g = lax.broadcasted_iota(I32, (1, GMLP_WIDTH), 1) >> 6
    ws = [jnp.where(tril, ws_ref[g], 0.0).astype(BF16) for g in range(GMLP_GROUPS)]
    for ci in range(tc // c):
        zg = _gelu_tanh(z_ref[0, ci * c:(ci + 1) * c, :])
        u = zg[:, 0:GMLP_WIDTH]
        v = _layer_norm(zg[:, GMLP_WIDTH:2 * GMLP_WIDTH], lng_ref[...], lnb_ref[...]).astype(BF16)
        s = bs_ref[...]
        for g in range(GMLP_GROUPS):
            s = s + jnp.where(lane_g == g, _dot(ws[g], v), 0.0)
        o_ref[0, ci * c:(ci + 1) * c, :] = u * s


def _gmlp(z, ln_g, ln_b, w_s, b_s, tc):
    b, t, _ = z.shape
    c = w_s.shape[1]
    bs_exp = jnp.repeat(b_s.T, HEAD_DIM, axis=1)
    return pl.pallas_call(
        _gmlp_kernel,
        grid=(b, t // tc),
        in_specs=[pl.BlockSpec((1, tc, 2 * GMLP_WIDTH), lambda bi, i: (bi, i, C_GMLP // (2 * GMLP_WIDTH))),
                  pl.BlockSpec((1, GMLP_WIDTH), lambda bi, i: (0, 0)),
                  pl.BlockSpec((1, GMLP_WIDTH), lambda bi, i: (0, 0)),
                  pl.BlockSpec((GMLP_GROUPS, c, c), lambda bi, i: (0, 0, 0)),
                  pl.BlockSpec((c, GMLP_WIDTH), lambda bi, i: (0, 0))],
        out_specs=pl.BlockSpec((1, tc, GMLP_WIDTH), lambda bi, i: (bi, i, 0)),
        out_shape=jax.ShapeDtypeStruct((b, t, GMLP_WIDTH), F32),
        compiler_params=_cparams(("parallel", "parallel")),
        name="gmlp",
    )(z, ln_g.reshape(1, -1), ln_b.reshape(1, -1), w_s, bs_exp)


def _head_sum(x):
    lane_h = lax.broadcasted_iota(I32, (1, x.shape[-1]), 1) >> 6
    out = jnp.zeros_like(x)
    for h in range(x.shape[-1] // HEAD_DIM):
        msk = lane_h == h
        out = out + jnp.where(msk, jnp.sum(jnp.where(msk, x, 0.0), axis=-1, keepdims=True), 0.0)
    return out


def _softplus(x):
    return jnp.maximum(x, 0.0) + jnp.log1p(jnp.exp(-jnp.abs(x)))


def _rwkv_features(f, w0, w2, a0, a2, g2, kkw, ka, rk):
    wd = RWKV_WIDTH
    r, k, v = f[:, 0:wd], f[:, wd:2 * wd], f[:, 2 * wd:3 * wd]
    wl, al, gl = f[:, 3 * wd:3 * wd + 64], f[:, 3 * wd + 64:3 * wd + 128], f[:, 3 * wd + 128:3 * wd + 256]
    w_log = -_softplus(-(w0 + _dot(jnp.tanh(wl).astype(BF16), w2))) - 0.5
    log_decay = -jnp.exp(w_log)
    a = _sigmoid(a0 + _dot(al.astype(BF16), a2))
    g = _dot(_sigmoid(gl).astype(BF16), g2)
    kk = k * kkw
    kk = kk * lax.rsqrt(_head_sum(kk * kk) + 1e-12)
    k = k * (1.0 + (a - 1.0) * ka)
    bonus = _head_sum(r * k * rk) * v
    return r, log_decay, k, v, kk, kk * a, g, bonus


RWKV_CHUNK = 64


def _rwkv_pre_kernel(f_ref, prev_ref, shift_ref, mu_ref, w0_ref, w2_ref, a0_ref, a2_ref, g2_ref, kkw_ref, ka_ref, rk_ref,
                     kh_o, rh_o, kb_o, bb_o, kbe_o, bbe_o, v_o, pe_o, g_o, bonus_o, *, chunk):
    feat = f_ref[0]
    tm = feat.shape[0]
    first = jnp.where(pl.program_id(1) == 0, shift_ref[0], prev_ref[0, 7:8, :])
    prev = jnp.where(lax.broadcasted_iota(I32, (tm, 1), 0) == 0, first, pltpu.roll(feat, 1, axis=0))
    f = feat + mu_ref[...] * (prev - feat)
    r, lw, k, v, kk, be, g, bonus = _rwkv_features(f, w0_ref[...], w2_ref[...], a0_ref[...], a2_ref[...], g2_ref[...],
                                                   kkw_ref[...], ka_ref[...], rk_ref[...])
    row = lax.broadcasted_iota(I32, (tm, tm), 0)
    col = lax.broadcasted_iota(I32, (tm, tm), 1)
    tri = jnp.where((col >= (row // chunk) * chunk) & (col <= row), 1.0, 0.0).astype(BF16)
    p1 = lw.astype(BF16)
    r1 = lw - p1.astype(F32)
    p2 = r1.astype(BF16)
    p3 = (r1 - p2.astype(F32)).astype(BF16)
    cum = _dot(tri, p1) + (_dot(tri, p2) + _dot(tri, p3))
    cum_end = jnp.concatenate([jnp.broadcast_to(cum[c * chunk + chunk - 1:(c + 1) * chunk, :], (chunk, cum.shape[1]))
                               for c in range(tm // chunk)], axis=0)
    down, to_end = jnp.exp(-cum), jnp.exp(cum_end - cum)
    for o, x in ((kh_o, kk * jnp.exp(cum - lw)), (rh_o, r * jnp.exp(cum)), (kb_o, k * down), (bb_o, be * down),
                 (kbe_o, k * to_end), (bbe_o, be * to_end), (v_o, v), (pe_o, jnp.exp(cum_end)), (g_o, g), (bonus_o, bonus)):
        o[0] = x.astype(o.dtype)


def _rwkv_pre(z, shift_prev, lp, tm):
    b, t, _ = z.shape
    wd = RWKV_WIDTH
    pw = 4 * wd
    vec = lambda n: pl.BlockSpec((1, n), lambda bi, i: (0, 0))
    mat = lambda r, c: pl.BlockSpec((r, c), lambda bi, i: (0, 0))
    fl = jax.ShapeDtypeStruct((b, t, wd), F32)
    hl = jax.ShapeDtypeStruct((b, t, wd), BF16)
    fspec = pl.BlockSpec((1, tm, wd), lambda bi, i: (bi, i, 0))
    return pl.pallas_call(
        functools.partial(_rwkv_pre_kernel, chunk=min(RWKV_CHUNK, t)),
        grid=(b, t // tm),
        in_specs=[pl.BlockSpec((1, tm, pw), lambda bi, i: (bi, i, C_RWKV // pw)),
                  pl.BlockSpec((1, 8, pw), lambda bi, i: (bi, jnp.maximum(i * (tm // 8) - 1, 0), C_RWKV // pw)),
                  pl.BlockSpec((1, 1, pw), lambda bi, i: (bi, 0, 0)),
                  vec(pw), vec(wd), mat(64, wd), vec(wd), mat(64, wd), mat(128, wd), vec(wd), vec(wd), vec(wd)],
        out_specs=[fspec] * 10,
        out_shape=[hl] * 7 + [fl] * 3,
        compiler_params=_cparams(("parallel", "parallel")),
        name="rwkv_pre",
    )(z, z, shift_prev, lp['rwkv_mu'].reshape(1, pw), lp['rwkv_w0'].reshape(1, wd), lp['rwkv_w2'].astype(BF16),
      lp['rwkv_a0'].reshape(1, wd), lp['rwkv_a2'].astype(BF16), lp['rwkv_g2'].astype(BF16),
      lp['rwkv_kk'].reshape(1, wd), lp['rwkv_ka'].reshape(1, wd), lp['rwkv_rk'].reshape(1, wd))


def _hdot(a, b):
    return jnp.dot(a, b, preferred_element_type=F32, precision=lax.Precision.HIGHEST)


def _rwkv_scan_kernel(kh_ref, rh_ref, kb_ref, bb_ref, kbe_ref, bbe_ref, v_ref, pe_ref, g_ref, bonus_ref, lnw_ref, lnb_ref,
                      s0_ref, o_ref, s_ref):
    @pl.when(pl.program_id(1) == 0)
    def _():
        s_ref[...] = s0_ref[...]

    bt, c, wd = kh_ref.shape
    n = N_HEADS * c
    row = lax.broadcasted_iota(I32, (n, wd), 0)
    col = lax.broadcasted_iota(I32, (n, wd), 1)
    same_head = (row // c) == (col // HEAD_DIM)
    tpos, jpos = row % c, col % HEAD_DIM
    strict, incl = same_head & (jpos < tpos), same_head & (jpos <= tpos)
    eye = jnp.where(same_head & (jpos == tpos), 1.0, 0.0)
    bf = lambda x: x.astype(BF16)
    block_diag = lambda ref, i: jnp.where(same_head, jnp.concatenate([ref[i]] * N_HEADS, axis=0), jnp.zeros((), BF16))
    for i in range(bt):
        kh, rh, kb, bb, kbe, bbe, v = (block_diag(ref, i) for ref in (kh_ref, rh_ref, kb_ref, bb_ref, kbe_ref, bbe_ref, v_ref))
        g1 = _dot_nt(jnp.concatenate([kh, rh], axis=0), jnp.concatenate([bb, kb], axis=0))
        a_kb = jnp.where(strict, g1[0:n, 0:n], 0.0)
        a_kk = jnp.where(strict, g1[0:n, n:2 * n], 0.0)
        a_rb = jnp.where(incl, g1[n:2 * n, 0:n], 0.0)
        a_rk = jnp.where(incl, g1[n:2 * n, n:2 * n], 0.0)
        s0 = s_ref[i]
        zy = _dot(jnp.concatenate([jnp.concatenate([kh, bf(a_kk)], axis=1), jnp.concatenate([rh, bf(a_rk)], axis=1)], axis=0),
                  jnp.concatenate([bf(s0), v], axis=0))
        z, y = zy[0:n], zy[n:2 * n]
        p = bf(-a_kb)
        tinv = eye - a_kb
        p = bf(_dot(p, p))
        for _ in range(int(np.log2(c)) - 2):
            tinv, p = tinv + _dot(bf(tinv), p), bf(_dot(p, p))
        tinv = tinv + _dot(bf(tinv), p)
        u = bf(-_dot(bf(tinv), bf(z)))
        y = y + _dot(bf(a_rb), u)
        pe_rows = pe_ref[i]
        pe_col = jnp.concatenate([pe_rows] * (LANE // c), axis=0).T
        s_ref[i] = (s0 * jnp.concatenate([pe_col] * (wd // LANE), axis=1)
                    + _dot_tn(jnp.concatenate([kbe, bbe], axis=0), jnp.concatenate([v, u], axis=0)))
        y_flat = y[0:c]
        for h in range(1, N_HEADS):
            y_flat = y_flat + y[h * c:(h + 1) * c]
        mu = _head_sum(y_flat) * (1.0 / HEAD_DIM)
        yc = y_flat - mu
        yn = yc * lax.rsqrt(_head_sum(yc * yc) * (1.0 / HEAD_DIM) + RWKV_GN_EPS)
        o_ref[i] = (yn * lnw_ref[...] + lnb_ref[...] + bonus_ref[i]) * g_ref[i]


def _rwkv_scan(kh, rh, kb, bb, kbe, bbe, v, pe, g, bonus, ln_w, ln_b, s0):
    b, t, wd = kh.shape
    c = min(RWKV_CHUNK, t)
    assert c == HEAD_DIM
    bt = _pick(b, (4, 2, 1))
    fspec = pl.BlockSpec((bt, c, wd), lambda bi, i: (bi, i, 0))
    sspec = pl.BlockSpec((bt, wd, wd), lambda bi, i: (bi, 0, 0))
    vec = pl.BlockSpec((1, wd), lambda bi, i: (0, 0))
    return pl.pallas_call(
        _rwkv_scan_kernel,
        grid=(b // bt, t // c),
        in_specs=[fspec] * 10 + [vec, vec, sspec],
        out_specs=[fspec, sspec],
        out_shape=[jax.ShapeDtypeStruct((b, t, wd), F32), jax.ShapeDtypeStruct((b, wd, wd), F32)],
        compiler_params=_cparams(("parallel", "arbitrary")),
        name="rwkv_scan",
    )(kh, rh, kb, bb, kbe, bbe, v, pe, g, bonus, ln_w.reshape(1, wd), ln_b.reshape(1, wd), s0)


SLOPES = tuple(2.0 ** (-8.0 * (h + 1) / N_HEADS) for h in range(N_HEADS))
SEQ_PER_STEP = 2


def _head_rows(tiles):
    rows8 = lax.broadcasted_iota(I32, (8, LANE), 0)
    out = jnp.zeros((8, LANE), F32)
    for h in range(N_HEADS):
        out = jnp.where(rows8 == h, tiles[:, h * LANE:(h + 1) * LANE], out)
    return out * (HEAD_DIM ** -0.5)


def _per_head(vals):
    head_row = lax.broadcasted_iota(I32, (8, 1), 0)
    out = jnp.zeros((8, 1), F32)
    for h in range(N_HEADS):
        out = jnp.where(head_row == h, vals[h], out)
    return out


def _sidx_kernel(pt_ref, qi_ref, kn_ref, w_ref, *refs, n_pages, page, group):
    o_ref = refs[group * n_pages]
    past = n_pages * page
    lane = lax.broadcasted_iota(I32, (1, LANE), 1)
    for g in range(group):
        pages = refs[g * n_pages:(g + 1) * n_pages]
        q = _head_rows(qi_ref[g])
        w = _per_head([w_ref[g, :, h:h + 1] * (N_HEADS ** -0.5) for h in range(N_HEADS)])
        q_hi, q_lo = _split_bf16(q[:, 0:HEAD_DIM])
        k_hi, k_lo = _split_bf16(jnp.concatenate([r[0, 0, 0] for r in pages], axis=1))
        logits = _dot(q_hi, k_hi) + (_dot(q_hi, k_lo) + _dot(q_lo, k_hi))
        o_ref[g, :, 0:past] = jnp.sum(jnp.maximum(logits, 0.0) * w, axis=0, keepdims=True)
        self_logit = jnp.sum(q * kn_ref[g], axis=1, keepdims=True)
        idx_self = jnp.sum(jnp.maximum(self_logit, 0.0) * w, axis=0, keepdims=True)
        o_ref[g, :, past:past + LANE] = jnp.where(lane == 0, idx_self, NEG)


def _sample_index(page_table, zi, cache_d, layer):
    b, n_pages = page_table.shape
    page = cache_d.shape[-1]
    width = n_pages * page + LANE
    grp = _pick(b, (SEQ_PER_STEP, 1))
    kern = functools.partial(_sidx_kernel, n_pages=n_pages, page=page, group=grp)
    page_spec = lambda g, p: pl.BlockSpec((1, 1, 1, HEAD_DIM, page), lambda bi, pt: (layer, pt[grp * bi + g, p], 2, 0, 0))
    zblk = lambda width_, col: pl.BlockSpec((grp, 1, width_), lambda bi, pt: (bi, 0, col // width_))
    gs = pltpu.PrefetchScalarGridSpec(
        num_scalar_prefetch=1, grid=(b // grp,),
        in_specs=[zblk(N_HEADS * LANE, ZI_Q), zblk(LANE, ZI_K), zblk(LANE, ZI_W)]
                 + [page_spec(g, p) for g in range(grp) for p in range(n_pages)],
        out_specs=pl.BlockSpec((grp, 1, width), lambda bi, pt: (bi, 0, 0)))
    return pl.pallas_call(kern, grid_spec=gs, out_shape=jax.ShapeDtypeStruct((b, 1, width), F32),
                          compiler_params=_cparams(("parallel",)), name="sample_index",
                          )(page_table, zi, zi, zi, *([cache_d] * (grp * n_pages)))


def _stopk_kernel(idx_ref, o_ref, *, topk):
    key = _sortable_key(idx_ref[...])
    rows, width = key.shape

    def bit_body(i, ans):
        cand = ans + lax.shift_left(jnp.int32(1), 31 - i)
        cnt = jnp.sum(jnp.where(key >= cand, 1, 0), axis=1, keepdims=True)
        return jnp.where(cnt >= topk, cand, ans)

    thr = lax.fori_loop(0, 32, bit_body, jnp.full((rows, 1), INT_MIN, I32))
    need = (topk - jnp.sum(jnp.where(key > thr, 1, 0), axis=1, keepdims=True)).astype(F32)
    tri = jnp.where(lax.broadcasted_iota(I32, (LANE, LANE), 0) < lax.broadcasted_iota(I32, (LANE, LANE), 1), 1.0, 0.0).astype(BF16)
    running = jnp.zeros((rows, 1), F32)
    for c in range(width // LANE):
        k = key[:, c * LANE:(c + 1) * LANE]
        eq = k == thr
        eqf = jnp.where(eq, 1.0, 0.0)
        before = _dot(eqf.astype(BF16), tri) + running
        sel = ((k > thr) | (eq & (before < need))) & (k > _KEY_HALF_NEG)
        o_ref[:, c * LANE:(c + 1) * LANE] = jnp.where(sel, 1.0, 0.0)
        running = running + jnp.sum(eqf, axis=1, keepdims=True)


def _sample_topk(idx, topk):
    return pl.pallas_call(functools.partial(_stopk_kernel, topk=topk),
                          out_shape=jax.ShapeDtypeStruct(idx.shape, F32),
                          compiler_params=pltpu.CompilerParams(vmem_limit_bytes=VMEM_LIMIT), name="sample_topk")(idx)


def _sattn_kernel(pt_ref, qn_ref, qd_ref, misc_ref, kslc_ref, kwin_ref, kdsa_ref, dsel_ref, win_ref, w1a_ref, w1b_ref,
                  posa_ref, posb_ref, w2_ref, c2s_ref, exp_ref, *refs, n_pages, page, group):
    o_ref, win_o_ref, xt_scr = refs[2 * group * n_pages:2 * group * n_pages + 3]
    for g in range(group):
        one = pl.ds(g, 1)
        _sattn_one(qn_ref.at[one], qd_ref.at[one], misc_ref.at[one], kslc_ref.at[one], kwin_ref.at[one], kdsa_ref.at[one],
                   dsel_ref.at[one], win_ref.at[:, one], w1a_ref, w1b_ref, posa_ref, posb_ref, w2_ref, c2s_ref, exp_ref,
                   refs[g * n_pages:(g + 1) * n_pages], refs[(group + g) * n_pages:(group + g + 1) * n_pages],
                   o_ref.at[one], win_o_ref.at[one], xt_scr.at[g], n_pages, page)


def _sattn_one(qn_ref, qd_ref, misc_ref, kslc_ref, kwin_ref, kdsa_ref, dsel_ref, win_ref, w1a_ref, w1b_ref, posa_ref,
               posb_ref, w2_ref, c2s_ref, exp_ref, nsa_pages, dsa_pages, o_ref, win_o_ref, xt_scr, n_pages, page):
    past = n_pages * page
    q_pos = past
    live = lax.broadcasted_iota(I32, (8, 1), 0) < N_HEADS
    qn, qd = _head_rows(qn_ref[0]), _head_rows(qd_ref[0])
    slope = _per_head(SLOPES)
    gates = _sigmoid(misc_ref[0, :, 0:3 * N_HEADS])
    gate = [_per_head([gates[:, 3 * h + c:3 * h + c + 1] for h in range(N_HEADS)]) for c in range(3)]

    def attend(q, k_t, v_t, mask, dist, new_pair, self_mask):
        sm = jnp.where(mask, _dot(q[:, 0:HEAD_DIM].astype(BF16), k_t) - slope * dist.astype(F32), NEG)
        s_self = jnp.where(self_mask, jnp.sum(q * new_pair, axis=1, keepdims=True), NEG)
        mx = jnp.maximum(s_self, jnp.max(sm, axis=1, keepdims=True))
        e = jnp.where(mask, jnp.exp(sm - mx), 0.0)
        e_self = jnp.where(self_mask, jnp.exp(s_self - mx), 0.0)
        l = e_self + jnp.sum(e, axis=1, keepdims=True)
        o = _dot_nt(e.astype(BF16), v_t) + e_self * new_pair[:, HEAD_DIM:2 * HEAD_DIM]
        return o / jnp.where(l > 0.0, l, 1.0)

    for p in range(n_pages):
        xt_scr[p * page:(p + 1) * page, :] = nsa_pages[p][0, 0, 0:2].reshape(2 * HEAD_DIM, page).T
    n16 = past // NSA_CMP_STRIDE
    x_all = jnp.concatenate([xt_scr[pl.ds(r, n16, stride=NSA_CMP_STRIDE), :] for r in range(NSA_CMP_STRIDE)], axis=1)
    a = _dot((x_all + posa_ref[...]).astype(BF16), w1a_ref[...])
    b = _dot((x_all + posb_ref[...]).astype(BF16), w1b_ref[...])
    hid = _gelu_tanh(a + pltpu.roll(b, n16 - 1, axis=0))
    comp = _dot(hid.astype(BF16), w2_ref[...]).astype(BF16)

    blk = lax.broadcasted_iota(I32, (1, n16), 1)
    d_cmp = q_pos - (blk * NSA_CMP_STRIDE + NSA_CMP_LEN - 1)
    m_cmp = (d_cmp >= 0) & live
    sm = jnp.where(m_cmp, _dot_nt(qn.astype(BF16), comp) - slope * d_cmp.astype(F32), NEG)
    e = jnp.where(m_cmp, jnp.exp(sm - jnp.max(sm, axis=1, keepdims=True)), 0.0)
    l = jnp.sum(e, axis=1, keepdims=True)
    pr = e / jnp.where(l > 0.0, l, 1.0)
    o_cmp = _dot(pr.astype(BF16), comp)[:, HEAD_DIM:2 * HEAD_DIM]
    psum = jnp.sum(pr, axis=0, keepdims=True)

    n_slc = past // NSA_SLC_LEN + 1
    imp = _hdot(jnp.broadcast_to(psum, (8, n16)), c2s_ref[...])[0:1]
    j = lax.broadcasted_iota(I32, (1, LANE), 1)
    cur = q_pos // NSA_SLC_LEN
    forced = (j == 0) | (j == cur) | (j == cur - 1)
    score = jnp.where((j * NSA_SLC_LEN <= q_pos) & (j < n_slc), imp + jnp.where(forced, FORCE_BONUS, 0.0), NEG)
    srow = jnp.broadcast_to(score, (LANE, LANE))
    scol = srow.T
    jp = lax.broadcasted_iota(I32, (LANE, LANE), 0)
    jj = lax.broadcasted_iota(I32, (LANE, LANE), 1)
    rank = jnp.sum(jnp.where((scol > srow) | ((scol == srow) & (jp < jj)), 1, 0), axis=0, keepdims=True)
    sel = jnp.where((rank < min(NSA_TOPN, n_slc)) & (score > 0.5 * NEG), 1.0, 0.0)
    sel_pos = _dot(jnp.broadcast_to(sel, (8, LANE)).astype(BF16), exp_ref[...])[0:1]

    cat = lambda refs, comp_i: jnp.concatenate([r[0, 0, comp_i] for r in refs], axis=1).astype(BF16)
    dist = q_pos - lax.broadcasted_iota(I32, (1, past), 1)
    m_slc = (sel_pos[:, 0:past] > 0.5) & live
    m_dsa = (dsel_ref[0, :, 0:past] > 0.5) & live
    self_slc = (sel_pos[:, past:past + 1] > 0.5) & live
    self_dsa = (dsel_ref[0, :, past:past + 1] > 0.5) & live
    wb = win_ref.shape[-1]
    d_win = q_pos - (past - wb + lax.broadcasted_iota(I32, (1, wb), 1))
    m_win = (d_win >= 0) & (d_win <= NSA_WINDOW) & (d_win <= q_pos) & live
    o_slc = attend(qn, cat(nsa_pages, 2), cat(nsa_pages, 3), m_slc, dist, kslc_ref[0], self_slc)
    o_win = attend(qn, win_ref[0, 0, 0].astype(BF16), win_ref[0, 0, 1].astype(BF16), m_win, d_win, kwin_ref[0], live)
    o_dsa = attend(qd, cat(dsa_pages, 0), cat(dsa_pages, 1), m_dsa, dist, kdsa_ref[0], self_dsa)
    o_nsa = gate[0] * o_cmp + gate[1] * o_slc + gate[2] * o_win
    o_ref[0] = jnp.concatenate([o_nsa, o_dsa], axis=1)

    new_cols = jnp.broadcast_to(kwin_ref[0], (LANE, LANE)).T
    lane_w = lax.broadcasted_iota(I32, (HEAD_DIM, wb), 1)
    for c in range(2):
        win_o_ref[0, c] = jnp.where(lane_w == wb - 1, new_cols[c * HEAD_DIM:(c + 1) * HEAD_DIM, 0:1],
                                    pltpu.roll(win_ref[0, 0, c], wb - 1, axis=1))


def _sample_attention(page_table, z, dsel, win_t, cmp_w, cache_n, cache_d, layer):
    b, n_pages = page_table.shape
    page = cache_n.shape[-1]
    past = n_pages * page
    wb = win_t.shape[-1]
    n16 = past // NSA_CMP_STRIDE
    w1a, w1b, pos_a, pos_b, w2 = cmp_w
    n_slc = past // NSA_SLC_LEN + 1
    c2s = np.zeros((n16, LANE), np.float32)
    c2s[:, :n_slc] = _cmp_to_slc_t(n16, n_slc).T
    expand = (np.arange(past + LANE)[None, :] // NSA_SLC_LEN == np.arange(LANE)[:, None]) & (np.arange(past + LANE)[None, :] <= past)
    grp = _pick(b, (SEQ_PER_STEP, 1))
    kern = functools.partial(_sattn_kernel, n_pages=n_pages, page=page, group=grp)
    full = lambda shape: pl.BlockSpec(shape, lambda bi, pt: (0,) * len(shape))
    nspec = lambda g, p: pl.BlockSpec((1, 1, 4, HEAD_DIM, page), lambda bi, pt: (layer, pt[grp * bi + g, p], 0, 0, 0))
    dspec = lambda g, p: pl.BlockSpec((1, 1, 2, HEAD_DIM, page), lambda bi, pt: (layer, pt[grp * bi + g, p], 0, 0, 0))
    pages = [(g, p) for g in range(grp) for p in range(n_pages)]
    qw = N_HEADS * LANE
    zblk = lambda width, col: pl.BlockSpec((grp, 1, width), lambda bi, pt: (bi, 0, col // width))
    gs = pltpu.PrefetchScalarGridSpec(
        num_scalar_prefetch=1, grid=(b // grp,),
        in_specs=[zblk(qw, C_QN), zblk(qw, C_QD), zblk(LANE, C_MISC), zblk(LANE, C_K + P_SLC * LANE),
                  zblk(LANE, C_K + P_WIN * LANE), zblk(LANE, C_K + P_DSA * LANE),
                  pl.BlockSpec((grp, 1, past + LANE), lambda bi, pt: (bi, 0, 0)),
                  pl.BlockSpec((1, grp, 2, HEAD_DIM, wb), lambda bi, pt: (layer, bi, 0, 0, 0)),
                  full(w1a.shape), full(w1b.shape), full(pos_a.shape), full(pos_b.shape), full(w2.shape),
                  full(c2s.shape), full(expand.shape)]
                 + [nspec(g, p) for g, p in pages] + [dspec(g, p) for g, p in pages],
        out_specs=[pl.BlockSpec((grp, 8, LANE), lambda bi, pt: (bi, 0, 0)),
                   pl.BlockSpec((grp, 2, HEAD_DIM, wb), lambda bi, pt: (bi, 0, 0, 0))],
        scratch_shapes=[pltpu.VMEM((grp, past, 2 * HEAD_DIM), F32)])
    return pl.pallas_call(
        kern, grid_spec=gs,
        out_shape=[jax.ShapeDtypeStruct((b, 8, LANE), F32), jax.ShapeDtypeStruct((b, 2, HEAD_DIM, wb), F32)],
        compiler_params=_cparams(("parallel",)), name="sample_attention",
    )(page_table, z, z, z, z, z, z, dsel, win_t, w1a, w1b, pos_a, pos_b, w2, jnp.asarray(c2s), jnp.asarray(expand, BF16),
      *([cache_n] * (grp * n_pages)), *([cache_d] * (grp * n_pages)))


def _smix_kernel(zg_ref, zr_ref, shift_ref, s_ref, lng_ref, lnb_ref, gw_ref, gb_ref, mu_ref, w0_ref, w2_ref, a0_ref,
                 a2_ref, g2_ref, kkw_ref, ka_ref, rk_ref, lnw_ref, lnb2_ref, oc_ref, vc_ref, or_ref, so_ref):
    bt = zg_ref.shape[0]
    zg = _gelu_tanh(zg_ref[...])
    v = _layer_norm(zg[:, GMLP_WIDTH:2 * GMLP_WIDTH], lng_ref[...], lnb_ref[...])
    vc_ref[...] = v
    oc_ref[...] = zg[:, 0:GMLP_WIDTH] * (v * gw_ref[...] + gb_ref[...])

    feat = zr_ref[...]
    f = feat + mu_ref[...] * (shift_ref[...] - feat)
    r, lw, k, vv, kk, be, g, bonus = _rwkv_features(f, w0_ref[...], w2_ref[...], a0_ref[...], a2_ref[...], g2_ref[...],
                                                    kkw_ref[...], ka_ref[...], rk_ref[...])
    w = jnp.exp(lw)
    lane_t = lax.broadcasted_iota(I32, (LANE, LANE), 1)
    y_cols = [jnp.zeros((LANE, LANE), F32) for _ in range(N_HEADS // 2)]
    for i in range(bt):
        for hp in range(N_HEADS // 2):
            v_colb = jnp.broadcast_to(vv[i:i + 1, hp * LANE:(hp + 1) * LANE], (LANE, LANE)).T
            ys = []
            for hh in range(2):
                h = 2 * hp + hh
                sl = slice(h * HEAD_DIM, (h + 1) * HEAD_DIM)
                s0 = s_ref[i, h]
                sa = -jnp.sum(s0 * kk[i:i + 1, sl], axis=1, keepdims=True)
                s1 = (s0 * w[i:i + 1, sl] + sa * be[i:i + 1, sl]
                      + v_colb[hh * HEAD_DIM:(hh + 1) * HEAD_DIM, 0:HEAD_DIM] * k[i:i + 1, sl])
                so_ref[i, h] = s1
                ys.append(jnp.sum(s1 * r[i:i + 1, sl], axis=1, keepdims=True))
            y_cols[hp] = jnp.where(lane_t == i, jnp.concatenate(ys, axis=0), y_cols[hp])
    y = jnp.concatenate([yc.T[0:bt] for yc in y_cols], axis=1)
    mu = _head_sum(y) * (1.0 / HEAD_DIM)
    yc = y - mu
    yn = yc * lax.rsqrt(_head_sum(yc * yc) * (1.0 / HEAD_DIM) + RWKV_GN_EPS)
    or_ref[...] = (yn * lnw_ref[...] + lnb2_ref[...] + bonus) * g


def _sample_mixers(z, shift_prev, wkv_prev, lp, bt):
    b = z.shape[0]
    wd = RWKV_WIDTH
    pw = 4 * wd
    gw = jnp.repeat(lp['gmlp_ws'][:, 0, 0], HEAD_DIM).reshape(1, GMLP_WIDTH)
    gb = jnp.repeat(lp['gmlp_bs'][:, 0], HEAD_DIM).reshape(1, GMLP_WIDTH)
    vec = lambda n: pl.BlockSpec((1, n), lambda i: (0, 0))
    mat = lambda r, c: pl.BlockSpec((r, c), lambda i: (0, 0))
    row = lambda n: pl.BlockSpec((bt, n), lambda i: (i, 0))
    sspec = pl.BlockSpec((bt, N_HEADS, HEAD_DIM, HEAD_DIM), lambda i: (i, 0, 0, 0))
    fl = jax.ShapeDtypeStruct((b, wd), F32)
    return pl.pallas_call(
        _smix_kernel,
        grid=(b // bt,),
        in_specs=[pl.BlockSpec((bt, 2 * GMLP_WIDTH), lambda i: (i, C_GMLP // (2 * GMLP_WIDTH))),
                  pl.BlockSpec((bt, pw), lambda i: (i, C_RWKV // pw)),
                  row(pw), sspec, vec(wd), vec(wd), vec(wd), vec(wd),
                  vec(pw), vec(wd), mat(64, wd), vec(wd), mat(64, wd), mat(128, wd), vec(wd), vec(wd), vec(wd), vec(wd), vec(wd)],
        out_specs=[row(wd), row(wd), row(wd), sspec],
        out_shape=[fl, fl, fl, jax.ShapeDtypeStruct(wkv_prev.shape, F32)],
        compiler_params=_cparams(("parallel",)),
        name="sample_mixers",
    )(z, z, shift_prev, wkv_prev, lp['gmlp_ln_g'].reshape(1, wd), lp['gmlp_ln_b'].reshape(1, wd), gw, gb,
      lp['rwkv_mu'].reshape(1, pw), lp['rwkv_w0'].reshape(1, wd), lp['rwkv_w2'].astype(BF16),
      lp['rwkv_a0'].reshape(1, wd), lp['rwkv_a2'].astype(BF16), lp['rwkv_g2'].astype(BF16),
      lp['rwkv_kk'].reshape(1, wd), lp['rwkv_ka'].reshape(1, wd), lp['rwkv_rk'].reshape(1, wd),
      lp['rwkv_ln_w'].reshape(1, wd), lp['rwkv_ln_b'].reshape(1, wd))


_W_IN_COLS = (('q_nsa', 256), ('kv_nsa', 384), ('g_nsa', 12), ('q_dsa', 256), ('kv_dsa', 128), ('q_idx', 256),
              ('k_idx', 64), ('w_idx', 4), ('gmlp', 512), ('rwkv', 1024), ('merge', 4096))


def _proj_weights(w_in):
    d = w_in.shape[0]
    parts, off = {}, 0
    for name, width in _W_IN_COLS:
        parts[name] = w_in[:, off:off + width]
        off += width

    def pad_heads(w):
        w4 = w.reshape(d, N_HEADS, HEAD_DIM)
        return jnp.concatenate([w4, jnp.zeros_like(w4)], axis=-1).reshape(d, N_HEADS * LANE)

    zeros = lambda n: jnp.zeros((d, n), w_in.dtype)
    w = jnp.concatenate([parts['merge'], parts['rwkv'], parts['gmlp'], pad_heads(parts['q_nsa']), pad_heads(parts['q_dsa']),
                         parts['g_nsa'], zeros(LANE - 12), parts['kv_nsa'], parts['kv_dsa']], axis=1)
    w_idx = jnp.concatenate([pad_heads(parts['q_idx']), parts['k_idx'], zeros(LANE - HEAD_DIM),
                             parts['w_idx'], zeros(LANE - N_HEADS)], axis=1)
    return w.astype(BF16), _split_bf16(w_idx)


def _compress_weights_rows(w1, w2, pos):
    half = NSA_CMP_STRIDE * HEAD_DIM
    return (w1[:, :half].astype(BF16), w1[:, half:].astype(BF16),
            pos[:, :NSA_CMP_STRIDE].reshape(2, 1, half), pos[:, NSA_CMP_STRIDE:].reshape(2, 1, half), w2.astype(BF16))


def _compress_weights_pairs(w1, w2, pos):
    hid = w1.shape[-1]
    w1r = w1.reshape(2, 2, NSA_CMP_STRIDE, HEAD_DIM, hid)
    posr = pos.reshape(2, 2, NSA_CMP_STRIDE, HEAD_DIM)
    bigs, poss = [], []
    for half in range(2):
        big = jnp.zeros((NSA_CMP_STRIDE, 2, HEAD_DIM, 2, hid), w1.dtype)
        for c in range(2):
            big = big.at[:, c, :, c, :].set(w1r[c, half])
        bigs.append(big.reshape(NSA_CMP_STRIDE * 2 * HEAD_DIM, 2 * hid).astype(BF16))
        poss.append(jnp.transpose(posr[:, half], (1, 0, 2)).reshape(1, NSA_CMP_STRIDE * 2 * HEAD_DIM))
    w2b = jnp.zeros((2, hid, 2, HEAD_DIM), w2.dtype)
    for c in range(2):
        w2b = w2b.at[c, :, c, :].set(w2[c])
    return bigs[0], bigs[1], poss[0], poss[1], w2b.reshape(2 * hid, 2 * HEAD_DIM).astype(BF16)


def _pick(n, cands):
    for c in cands:
        if n % c == 0:
            return c
    return n


def kernel(x_prompt, x_sample, cache_nsa, cache_dsa, state_nsa_win, state_rwkv_shift, state_rwkv_wkv, page_table,
           c_prompt, c_sample, w_ada, b_ada, norm_mix_g, norm_ffn_g, w_in, nsa_cmp_w1, nsa_cmp_w2, nsa_cmp_pos,
           gmlp_ln_g, gmlp_ln_b, gmlp_ws, gmlp_bs, rwkv_mu, rwkv_w0, rwkv_w2, rwkv_a0, rwkv_a2, rwkv_g2, rwkv_kk,
           rwkv_ka, rwkv_rk, rwkv_ln_w, rwkv_ln_b, w_branch, w_out, w_ffn_in, w_ffn_out, final_norm_g):
    depth = w_ada.shape[0]
    bp, t, d = x_prompt.shape
    bs = x_sample.shape[0]
    assert x_sample.shape[1] == 1 and t % CK == 0 and t % (8 * NSA_CMP_STRIDE) == 0
    cache_n = jnp.transpose(cache_nsa, (0, 1, 3, 4, 2))
    cache_d = jnp.transpose(cache_dsa, (0, 1, 3, 4, 2))
    win_t = jnp.transpose(state_nsa_win, (0, 1, 3, 4, 2))
    past = page_table.shape[1] * cache_nsa.shape[2]
    c_all = jnp.concatenate([c_prompt, c_sample], axis=0)
    xp, xs = x_prompt, x_sample.reshape(1, bs, d)
    tm_p = _pick(t, (512, 256, 128))
    tm_f = _pick(t, (1024, 512, 256, 128))
    tf = _pick(w_ffn_out.shape[1], (256, 128))
    tn = _pick(N_PROJ, (2432,))
    outs = {k: [] for k in ('rows_n_p', 'rows_n_s', 'rows_d_p', 'rows_d_s', 'win_p', 'win_s', 'v_s', 'shift_p', 'shift_s',
                            'wkv_p', 'wkv_s')}
    for l in range(depth):
        lp = {'gmlp_ln_g': gmlp_ln_g[l], 'gmlp_ln_b': gmlp_ln_b[l], 'gmlp_ws': gmlp_ws[l], 'gmlp_bs': gmlp_bs[l],
              'rwkv_mu': rwkv_mu[l], 'rwkv_w0': rwkv_w0[l], 'rwkv_w2': rwkv_w2[l], 'rwkv_a0': rwkv_a0[l],
              'rwkv_a2': rwkv_a2[l], 'rwkv_g2': rwkv_g2[l], 'rwkv_kk': rwkv_kk[l], 'rwkv_ka': rwkv_ka[l],
              'rwkv_rk': rwkv_rk[l], 'rwkv_ln_w': rwkv_ln_w[l], 'rwkv_ln_b': rwkv_ln_b[l]}
        last = l == depth - 1
        mod = _ada(c_all, w_ada[l], b_ada[l]).reshape(bp + bs, 6, d)
        mod_p = [mod[:bp, i:i + 1] for i in range(6)]
        mod_s = [mod[bp:, i][None] for i in range(6)]
        w_proj, (wi_hi, wi_lo) = _proj_weights(w_in[l])
        wb, wo = w_branch[l].astype(BF16), w_out[l].astype(BF16)
        wfi, wfo = w_ffn_in[l].astype(BF16), w_ffn_out[l].astype(BF16)

        zp = _inproj(xp, norm_mix_g[l], mod_p[0], mod_p[1], w_proj, tm_p, tn)
        zi = _inproj_hp(xp, norm_mix_g[l], mod_p[0], mod_p[1], wi_hi, wi_lo, tm_p)
        outs['rows_n_p'].append(zp[..., C_K:C_K + 4 * HEAD_DIM].reshape(bp, t, 4, HEAD_DIM))
        outs['rows_d_p'].append(jnp.concatenate([zp[..., C_K + 3 * LANE:C_K + 4 * LANE], zi[..., ZI_K:ZI_K + HEAD_DIM]],
                                                axis=-1).reshape(bp, t, 3, HEAD_DIM))
        wn = min(NSA_WINDOW, t)
        outs['win_p'].append(zp[:, t - wn:, C_K + 2 * LANE:C_K + 3 * LANE].reshape(bp, wn, 2, HEAD_DIM))
        outs['shift_p'].append(zp[:, t - 1, C_RWKV:C_RWKV + 4 * RWKV_WIDTH])
        n16 = t // NSA_CMP_STRIDE
        r = jnp.stack([zp[..., C_K:C_K + HEAD_DIM].reshape(bp, n16, NSA_CMP_STRIDE * HEAD_DIM),
                       zp[..., C_K + HEAD_DIM:C_K + 2 * HEAD_DIM].reshape(bp, n16, NSA_CMP_STRIDE * HEAD_DIM)], axis=1)
        w1a, w1b, pos_a, pos_b, w2 = _compress_weights_rows(nsa_cmp_w1[l], nsa_cmp_w2[l], nsa_cmp_pos[l])
        cb, ct = _nsa_compress(r, pos_a, pos_b, w1a, w1b, w2)
        kb, kt = _kprep(zp)
        o_att = _prompt_attention(zp, zi, kb, kt, _kprep_idx(zi), cb, ct)
        o_c = _gmlp(zp, lp['gmlp_ln_g'], lp['gmlp_ln_b'], lp['gmlp_ws'], lp['gmlp_bs'], _pick(t, (512, 256, 128)))
        pre = _rwkv_pre(zp, jnp.zeros((bp, 1, 4 * RWKV_WIDTH), F32), lp, _pick(t, (256, 128)))
        o_r, wkv_bd = _rwkv_scan(*pre, lp['rwkv_ln_w'], lp['rwkv_ln_b'], jnp.zeros((bp, RWKV_WIDTH, RWKV_WIDTH), F32))
        outs['wkv_p'].append(jnp.stack([wkv_bd[:, h * HEAD_DIM:(h + 1) * HEAD_DIM, h * HEAD_DIM:(h + 1) * HEAD_DIM]
                                        for h in range(N_HEADS)], axis=1).swapaxes(-1, -2))
        xp = _merge(zp, o_att, o_c, o_r, wb, wo, xp, mod_p[2], tm_p)
        xp = _ffn(xp, norm_ffn_g[l], mod_p[3], mod_p[4], mod_p[5], wfi, wfo, final_norm_g, last, tm_f, tf)

        zs = _inproj(xs, norm_mix_g[l], mod_s[0], mod_s[1], w_proj, bs, tn)
        z2 = zs[0]
        zi2 = _inproj_hp(xs, norm_mix_g[l], mod_s[0], mod_s[1], wi_hi, wi_lo, bs)[0]
        k_idx_new = zi2[:, ZI_K:ZI_K + HEAD_DIM]
        outs['rows_n_s'].append(z2[:, C_K:C_K + 4 * HEAD_DIM].reshape(bs, 1, 4, HEAD_DIM))
        outs['rows_d_s'].append(jnp.concatenate([z2[:, C_K + 3 * LANE:C_K + 4 * LANE], k_idx_new],
                                                axis=-1).reshape(bs, 1, 3, HEAD_DIM))
        outs['shift_s'].append(z2[:, C_RWKV:C_RWKV + 4 * RWKV_WIDTH])
        idx = _sample_index(page_table, zi2.reshape(bs, 1, N_HP), cache_d, l)
        width = idx.shape[-1]
        dsel = _sample_topk(idx.reshape(bs, width), min(DSA_TOPK, (past + 1) // 4)).reshape(bs, 1, width)
        cmp_w = _compress_weights_pairs(nsa_cmp_w1[l], nsa_cmp_w2[l], nsa_cmp_pos[l])
        o_rows, win_new = _sample_attention(page_table, zs.reshape(bs, 1, N_PROJ), dsel, win_t, cmp_w, cache_n, cache_d, l)
        o_heads = jnp.concatenate([o_rows[:, 0:N_HEADS, 0:HEAD_DIM].reshape(bs, N_HEADS * HEAD_DIM),
                                   o_rows[:, 0:N_HEADS, HEAD_DIM:2 * HEAD_DIM].reshape(bs, N_HEADS * HEAD_DIM)], axis=1)
        outs['win_s'].append(jnp.transpose(win_new, (0, 3, 1, 2)))
        o_cs, v_cs, o_rs, wkv_s = _sample_mixers(z2, state_rwkv_shift[l], state_rwkv_wkv[l], lp, 8)
        outs['v_s'].append(v_cs.reshape(bs, 1, GMLP_WIDTH))
        outs['wkv_s'].append(wkv_s)
        xs = _merge(zs, o_heads[None], o_cs[None], o_rs[None], wb, wo, xs, mod_s[2], bs)
        xs = _ffn(xs, norm_ffn_g[l], mod_s[3], mod_s[4], mod_s[5], wfi, wfo, final_norm_g, last, bs, tf)

    st = lambda k: jnp.stack(outs[k])
    return (xp, xs.reshape(bs, 1, d), st('rows_n_p'), st('rows_n_s'), st('rows_d_p'), st('rows_d_s'), st('win_p'),
            st('win_s'), st('v_s'), st('shift_p'), st('shift_s'), st('wkv_p'), st('wkv_s'))
```

```python
import functools

import numpy as np
import jax
import jax.numpy as jnp
from jax import lax
from jax.experimental import pallas as pl
from jax.experimental.pallas import tpu as pltpu

F32 = jnp.float32
BF16 = jnp.bfloat16
I32 = jnp.int32

HEAD_DIM = 64
N_HEADS = 4
NSA_CMP_LEN = 32
NSA_CMP_STRIDE = 16
NSA_SLC_LEN = 64
NSA_TOPN = 8
NSA_WINDOW = 512
DSA_TOPK = 256
GMLP_GROUPS = 4
GMLP_WIDTH = GMLP_GROUPS * HEAD_DIM
CHUNK = 128
RWKV_WIDTH = N_HEADS * HEAD_DIM
RWKV_GN_EPS = 64e-5
QB = 128
EPS = 1e-6
LN_EPS = 1e-5
NEG = -1e30
FORCE_BONUS = 1e4
LANE = 128
VMEM_LIMIT = 56 * 1024 * 1024

C_MERGE, C_RWKV, C_GMLP = 0, 4096, 5120
C_QN, C_QD, C_MISC, C_K = 5632, 6144, 6656, 6784
N_PROJ = 7296
K_PAIRS = 4
P_CMP, P_SLC, P_WIN, P_DSA = range(K_PAIRS)
ZI_Q, ZI_K, ZI_W, N_HP = 0, 512, 640, 768


def _cparams(sem):
    return pltpu.CompilerParams(dimension_semantics=sem, vmem_limit_bytes=VMEM_LIMIT)


def _dot(a, b):
    return jnp.dot(a, b, preferred_element_type=F32)


def _dot_nt(a, b):
    return lax.dot_general(a, b, (((1,), (1,)), ((), ())), preferred_element_type=F32)


def _dot_tn(a, b):
    return lax.dot_general(a, b, (((0,), (0,)), ((), ())), preferred_element_type=F32)


def _gelu_tanh(x):
    return 0.5 * x * (1.0 + jnp.tanh(np.sqrt(2.0 / np.pi).astype(np.float32) * (x + 0.044715 * (x * x * x))))


def _sigmoid(x):
    return 0.5 * jnp.tanh(0.5 * x) + 0.5


def _rms_mod(x, g, scale, shift):
    ms = jnp.mean(x * x, axis=-1, keepdims=True)
    return (x * lax.rsqrt(ms + EPS) * g) * (1.0 + scale) + shift


def _ada_kernel(c_ref, w_ref, b_ref, o_ref):
    c = c_ref[...]
    s = (c * _sigmoid(c)).astype(BF16)
    o_ref[...] = _dot(s, w_ref[...].astype(BF16)) + b_ref[...]


def _ada(c, w, b):
    m, d = c.shape
    n = w.shape[1]
    tn = 1536
    return pl.pallas_call(
        _ada_kernel,
        grid=(n // tn,),
        in_specs=[pl.BlockSpec((m, d), lambda j: (0, 0)),
                  pl.BlockSpec((d, tn), lambda j: (0, j)),
                  pl.BlockSpec((1, tn), lambda j: (0, j))],
        out_specs=pl.BlockSpec((m, tn), lambda j: (0, j)),
        out_shape=jax.ShapeDtypeStruct((m, n), F32),
        compiler_params=_cparams(("parallel",)),
        name="ada",
    )(c, w, b.reshape(1, n))


def _inproj_kernel(x_ref, g_ref, sh_ref, sc_ref, w_ref, o_ref):
    h = _rms_mod(x_ref[0], g_ref[...], sc_ref[0], sh_ref[0])
    o_ref[0] = _dot(h.astype(BF16), w_ref[...])


def _inproj(x, g, shift, scale, w, tm, tn):
    b, t, d = x.shape
    n = w.shape[1]
    tmod = tm if shift.shape[1] == t else 1
    mod_map = (lambda j, bi, i: (bi, i, 0)) if shift.shape[1] == t else (lambda j, bi, i: (bi, 0, 0))
    return pl.pallas_call(
        _inproj_kernel,
        grid=(n // tn, b, t // tm),
        in_specs=[pl.BlockSpec((1, tm, d), lambda j, bi, i: (bi, i, 0)),
                  pl.BlockSpec((1, d), lambda j, bi, i: (0, 0)),
                  pl.BlockSpec((1, tmod, d), mod_map),
                  pl.BlockSpec((1, tmod, d), mod_map),
                  pl.BlockSpec((d, tn), lambda j, bi, i: (0, j))],
        out_specs=pl.BlockSpec((1, tm, tn), lambda j, bi, i: (bi, i, j)),
        out_shape=jax.ShapeDtypeStruct((b, t, n), F32),
        compiler_params=_cparams(("parallel", "parallel", "parallel")),
        name="inproj",
    )(x, g.reshape(1, d), shift, scale, w)


def _split_bf16(x):
    hi = x.astype(BF16)
    return hi, (x - hi.astype(F32)).astype(BF16)


def _inproj_hp_kernel(x_ref, g_ref, sh_ref, sc_ref, wh_ref, wl_ref, o_ref):
    h_hi, h_lo = _split_bf16(_rms_mod(x_ref[0], g_ref[...], sc_ref[0], sh_ref[0]))
    o_ref[0] = _dot(h_hi, wh_ref[...]) + (_dot(h_hi, wl_ref[...]) + _dot(h_lo, wh_ref[...]))


def _inproj_hp(x, g, shift, scale, w_hi, w_lo, tm):
    b, t, d = x.shape
    n = w_hi.shape[1]
    tmod = tm if shift.shape[1] == t else 1
    mod_map = (lambda bi, i: (bi, i, 0)) if shift.shape[1] == t else (lambda bi, i: (bi, 0, 0))
    return pl.pallas_call(
        _inproj_hp_kernel,
        grid=(b, t // tm),
        in_specs=[pl.BlockSpec((1, tm, d), lambda bi, i: (bi, i, 0)),
                  pl.BlockSpec((1, d), lambda bi, i: (0, 0)),
                  pl.BlockSpec((1, tmod, d), mod_map),
                  pl.BlockSpec((1, tmod, d), mod_map),
                  pl.BlockSpec((d, n), lambda bi, i: (0, 0)),
                  pl.BlockSpec((d, n), lambda bi, i: (0, 0))],
        out_specs=pl.BlockSpec((1, tm, n), lambda bi, i: (bi, i, 0)),
        out_shape=jax.ShapeDtypeStruct((b, t, n), F32),
        compiler_params=_cparams(("parallel", "parallel")),
        name="inproj_hp",
    )(x, g.reshape(1, d), shift, scale, w_hi, w_lo)


def _merge_kernel(zm_ref, oa_ref, oc_ref, or_ref, wb_ref, wo_ref, x_ref, gate_ref, o_ref):
    bw = wb_ref.shape[1]
    d = x_ref.shape[-1]
    outs = (oa_ref[0, :, 0:bw], oa_ref[0, :, bw:2 * bw], oc_ref[0], or_ref[0])
    mixed = None
    for n, o in enumerate(outs):
        br = _dot(o.astype(BF16), wb_ref[n])
        term = _sigmoid(zm_ref[0, :, n * d:(n + 1) * d]) * br
        mixed = term if mixed is None else mixed + term
    y = _dot(mixed.astype(BF16), wo_ref[...])
    o_ref[0] = x_ref[0] + gate_ref[0] * y


def _merge(z, o_att, o_c, o_r, w_branch, w_out, x, gate, tm):
    b, t, d = x.shape
    nb, bw, _ = w_branch.shape
    tmod = tm if gate.shape[1] == t else 1
    mod_map = (lambda bi, i: (bi, i, 0)) if gate.shape[1] == t else (lambda bi, i: (bi, 0, 0))
    return pl.pallas_call(
        _merge_kernel,
        grid=(b, t // tm),
        in_specs=[pl.BlockSpec((1, tm, nb * d), lambda bi, i: (bi, i, C_MERGE // (nb * d))),
                  pl.BlockSpec((1, tm, 2 * bw), lambda bi, i: (bi, i, 0)),
                  pl.BlockSpec((1, tm, bw), lambda bi, i: (bi, i, 0)),
                  pl.BlockSpec((1, tm, bw), lambda bi, i: (bi, i, 0)),
                  pl.BlockSpec((nb, bw, d), lambda bi, i: (0, 0, 0)),
                  pl.BlockSpec((d, d), lambda bi, i: (0, 0)),
                  pl.BlockSpec((1, tm, d), lambda bi, i: (bi, i, 0)),
                  pl.BlockSpec((1, tmod, d), mod_map)],
        out_specs=pl.BlockSpec((1, tm, d), lambda bi, i: (bi, i, 0)),
        out_shape=jax.ShapeDtypeStruct((b, t, d), F32),
        compiler_params=_cparams(("parallel", "parallel")),
        name="merge",
    )(z, o_att, o_c, o_r, w_branch, w_out, x, gate)


def _ffn_kernel(x_ref, g_ref, sh_ref, sc_ref, gate_ref, wg_ref, wu_ref, wd_ref, fg_ref, o_ref,
                h_scr, acc_scr, *, final_norm):
    k = pl.program_id(2)

    @pl.when(k == 0)
    def _():
        h_scr[...] = _rms_mod(x_ref[0], g_ref[...], sc_ref[0], sh_ref[0]).astype(BF16)
        acc_scr[...] = jnp.zeros_like(acc_scr)

    h = h_scr[...]
    gt = _dot(h, wg_ref[...])
    up = _dot(h, wu_ref[...])
    act = (gt * _sigmoid(gt)) * up
    acc_scr[...] += _dot(act.astype(BF16), wd_ref[...])

    @pl.when(k == pl.num_programs(2) - 1)
    def _():
        y = x_ref[0] + gate_ref[0] * acc_scr[...]
        if final_norm:
            ms = jnp.mean(y * y, axis=-1, keepdims=True)
            y = y * lax.rsqrt(ms + EPS) * fg_ref[...]
        o_ref[0] = y


def _ffn(x, g, shift, scale, gate, w_in, w_out, final_g, final_norm, tm, tf):
    b, t, d = x.shape
    ff = w_out.shape[0]
    nk = ff // tf
    tmod = tm if gate.shape[1] == t else 1
    mod_map = (lambda bi, i, k: (bi, i, 0)) if gate.shape[1] == t else (lambda bi, i, k: (bi, 0, 0))
    return pl.pallas_call(
        functools.partial(_ffn_kernel, final_norm=final_norm),
        grid=(b, t // tm, nk),
        in_specs=[pl.BlockSpec((1, tm, d), lambda bi, i, k: (bi, i, 0)),
                  pl.BlockSpec((1, d), lambda bi, i, k: (0, 0)),
                  pl.BlockSpec((1, tmod, d), mod_map),
                  pl.BlockSpec((1, tmod, d), mod_map),
                  pl.BlockSpec((1, tmod, d), mod_map),
                  pl.BlockSpec((d, tf), lambda bi, i, k: (0, k)),
                  pl.BlockSpec((d, tf), lambda bi, i, k: (0, nk + k)),
                  pl.BlockSpec((tf, d), lambda bi, i, k: (k, 0)),
                  pl.BlockSpec((1, d), lambda bi, i, k: (0, 0))],
        out_specs=pl.BlockSpec((1, tm, d), lambda bi, i, k: (bi, i, 0)),
        out_shape=jax.ShapeDtypeStruct((b, t, d), F32),
        scratch_shapes=[pltpu.VMEM((tm, d), BF16), pltpu.VMEM((tm, d), F32)],
        compiler_params=_cparams(("parallel", "parallel", "arbitrary")),
        name="ffn",
    )(x, g.reshape(1, d), shift, scale, gate, w_in, w_in, w_out, final_g.reshape(1, d))


def _cmp_kernel(r_ref, pa_ref, pb_ref, w1a_ref, w1b_ref, w2_ref, ob_ref, ot_ref):
    n16 = r_ref.shape[2]
    comp = []
    for z in range(2):
        r = r_ref[0, z]
        a = _dot((r + pa_ref[z]).astype(BF16), w1a_ref[z])
        b = _dot((r + pb_ref[z]).astype(BF16), w1b_ref[z])
        hid = _gelu_tanh(a + pltpu.roll(b, n16 - 1, axis=0))
        comp.append(_dot(hid.astype(BF16), w2_ref[z]))
    pair = jnp.concatenate(comp, axis=1)
    ob_ref[0] = pair.astype(BF16)
    ot_ref[0] = pair.T.astype(BF16)


def _nsa_compress(r, pos_a, pos_b, w1a, w1b, w2):
    b, _, n16, kd = r.shape
    hid = w1a.shape[-1]
    return pl.pallas_call(
        _cmp_kernel,
        grid=(b,),
        in_specs=[pl.BlockSpec((1, 2, n16, kd), lambda bi: (bi, 0, 0, 0)),
                  pl.BlockSpec((2, 1, kd), lambda bi: (0, 0, 0)),
                  pl.BlockSpec((2, 1, kd), lambda bi: (0, 0, 0)),
                  pl.BlockSpec((2, kd, hid), lambda bi: (0, 0, 0)),
                  pl.BlockSpec((2, kd, hid), lambda bi: (0, 0, 0)),
                  pl.BlockSpec((2, hid, HEAD_DIM), lambda bi: (0, 0, 0))],
        out_specs=[pl.BlockSpec((1, n16, 2 * HEAD_DIM), lambda bi: (bi, 0, 0)),
                   pl.BlockSpec((1, 2 * HEAD_DIM, n16), lambda bi: (bi, 0, 0))],
        out_shape=[jax.ShapeDtypeStruct((b, n16, 2 * HEAD_DIM), BF16),
                   jax.ShapeDtypeStruct((b, 2 * HEAD_DIM, n16), BF16)],
        compiler_params=_cparams(("parallel",)),
        name="nsa_compress",
    )(r, pos_a, pos_b, w1a, w1b, w2)


CK = 256
AUX_POS = 32


def _attn_aux(t):
    pos = np.arange(t)
    aux = np.zeros((t, LANE), np.float32)
    aux[pos, pos // NSA_SLC_LEN] = 1.0
    aux[:, AUX_POS] = pos % CK
    aux[:, AUX_POS + 1] = pos // CK
    return jnp.asarray(aux, BF16)


def _kprep_kernel(z_ref, aux_ref, ob_ref, ot_ref):
    x = z_ref[0]
    ob_ref[0, 0] = jnp.concatenate([x.astype(BF16), aux_ref[...]], axis=1)
    for c in range(x.shape[0] // CK):
        ot_ref[0, 0, c] = x[c * CK:(c + 1) * CK].T[HEAD_DIM:2 * HEAD_DIM].astype(BF16)


def _kprep(z):
    b, t, _ = z.shape
    tk = _pick(t, (1024, 512, CK))
    return pl.pallas_call(
        _kprep_kernel,
        grid=(b, K_PAIRS, t // tk),
        in_specs=[pl.BlockSpec((1, tk, LANE), lambda bi, p, c: (bi, c, C_K // LANE + p)),
                  pl.BlockSpec((tk, LANE), lambda bi, p, c: (c, 0))],
        out_specs=[pl.BlockSpec((1, 1, tk, 2 * LANE), lambda bi, p, c: (bi, p, c, 0)),
                   pl.BlockSpec((1, 1, tk // CK, HEAD_DIM, CK), lambda bi, p, c: (bi, p, c, 0, 0))],
        out_shape=[jax.ShapeDtypeStruct((b, K_PAIRS, t, 2 * LANE), BF16),
                   jax.ShapeDtypeStruct((b, K_PAIRS, t // CK, HEAD_DIM, CK), BF16)],
        compiler_params=_cparams(("parallel", "parallel", "parallel")),
        name="kprep",
    )(z, _attn_aux(t))


def _kprep_idx_kernel(z_ref, o_ref):
    x = z_ref[0]
    hi = x.astype(BF16).astype(F32)
    o_ref[0] = jnp.concatenate([hi + pltpu.roll(x - hi, HEAD_DIM, axis=1), hi], axis=1).astype(BF16)


def _kprep_idx(zi):
    b, t, _ = zi.shape
    tt = _pick(t, (512, 256, 128))
    return pl.pallas_call(
        _kprep_idx_kernel,
        grid=(b, t // tt),
        in_specs=[pl.BlockSpec((1, tt, LANE), lambda bi, i: (bi, i, ZI_K // LANE))],
        out_specs=pl.BlockSpec((1, tt, 2 * LANE), lambda bi, i: (bi, i, 0)),
        out_shape=jax.ShapeDtypeStruct((b, t, 2 * LANE), BF16),
        compiler_params=_cparams(("parallel", "parallel")),
        name="kprep_idx",
    )(zi)


INT_MIN = -2 ** 31


def _sortable_key(v):
    v = jnp.where(v == 0.0, 0.0, v)
    u = lax.bitcast_convert_type(v, I32)
    return jnp.where(u < 0, u ^ 0x7FFFFFFF, u)


_KEY_HALF_NEG = int(np.array(0.5 * NEG, np.float32).view(np.int32) ^ 0x7FFFFFFF)


def _lane_consts(q0):
    lane = lax.broadcasted_iota(I32, (1, N_HEADS * QB), 1)
    hl = lane >> 7
    q_pos = q0 + (lane & (QB - 1))
    slope = jnp.where(hl == 0, 2.0 ** -2, jnp.where(hl == 1, 2.0 ** -4, jnp.where(hl == 2, 2.0 ** -6, 2.0 ** -8)))
    return q_pos, slope.astype(F32)


def _tile_heads(x):
    return jnp.concatenate([x] * N_HEADS, axis=1)


def _place_heads(tiles):
    lane = lax.broadcasted_iota(I32, (QB, LANE), 1)
    out = []
    for t in range(N_HEADS // 2):
        out.append(jnp.where(lane < HEAD_DIM, pltpu.roll(tiles[2 * t], HEAD_DIM, axis=1), tiles[2 * t + 1]))
    return jnp.concatenate(out, axis=1)


def _attn_kernel(qn_ref, qd_ref, misc_ref, qi_ref, wi_ref, kb_ref, kt_ref, ki_ref, cb_ref, ct_ref, c2s_ref, o_ref,
                 key_scr, dsel_scr, *, seq_len, topk):
    t = seq_len
    q0 = pl.program_id(1) * QB
    nc = (q0 + QB + CK - 1) // CK
    hq = N_HEADS * QB
    scale = HEAD_DIM ** -0.5

    def stack_q(ref):
        return jnp.concatenate([ref[0, :, h * LANE:(h + 1) * LANE] for h in range(N_HEADS)], axis=0) * scale

    qn, qd = stack_q(qn_ref).astype(BF16), stack_q(qd_ref).astype(BF16)
    qi = stack_q(qi_ref)
    qi_cat = jnp.concatenate([(qi + pltpu.roll(qi, HEAD_DIM, axis=1)).astype(BF16),
                              (qi - qi.astype(BF16).astype(F32)).astype(BF16)], axis=1)
    misc_t = misc_ref[0].T
    wi_t = wi_ref[0].T
    q_pos, slope = _lane_consts(q0)

    n16 = cb_ref.shape[1]
    sc = _dot_nt(cb_ref[0], qn)
    cmp_end = lax.broadcasted_iota(I32, (n16, hq), 0) * NSA_CMP_STRIDE + (NSA_CMP_LEN - 1)
    d = q_pos - cmp_end
    mask = d >= 0
    sm = jnp.where(mask, sc - slope * d.astype(F32), NEG)
    e = jnp.where(mask, jnp.exp(sm - jnp.max(sm, axis=0, keepdims=True)), 0.0)
    l = jnp.sum(e, axis=0, keepdims=True)
    p = e / jnp.where(l > 0.0, l, 1.0)
    o_cmp_t = _dot(ct_ref[0], p.astype(BF16))
    psum_t = p[:, 0:QB]
    for h in range(1, N_HEADS):
        psum_t = psum_t + p[:, h * QB:(h + 1) * QB]
    imp_t = jnp.dot(c2s_ref[...], psum_t, preferred_element_type=F32, precision=lax.Precision.HIGHEST)

    nslc = c2s_ref.shape[0]
    jrow = lax.broadcasted_iota(I32, (nslc, QB), 0)
    qp = q0 + lax.broadcasted_iota(I32, (nslc, QB), 1)
    cur = qp >> 6
    adm = jrow * NSA_SLC_LEN <= qp
    forced = (jrow == 0) | (jrow == cur) | (jrow == cur - 1)
    score = jnp.where(adm, imp_t + jnp.where(forced, FORCE_BONUS, 0.0), NEG)
    rank = jnp.zeros((nslc, QB), I32)
    for j in range(nslc):
        row = score[j:j + 1, :]
        rank = rank + jnp.where((row > score) | ((row == score) & (jrow > j)), 1, 0)
    sel_bias = jnp.where((rank < min(NSA_TOPN, nslc)) & (score > 0.5 * NEG), 0.0, NEG)

    w_rows = [wi_t[h:h + 1, :] * (N_HEADS ** -0.5) for h in range(N_HEADS)]

    tri = jnp.where(lax.broadcasted_iota(I32, (CK, CK), 1) < lax.broadcasted_iota(I32, (CK, CK), 0), 1.0, 0.0).astype(BF16)

    def dsa_select(n_spans):
        span = CK
        for c in range(n_spans):
            lg = jnp.maximum(_dot_nt(ki_ref[0, c * span:(c + 1) * span, :], qi_cat), 0.0)
            idx = lg[:, 0:QB] * w_rows[0]
            for h in range(1, N_HEADS):
                idx = idx + lg[:, h * QB:(h + 1) * QB] * w_rows[h]
            kpos = c * span + lax.broadcasted_iota(I32, (span, QB), 0)
            causal = kpos <= q0 + lax.broadcasted_iota(I32, (span, QB), 1)
            key_scr[c * span:(c + 1) * span, :] = _sortable_key(jnp.where(causal, idx, NEG))

        def count(pred_fn):
            acc = jnp.zeros((1, QB), I32)
            for c in range(n_spans):
                acc = acc + jnp.sum(jnp.where(pred_fn(key_scr[c * span:(c + 1) * span, :]), 1, 0), axis=0, keepdims=True)
            return acc

        def bit_body(i, ans):
            cand = ans + lax.shift_left(jnp.int32(1), 31 - i)
            return jnp.where(count(lambda k: k >= cand) >= topk, cand, ans)

        thr = lax.fori_loop(0, 32, bit_body, jnp.full((1, QB), INT_MIN, I32))
        need = (topk - count(lambda k: k > thr)).astype(F32)
        running = jnp.zeros((1, QB), F32)
        for c in range(n_spans):
            k = key_scr[c * CK:(c + 1) * CK, :]
            eq = k == thr
            eqf = jnp.where(eq, 1.0, 0.0)
            before = _dot(tri, eqf.astype(BF16)) + running
            sel = ((k > thr) | (eq & (before < need))) & (k > _KEY_HALF_NEG)
            dsel_scr[c * CK:(c + 1) * CK, :] = jnp.where(sel, 0.0, NEG)
            running = running + jnp.sum(eqf, axis=0, keepdims=True)
        return jnp.int32(0)

    lax.switch(nc - 1, [functools.partial(dsa_select, n) for n in range(1, t // CK + 1)])

    sel_bias_t = jnp.concatenate([sel_bias, jnp.zeros((LANE - nslc, QB), F32)], axis=0).T
    lane = lax.broadcasted_iota(I32, (QB, LANE), 1)
    q_slc, q_dsa = [], []
    for h in range(N_HEADS):
        pos_cols = jnp.where(lane == AUX_POS, SLOPES[h], jnp.where(lane == AUX_POS + 1, SLOPES[h] * CK, 0.0))
        q_slc.append(sel_bias_t + pos_cols)
        q_dsa.append(pos_cols)
    qn_cat = jnp.concatenate([qn, jnp.concatenate(q_slc, axis=0).astype(BF16)], axis=1)
    qd_cat = jnp.concatenate([qd, jnp.concatenate(q_dsa, axis=0).astype(BF16)], axis=1)
    def online(state, s, vt):
        m_old, l_old, acc_old = state
        m_new = jnp.maximum(m_old, jnp.max(s, axis=0, keepdims=True))
        alpha = jnp.exp(m_old - m_new)
        e_ = jnp.exp(s - m_new)
        return m_new, alpha * l_old + jnp.sum(e_, axis=0, keepdims=True), alpha * acc_old + _dot(vt, e_.astype(BF16))

    def scores(c, diagonal):
        rows = pl.ds(pl.multiple_of(c * CK, CK), CK)
        s_slc = _dot_nt(kb_ref[0, P_SLC, rows, :], qn_cat)
        if diagonal:
            kpos = c * CK + lax.broadcasted_iota(I32, (CK, hq), 0)
            s_slc = jnp.where(kpos <= q_pos, s_slc, NEG)
        return s_slc, _dot_nt(kb_ref[0, P_DSA, rows, :], qd_cat) + _tile_heads(dsel_scr[rows, :])

    def flash_chunks(chunks, state, diagonal=None):
        diagonal = diagonal or (False,) * len(chunks)
        sc = [scores(c, d) for c, d in zip(chunks, diagonal)]
        st_slc, st_dsa = state
        for c, (s_slc, s_dsa) in zip(chunks, sc):
            st_slc = online(st_slc, s_slc, kt_ref[0, P_SLC, c])
            st_dsa = online(st_dsa, s_dsa, kt_ref[0, P_DSA, c])
        return st_slc, st_dsa

    init = (jnp.full((1, hq), NEG, F32), jnp.zeros((1, hq), F32), jnp.zeros((HEAD_DIM, hq), F32))
    n_full = nc - 1
    state = lax.fori_loop(0, n_full // 2, lambda i, st: flash_chunks((2 * i, 2 * i + 1), st), (init, init))
    state = lax.cond(n_full % 2 == 1,
                     lambda st: flash_chunks((nc - 2, nc - 1), st, diagonal=(False, True)),
                     lambda st: flash_chunks((nc - 1,), st, diagonal=(True,)), state)

    def finish(st):
        m_, l_, acc = st
        return acc * jnp.where(m_ > 0.5 * NEG, 1.0 / l_, 0.0)

    o_slc_t, o_dsa_t = finish(state[0]), finish(state[1])

    wk = min(NSA_WINDOW + QB, t)
    ws = pl.multiple_of(jnp.clip(q0 - NSA_WINDOW, 0, t - wk), QB)
    kw = kb_ref[0, P_WIN, pl.ds(ws, wk), :]
    dw = q_pos - (ws + lax.broadcasted_iota(I32, (wk, hq), 0))
    qw_cat = jnp.concatenate([qn, qd_cat[:, LANE:2 * LANE]], axis=1)
    sm = jnp.where((dw >= 0) & (dw <= NSA_WINDOW), _dot_nt(kw, qw_cat), NEG)
    mw = jnp.max(sm, axis=0, keepdims=True)
    e = jnp.exp(sm - mw)
    lw = jnp.sum(e, axis=0, keepdims=True)
    o_win_t = _dot_tn(kw[:, 0:LANE], e.astype(BF16)) * jnp.where(mw > 0.5 * NEG, 1.0 / lw, 0.0)

    gates = _sigmoid(misc_t[0:3 * N_HEADS, :])
    nsa_tiles, dsa_tiles = [], []
    top = jnp.zeros((HEAD_DIM, QB), F32)
    vrows = slice(HEAD_DIM, 2 * HEAD_DIM)
    for h in range(N_HEADS):
        cols = slice(h * QB, (h + 1) * QB)
        on = (gates[3 * h:3 * h + 1, :] * o_cmp_t[vrows, cols] + gates[3 * h + 1:3 * h + 2, :] * o_slc_t[:, cols]
              + gates[3 * h + 2:3 * h + 3, :] * o_win_t[vrows, cols])
        nsa_tiles.append(jnp.concatenate([top, on], axis=0).T)
        dsa_tiles.append(jnp.concatenate([top, o_dsa_t[:, cols]], axis=0).T)
    o_ref[0, :, 0:N_HEADS * HEAD_DIM] = _place_heads(nsa_tiles)
    o_ref[0, :, N_HEADS * HEAD_DIM:2 * N_HEADS * HEAD_DIM] = _place_heads(dsa_tiles)


def _cmp_to_slc_t(n16, n_slc):
    start = np.arange(n16) * NSA_CMP_STRIDE
    bstart = np.arange(n_slc) * NSA_SLC_LEN
    ov = np.minimum(start[:, None] + NSA_CMP_LEN, bstart[None, :] + NSA_SLC_LEN) - np.maximum(start[:, None], bstart[None, :])
    return (np.clip(ov, 0, None) / NSA_CMP_LEN).T.astype(np.float32)


def _prompt_attention(z, zi, kb, kt, ki, cb, ct):
    b, t, _ = z.shape
    n16 = cb.shape[1]
    n_slc = -(-t // NSA_SLC_LEN)
    assert n_slc <= AUX_POS and t // CK <= 256 and t % (2 * CK) == 0
    hq = N_HEADS * QB
    qw = N_HEADS * LANE
    topk = min(DSA_TOPK, t // 4)
    c2s = jnp.asarray(_cmp_to_slc_t(n16, n_slc))
    qspec = lambda col: pl.BlockSpec((1, QB, qw), lambda bi, i: (bi, i, col // qw))
    tile = lambda col: pl.BlockSpec((1, QB, LANE), lambda bi, i: (bi, i, col // LANE))
    return pl.pallas_call(
        functools.partial(_attn_kernel, seq_len=t, topk=topk),
        grid=(b, t // QB),
        in_specs=[qspec(C_QN), qspec(C_QD), tile(C_MISC), qspec(ZI_Q), tile(ZI_W),
                  pl.BlockSpec((1, K_PAIRS, t, 2 * LANE), lambda bi, i: (bi, 0, 0, 0)),
                  pl.BlockSpec((1, K_PAIRS, t // CK, HEAD_DIM, CK), lambda bi, i: (bi, 0, 0, 0, 0)),
                  pl.BlockSpec((1, t, 2 * LANE), lambda bi, i: (bi, 0, 0)),
                  pl.BlockSpec((1, n16, LANE), lambda bi, i: (bi, 0, 0)),
                  pl.BlockSpec((1, LANE, n16), lambda bi, i: (bi, 0, 0)),
                  pl.BlockSpec((n_slc, n16), lambda bi, i: (0, 0))],
        out_specs=pl.BlockSpec((1, QB, 2 * N_HEADS * HEAD_DIM), lambda bi, i: (bi, i, 0)),
        out_shape=jax.ShapeDtypeStruct((b, t, 2 * N_HEADS * HEAD_DIM), F32),
        scratch_shapes=[pltpu.VMEM((t, QB), I32), pltpu.VMEM((t, QB), F32)],
        compiler_params=_cparams(("parallel", "parallel")),
        name="prompt_attention",
    )(z, z, z, zi, zi, kb, kt, ki, cb, ct, c2s)


def _layer_norm(v, g, b):
    mu = jnp.mean(v, axis=-1, keepdims=True)
    var = jnp.mean(jnp.square(v - mu), axis=-1, keepdims=True)
    return (v - mu) * lax.rsqrt(var + LN_EPS) * g + b


def _gmlp_kernel(z_ref, lng_ref, lnb_ref, ws_ref, bs_ref, o_ref):
    tc = z_ref.shape[1]
    c = ws_ref.shape[1]
    tril = lax.broadcasted_iota(I32, (c, c), 1) <= lax.broadcasted_iota(I32, (c, c), 0)
    lane_g = lax.broadcasted_iota(I32, (1, GMLP_WIDTH), 1) >> 6
    ws = [jnp.where(tril, ws_ref[g], 0.0).astype(BF16) for g in range(GMLP_GROUPS)]
    for ci in range(tc // c):
        zg = _gelu_tanh(z_ref[0, ci * c:(ci + 1) * c, :])
        u = zg[:, 0:GMLP_WIDTH]
        v = _layer_norm(zg[:, GMLP_WIDTH:2 * GMLP_WIDTH], lng_ref[...], lnb_ref[...]).astype(BF16)
        s = bs_ref[...]
        for g in range(GMLP_GROUPS):
            s = s + jnp.where(lane_g == g, _dot(ws[g], v), 0.0)
        o_ref[0, ci * c:(ci + 1) * c, :] = u * s


def _gmlp(z, ln_g, ln_b, w_s, b_s, tc):
    b, t, _ = z.shape
    c = w_s.shape[1]
    bs_exp = jnp.repeat(b_s.T, HEAD_DIM, axis=1)
    return pl.pallas_call(
        _gmlp_kernel,
        grid=(b, t // tc),
        in_specs=[pl.BlockSpec((1, tc, 2 * GMLP_WIDTH), lambda bi, i: (bi, i, C_GMLP // (2 * GMLP_WIDTH))),
                  pl.BlockSpec((1, GMLP_WIDTH), lambda bi, i: (0, 0)),
                  pl.BlockSpec((1, GMLP_WIDTH), lambda bi, i: (0, 0)),
                  pl.BlockSpec((GMLP_GROUPS, c, c), lambda bi, i: (0, 0, 0)),
                  pl.BlockSpec((c, GMLP_WIDTH), lambda bi, i: (0, 0))],
        out_specs=pl.BlockSpec((1, tc, GMLP_WIDTH), lambda bi, i: (bi, i, 0)),
        out_shape=jax.ShapeDtypeStruct((b, t, GMLP_WIDTH), F32),
        compiler_params=_cparams(("parallel", "parallel")),
        name="gmlp",
    )(z, ln_g.reshape(1, -1), ln_b.reshape(1, -1), w_s, bs_exp)


def _head_sum(x):
    lane_h = lax.broadcasted_iota(I32, (1, x.shape[-1]), 1) >> 6
    out = jnp.zeros_like(x)
    for h in range(x.shape[-1] // HEAD_DIM):
        msk = lane_h == h
        out = out + jnp.where(msk, jnp.sum(jnp.where(msk, x, 0.0), axis=-1, keepdims=True), 0.0)
    return out


def _softplus(x):
    return jnp.maximum(x, 0.0) + jnp.log1p(jnp.exp(-jnp.abs(x)))


def _rwkv_features(f, w0, w2, a0, a2, g2, kkw, ka, rk):
    wd = RWKV_WIDTH
    r, k, v = f[:, 0:wd], f[:, wd:2 * wd], f[:, 2 * wd:3 * wd]
    wl, al, gl = f[:, 3 * wd:3 * wd + 64], f[:, 3 * wd + 64:3 * wd + 128], f[:, 3 * wd + 128:3 * wd + 256]
    w_log = -_softplus(-(w0 + _dot(jnp.tanh(wl).astype(BF16), w2))) - 0.5
    log_decay = -jnp.exp(w_log)
    a = _sigmoid(a0 + _dot(al.astype(BF16), a2))
    g = _dot(_sigmoid(gl).astype(BF16), g2)
    kk = k * kkw
    kk = kk * lax.rsqrt(_head_sum(kk * kk) + 1e-12)
    k = k * (1.0 + (a - 1.0) * ka)
    bonus = _head_sum(r * k * rk) * v
    return r, log_decay, k, v, kk, kk * a, g, bonus


RWKV_CHUNK = 64


def _rwkv_pre_kernel(f_ref, prev_ref, shift_ref, mu_ref, w0_ref, w2_ref, a0_ref, a2_ref, g2_ref, kkw_ref, ka_ref, rk_ref,
                     kh_o, rh_o, kb_o, bb_o, kbe_o, bbe_o, v_o, pe_o, g_o, bonus_o, *, chunk):
    feat = f_ref[0]
    tm = feat.shape[0]
    first = jnp.where(pl.program_id(1) == 0, shift_ref[0], prev_ref[0, 7:8, :])
    prev = jnp.where(lax.broadcasted_iota(I32, (tm, 1), 0) == 0, first, pltpu.roll(feat, 1, axis=0))
    f = feat + mu_ref[...] * (prev - feat)
    r, lw, k, v, kk, be, g, bonus = _rwkv_features(f, w0_ref[...], w2_ref[...], a0_ref[...], a2_ref[...], g2_ref[...],
                                                   kkw_ref[...], ka_ref[...], rk_ref[...])
    row = lax.broadcasted_iota(I32, (tm, tm), 0)
    col = lax.broadcasted_iota(I32, (tm, tm), 1)
    tri = jnp.where((col >= (row // chunk) * chunk) & (col <= row), 1.0, 0.0).astype(BF16)
    p1 = lw.astype(BF16)
    r1 = lw - p1.astype(F32)
    p2 = r1.astype(BF16)
    p3 = (r1 - p2.astype(F32)).astype(BF16)
    cum = _dot(tri, p1) + (_dot(tri, p2) + _dot(tri, p3))
    cum_end = jnp.concatenate([jnp.broadcast_to(cum[c * chunk + chunk - 1:(c + 1) * chunk, :], (chunk, cum.shape[1]))
                               for c in range(tm // chunk)], axis=0)
    down, to_end = jnp.exp(-cum), jnp.exp(cum_end - cum)
    for o, x in ((kh_o, kk * jnp.exp(cum - lw)), (rh_o, r * jnp.exp(cum)), (kb_o, k * down), (bb_o, be * down),
                 (kbe_o, k * to_end), (bbe_o, be * to_end), (v_o, v), (pe_o, jnp.exp(cum_end)), (g_o, g), (bonus_o, bonus)):
        o[0] = x.astype(o.dtype)


def _rwkv_pre(z, shift_prev, lp, tm):
    b, t, _ = z.shape
    wd = RWKV_WIDTH
    pw = 4 * wd
    vec = lambda n: pl.BlockSpec((1, n), lambda bi, i: (0, 0))
    mat = lambda r, c: pl.BlockSpec((r, c), lambda bi, i: (0, 0))
    fl = jax.ShapeDtypeStruct((b, t, wd), F32)
    hl = jax.ShapeDtypeStruct((b, t, wd), BF16)
    fspec = pl.BlockSpec((1, tm, wd), lambda bi, i: (bi, i, 0))
    return pl.pallas_call(
        functools.partial(_rwkv_pre_kernel, chunk=min(RWKV_CHUNK, t)),
        grid=(b, t // tm),
        in_specs=[pl.BlockSpec((1, tm, pw), lambda bi, i: (bi, i, C_RWKV // pw)),
                  pl.BlockSpec((1, 8, pw), lambda bi, i: (bi, jnp.maximum(i * (tm // 8) - 1, 0), C_RWKV // pw)),
                  pl.BlockSpec((1, 1, pw), lambda bi, i: (bi, 0, 0)),
                  vec(pw), vec(wd), mat(64, wd), vec(wd), mat(64, wd), mat(128, wd), vec(wd), vec(wd), vec(wd)],
        out_specs=[fspec] * 10,
        out_shape=[hl] * 7 + [fl] * 3,
        compiler_params=_cparams(("parallel", "parallel")),
        name="rwkv_pre",
    )(z, z, shift_prev, lp['rwkv_mu'].reshape(1, pw), lp['rwkv_w0'].reshape(1, wd), lp['rwkv_w2'].astype(BF16),
      lp['rwkv_a0'].reshape(1, wd), lp['rwkv_a2'].astype(BF16), lp['rwkv_g2'].astype(BF16),
      lp['rwkv_kk'].reshape(1, wd), lp['rwkv_ka'].reshape(1, wd), lp['rwkv_rk'].reshape(1, wd))


def _hdot(a, b):
    return jnp.dot(a, b, preferred_element_type=F32, precision=lax.Precision.HIGHEST)


def _rwkv_scan_kernel(kh_ref, rh_ref, kb_ref, bb_ref, kbe_ref, bbe_ref, v_ref, pe_ref, g_ref, bonus_ref, lnw_ref, lnb_ref,
                      s0_ref, o_ref, s_ref):
    @pl.when(pl.program_id(1) == 0)
    def _():
        s_ref[...] = s0_ref[...]

    bt, c, wd = kh_ref.shape
    n = N_HEADS * c
    row = lax.broadcasted_iota(I32, (n, wd), 0)
    col = lax.broadcasted_iota(I32, (n, wd), 1)
    same_head = (row // c) == (col // HEAD_DIM)
    tpos, jpos = row % c, col % HEAD_DIM
    strict, incl = same_head & (jpos < tpos), same_head & (jpos <= tpos)
    eye = jnp.where(same_head & (jpos == tpos), 1.0, 0.0)
    bf = lambda x: x.astype(BF16)
    block_diag = lambda ref, i: jnp.where(same_head, jnp.concatenate([ref[i]] * N_HEADS, axis=0), jnp.zeros((), BF16))
    for i in range(bt):
        kh, rh, kb, bb, kbe, bbe, v = (block_diag(ref, i) for ref in (kh_ref, rh_ref, kb_ref, bb_ref, kbe_ref, bbe_ref, v_ref))
        g1 = _dot_nt(jnp.concatenate([kh, rh], axis=0), jnp.concatenate([bb, kb], axis=0))
        a_kb = jnp.where(strict, g1[0:n, 0:n], 0.0)
        a_kk = jnp.where(strict, g1[0:n, n:2 * n], 0.0)
        a_rb = jnp.where(incl, g1[n:2 * n, 0:n], 0.0)
        a_rk = jnp.where(incl, g1[n:2 * n, n:2 * n], 0.0)
        s0 = s_ref[i]
        zy = _dot(jnp.concatenate([jnp.concatenate([kh, bf(a_kk)], axis=1), jnp.concatenate([rh, bf(a_rk)], axis=1)], axis=0),
                  jnp.concatenate([bf(s0), v], axis=0))
        z, y = zy[0:n], zy[n:2 * n]
        p = bf(-a_kb)
        tinv = eye - a_kb
        p = bf(_dot(p, p))
        for _ in range(int(np.log2(c)) - 2):
            tinv, p = tinv + _dot(bf(tinv), p), bf(_dot(p, p))
        tinv = tinv + _dot(bf(tinv), p)
        u = bf(-_dot(bf(tinv), bf(z)))
        y = y + _dot(bf(a_rb), u)
        pe_rows = pe_ref[i]
        pe_col = jnp.concatenate([pe_rows] * (LANE // c), axis=0).T
        s_ref[i] = (s0 * jnp.concatenate([pe_col] * (wd // LANE), axis=1)
                    + _dot_tn(jnp.concatenate([kbe, bbe], axis=0), jnp.concatenate([v, u], axis=0)))
        y_flat = y[0:c]
        for h in range(1, N_HEADS):
            y_flat = y_flat + y[h * c:(h + 1) * c]
        mu = _head_sum(y_flat) * (1.0 / HEAD_DIM)
        yc = y_flat - mu
        yn = yc * lax.rsqrt(_head_sum(yc * yc) * (1.0 / HEAD_DIM) + RWKV_GN_EPS)
        o_ref[i] = (yn * lnw_ref[...] + lnb_ref[...] + bonus_ref[i]) * g_ref[i]


def _rwkv_scan(kh, rh, kb, bb, kbe, bbe, v, pe, g, bonus, ln_w, ln_b, s0):
    b, t, wd = kh.shape
    c = min(RWKV_CHUNK, t)
    assert c == HEAD_DIM
    bt = _pick(b, (4, 2, 1))
    fspec = pl.BlockSpec((bt, c, wd), lambda bi, i: (bi, i, 0))
    sspec = pl.BlockSpec((bt, wd, wd), lambda bi, i: (bi, 0, 0))
    vec = pl.BlockSpec((1, wd), lambda bi, i: (0, 0))
    return pl.pallas_call(
        _rwkv_scan_kernel,
        grid=(b // bt, t // c),
        in_specs=[fspec] * 10 + [vec, vec, sspec],
        out_specs=[fspec, sspec],
        out_shape=[jax.ShapeDtypeStruct((b, t, wd), F32), jax.ShapeDtypeStruct((b, wd, wd), F32)],
        compiler_params=_cparams(("parallel", "arbitrary")),
        name="rwkv_scan",
    )(kh, rh, kb, bb, kbe, bbe, v, pe, g, bonus, ln_w.reshape(1, wd), ln_b.reshape(1, wd), s0)


SLOPES = tuple(2.0 ** (-8.0 * (h + 1) / N_HEADS) for h in range(N_HEADS))
SEQ_PER_STEP = 2


def _head_rows(tiles):
    rows8 = lax.broadcasted_iota(I32, (8, LANE), 0)
    out = jnp.zeros((8, LANE), F32)
    for h in range(N_HEADS):
        out = jnp.where(rows8 == h, tiles[:, h * LANE:(h + 1) * LANE], out)
    return out * (HEAD_DIM ** -0.5)


def _per_head(vals):
    head_row = lax.broadcasted_iota(I32, (8, 1), 0)
    out = jnp.zeros((8, 1), F32)
    for h in range(N_HEADS):
        out = jnp.where(head_row == h, vals[h], out)
    return out


def _sidx_kernel(pt_ref, qi_ref, kn_ref, w_ref, *refs, n_pages, page, group):
    o_ref = refs[group * n_pages]
    past = n_pages * page
    lane = lax.broadcasted_iota(I32, (1, LANE), 1)
    for g in range(group):
        pages = refs[g * n_pages:(g + 1) * n_pages]
        q = _head_rows(qi_ref[g])
        w = _per_head([w_ref[g, :, h:h + 1] * (N_HEADS ** -0.5) for h in range(N_HEADS)])
        q_hi, q_lo = _split_bf16(q[:, 0:HEAD_DIM])
        k_hi, k_lo = _split_bf16(jnp.concatenate([r[0, 0, 0] for r in pages], axis=1))
        logits = _dot(q_hi, k_hi) + (_dot(q_hi, k_lo) + _dot(q_lo, k_hi))
        o_ref[g, :, 0:past] = jnp.sum(jnp.maximum(logits, 0.0) * w, axis=0, keepdims=True)
        self_logit = jnp.sum(q * kn_ref[g], axis=1, keepdims=True)
        idx_self = jnp.sum(jnp.maximum(self_logit, 0.0) * w, axis=0, keepdims=True)
        o_ref[g, :, past:past + LANE] = jnp.where(lane == 0, idx_self, NEG)


def _sample_index(page_table, zi, cache_d, layer):
    b, n_pages = page_table.shape
    page = cache_d.shape[-1]
    width = n_pages * page + LANE
    grp = _pick(b, (SEQ_PER_STEP, 1))
    kern = functools.partial(_sidx_kernel, n_pages=n_pages, page=page, group=grp)
    page_spec = lambda g, p: pl.BlockSpec((1, 1, 1, HEAD_DIM, page), lambda bi, pt: (layer, pt[grp * bi + g, p], 2, 0, 0))
    zblk = lambda width_, col: pl.BlockSpec((grp, 1, width_), lambda bi, pt: (bi, 0, col // width_))
    gs = pltpu.PrefetchScalarGridSpec(
        num_scalar_prefetch=1, grid=(b // grp,),
        in_specs=[zblk(N_HEADS * LANE, ZI_Q), zblk(LANE, ZI_K), zblk(LANE, ZI_W)]
                 + [page_spec(g, p) for g in range(grp) for p in range(n_pages)],
        out_specs=pl.BlockSpec((grp, 1, width), lambda bi, pt: (bi, 0, 0)))
    return pl.pallas_call(kern, grid_spec=gs, out_shape=jax.ShapeDtypeStruct((b, 1, width), F32),
                          compiler_params=_cparams(("parallel",)), name="sample_index",
                          )(page_table, zi, zi, zi, *([cache_d] * (grp * n_pages)))


def _stopk_kernel(idx_ref, o_ref, *, topk):
    key = _sortable_key(idx_ref[...])
    rows, width = key.shape

    def bit_body(i, ans):
        cand = ans + lax.shift_left(jnp.int32(1), 31 - i)
        cnt = jnp.sum(jnp.where(key >= cand, 1, 0), axis=1, keepdims=True)
        return jnp.where(cnt >= topk, cand, ans)

    thr = lax.fori_loop(0, 32, bit_body, jnp.full((rows, 1), INT_MIN, I32))
    need = (topk - jnp.sum(jnp.where(key > thr, 1, 0), axis=1, keepdims=True)).astype(F32)
    tri = jnp.where(lax.broadcasted_iota(I32, (LANE, LANE), 0) < lax.broadcasted_iota(I32, (LANE, LANE), 1), 1.0, 0.0).astype(BF16)
    running = jnp.zeros((rows, 1), F32)
    for c in range(width // LANE):
        k = key[:, c * LANE:(c + 1) * LANE]
        eq = k == thr
        eqf = jnp.where(eq, 1.0, 0.0)
        before = _dot(eqf.astype(BF16), tri) + running
        sel = ((k > thr) | (eq & (before < need))) & (k > _KEY_HALF_NEG)
        o_ref[:, c * LANE:(c + 1) * LANE] = jnp.where(sel, 1.0, 0.0)
        running = running + jnp.sum(eqf, axis=1, keepdims=True)


def _sample_topk(idx, topk):
    return pl.pallas_call(functools.partial(_stopk_kernel, topk=topk),
                          out_shape=jax.ShapeDtypeStruct(idx.shape, F32),
                          compiler_params=pltpu.CompilerParams(vmem_limit_bytes=VMEM_LIMIT), name="sample_topk")(idx)


def _sattn_kernel(pt_ref, qn_ref, qd_ref, misc_ref, kslc_ref, kwin_ref, kdsa_ref, dsel_ref, win_ref, w1a_ref, w1b_ref,
                  posa_ref, posb_ref, w2_ref, c2s_ref, exp_ref, *refs, n_pages, page, group):
    o_ref, win_o_ref, xt_scr = refs[2 * group * n_pages:2 * group * n_pages + 3]
    for g in range(group):
        one = pl.ds(g, 1)
        _sattn_one(qn_ref.at[one], qd_ref.at[one], misc_ref.at[one], kslc_ref.at[one], kwin_ref.at[one], kdsa_ref.at[one],
                   dsel_ref.at[one], win_ref.at[:, one], w1a_ref, w1b_ref, posa_ref, posb_ref, w2_ref, c2s_ref, exp_ref,
                   refs[g * n_pages:(g + 1) * n_pages], refs[(group + g) * n_pages:(group + g + 1) * n_pages],
                   o_ref.at[one], win_o_ref.at[one], xt_scr.at[g], n_pages, page)


def _sattn_one(qn_ref, qd_ref, misc_ref, kslc_ref, kwin_ref, kdsa_ref, dsel_ref, win_ref, w1a_ref, w1b_ref, posa_ref,
               posb_ref, w2_ref, c2s_ref, exp_ref, nsa_pages, dsa_pages, o_ref, win_o_ref, xt_scr, n_pages, page):
    past = n_pages * page
    q_pos = past
    live = lax.broadcasted_iota(I32, (8, 1), 0) < N_HEADS
    qn, qd = _head_rows(qn_ref[0]), _head_rows(qd_ref[0])
    slope = _per_head(SLOPES)
    gates = _sigmoid(misc_ref[0, :, 0:3 * N_HEADS])
    gate = [_per_head([gates[:, 3 * h + c:3 * h + c + 1] for h in range(N_HEADS)]) for c in range(3)]

    def attend(q, k_t, v_t, mask, dist, new_pair, self_mask):
        sm = jnp.where(mask, _dot(q[:, 0:HEAD_DIM].astype(BF16), k_t) - slope * dist.astype(F32), NEG)
        s_self = jnp.where(self_mask, jnp.sum(q * new_pair, axis=1, keepdims=True), NEG)
        mx = jnp.maximum(s_self, jnp.max(sm, axis=1, keepdims=True))
        e = jnp.where(mask, jnp.exp(sm - mx), 0.0)
        e_self = jnp.where(self_mask, jnp.exp(s_self - mx), 0.0)
        l = e_self + jnp.sum(e, axis=1, keepdims=True)
        o = _dot_nt(e.astype(BF16), v_t) + e_self * new_pair[:, HEAD_DIM:2 * HEAD_DIM]
        return o / jnp.where(l > 0.0, l, 1.0)

    for p in range(n_pages):
        xt_scr[p * page:(p + 1) * page, :] = nsa_pages[p][0, 0, 0:2].reshape(2 * HEAD_DIM, page).T
    n16 = past // NSA_CMP_STRIDE
    x_all = jnp.concatenate([xt_scr[pl.ds(r, n16, stride=NSA_CMP_STRIDE), :] for r in range(NSA_CMP_STRIDE)], axis=1)
    a = _dot((x_all + posa_ref[...]).astype(BF16), w1a_ref[...])
    b = _dot((x_all + posb_ref[...]).astype(BF16), w1b_ref[...])
    hid = _gelu_tanh(a + pltpu.roll(b, n16 - 1, axis=0))
    comp = _dot(hid.astype(BF16), w2_ref[...]).astype(BF16)

    blk = lax.broadcasted_iota(I32, (1, n16), 1)
    d_cmp = q_pos - (blk * NSA_CMP_STRIDE + NSA_CMP_LEN - 1)
    m_cmp = (d_cmp >= 0) & live
    sm = jnp.where(m_cmp, _dot_nt(qn.astype(BF16), comp) - slope * d_cmp.astype(F32), NEG)
    e = jnp.where(m_cmp, jnp.exp(sm - jnp.max(sm, axis=1, keepdims=True)), 0.0)
    l = jnp.sum(e, axis=1, keepdims=True)
    pr = e / jnp.where(l > 0.0, l, 1.0)
    o_cmp = _dot(pr.astype(BF16), comp)[:, HEAD_DIM:2 * HEAD_DIM]
    psum = jnp.sum(pr, axis=0, keepdims=True)

    n_slc = past // NSA_SLC_LEN + 1
    imp = _hdot(jnp.broadcast_to(psum, (8, n16)), c2s_ref[...])[0:1]
    j = lax.broadcasted_iota(I32, (1, LANE), 1)
    cur = q_pos // NSA_SLC_LEN
    forced = (j == 0) | (j == cur) | (j == cur - 1)
    score = jnp.where((j * NSA_SLC_LEN <= q_pos) & (j < n_slc), imp + jnp.where(forced, FORCE_BONUS, 0.0), NEG)
    srow = jnp.broadcast_to(score, (LANE, LANE))
    scol = srow.T
    jp = lax.broadcasted_iota(I32, (LANE, LANE), 0)
    jj = lax.broadcasted_iota(I32, (LANE, LANE), 1)
    rank = jnp.sum(jnp.where((scol > srow) | ((scol == srow) & (jp < jj)), 1, 0), axis=0, keepdims=True)
    sel = jnp.where((rank < min(NSA_TOPN, n_slc)) & (score > 0.5 * NEG), 1.0, 0.0)
    sel_pos = _dot(jnp.broadcast_to(sel, (8, LANE)).astype(BF16), exp_ref[...])[0:1]

    cat = lambda refs, comp_i: jnp.concatenate([r[0, 0, comp_i] for r in refs], axis=1).astype(BF16)
    dist = q_pos - lax.broadcasted_iota(I32, (1, past), 1)
    m_slc = (sel_pos[:, 0:past] > 0.5) & live
    m_dsa = (dsel_ref[0, :, 0:past] > 0.5) & live
    self_slc = (sel_pos[:, past:past + 1] > 0.5) & live
    self_dsa = (dsel_ref[0, :, past:past + 1] > 0.5) & live
    wb = win_ref.shape[-1]
    d_win = q_pos - (past - wb + lax.broadcasted_iota(I32, (1, wb), 1))
    m_win = (d_win >= 0) & (d_win <= NSA_WINDOW) & (d_win <= q_pos) & live
    o_slc = attend(qn, cat(nsa_pages, 2), cat(nsa_pages, 3), m_slc, dist, kslc_ref[0], self_slc)
    o_win = attend(qn, win_ref[0, 0, 0].astype(BF16), win_ref[0, 0, 1].astype(BF16), m_win, d_win, kwin_ref[0], live)
    o_dsa = attend(qd, cat(dsa_pages, 0), cat(dsa_pages, 1), m_dsa, dist, kdsa_ref[0], self_dsa)
    o_nsa = gate[0] * o_cmp + gate[1] * o_slc + gate[2] * o_win
    o_ref[0] = jnp.concatenate([o_nsa, o_dsa], axis=1)

    new_cols = jnp.broadcast_to(kwin_ref[0], (LANE, LANE)).T
    lane_w = lax.broadcasted_iota(I32, (HEAD_DIM, wb), 1)
    for c in range(2):
        win_o_ref[0, c] = jnp.where(lane_w == wb - 1, new_cols[c * HEAD_DIM:(c + 1) * HEAD_DIM, 0:1],
                                    pltpu.roll(win_ref[0, 0, c], wb - 1, axis=1))


def _sample_attention(page_table, z, dsel, win_t, cmp_w, cache_n, cache_d, layer):
    b, n_pages = page_table.shape
    page = cache_n.shape[-1]
    past = n_pages * page
    wb = win_t.shape[-1]
    n16 = past // NSA_CMP_STRIDE
    w1a, w1b, pos_a, pos_b, w2 = cmp_w
    n_slc = past // NSA_SLC_LEN + 1
    c2s = np.zeros((n16, LANE), np.float32)
    c2s[:, :n_slc] = _cmp_to_slc_t(n16, n_slc).T
    expand = (np.arange(past + LANE)[None, :] // NSA_SLC_LEN == np.arange(LANE)[:, None]) & (np.arange(past + LANE)[None, :] <= past)
    grp = _pick(b, (SEQ_PER_STEP, 1))
    kern = functools.partial(_sattn_kernel, n_pages=n_pages, page=page, group=grp)
    full = lambda shape: pl.BlockSpec(shape, lambda bi, pt: (0,) * len(shape))
    nspec = lambda g, p: pl.BlockSpec((1, 1, 4, HEAD_DIM, page), lambda bi, pt: (layer, pt[grp * bi + g, p], 0, 0, 0))
    dspec = lambda g, p: pl.BlockSpec((1, 1, 2, HEAD_DIM, page), lambda bi, pt: (layer, pt[grp * bi + g, p], 0, 0, 0))
    pages = [(g, p) for g in range(grp) for p in range(n_pages)]
    qw = N_HEADS * LANE
    zblk = lambda width, col: pl.BlockSpec((grp, 1, width), lambda bi, pt: (bi, 0, col // width))
    gs = pltpu.PrefetchScalarGridSpec(
        num_scalar_prefetch=1, grid=(b // grp,),
        in_specs=[zblk(qw, C_QN), zblk(qw, C_QD), zblk(LANE, C_MISC), zblk(LANE, C_K + P_SLC * LANE),
                  zblk(LANE, C_K + P_WIN * LANE), zblk(LANE, C_K + P_DSA * LANE),
                  pl.BlockSpec((grp, 1, past + LANE), lambda bi, pt: (bi, 0, 0)),
                  pl.BlockSpec((1, grp, 2, HEAD_DIM, wb), lambda bi, pt: (layer, bi, 0, 0, 0)),
                  full(w1a.shape), full(w1b.shape), full(pos_a.shape), full(pos_b.shape), full(w2.shape),
                  full(c2s.shape), full(expand.shape)]
                 + [nspec(g, p) for g, p in pages] + [dspec(g, p) for g, p in pages],
        out_specs=[pl.BlockSpec((grp, 8, LANE), lambda bi, pt: (bi, 0, 0)),
                   pl.BlockSpec((grp, 2, HEAD_DIM, wb), lambda bi, pt: (bi, 0, 0, 0))],
        scratch_shapes=[pltpu.VMEM((grp, past, 2 * HEAD_DIM), F32)])
    return pl.pallas_call(
        kern, grid_spec=gs,
        out_shape=[jax.ShapeDtypeStruct((b, 8, LANE), F32), jax.ShapeDtypeStruct((b, 2, HEAD_DIM, wb), F32)],
        compiler_params=_cparams(("parallel",)), name="sample_attention",
    )(page_table, z, z, z, z, z, z, dsel, win_t, w1a, w1b, pos_a, pos_b, w2, jnp.asarray(c2s), jnp.asarray(expand, BF16),
      *([cache_n] * (grp * n_pages)), *([cache_d] * (grp * n_pages)))


def _smix_kernel(zg_ref, zr_ref, shift_ref, s_ref, lng_ref, lnb_ref, gw_ref, gb_ref, mu_ref, w0_ref, w2_ref, a0_ref,
                 a2_ref, g2_ref, kkw_ref, ka_ref, rk_ref, lnw_ref, lnb2_ref, oc_ref, vc_ref, or_ref, so_ref):
    bt = zg_ref.shape[0]
    zg = _gelu_tanh(zg_ref[...])
    v = _layer_norm(zg[:, GMLP_WIDTH:2 * GMLP_WIDTH], lng_ref[...], lnb_ref[...])
    vc_ref[...] = v
    oc_ref[...] = zg[:, 0:GMLP_WIDTH] * (v * gw_ref[...] + gb_ref[...])

    feat = zr_ref[...]
    f = feat + mu_ref[...] * (shift_ref[...] - feat)
    r, lw, k, vv, kk, be, g, bonus = _rwkv_features(f, w0_ref[...], w2_ref[...], a0_ref[...], a2_ref[...], g2_ref[...],
                                                    kkw_ref[...], ka_ref[...], rk_ref[...])
    w = jnp.exp(lw)
    lane_t = lax.broadcasted_iota(I32, (LANE, LANE), 1)
    y_cols = [jnp.zeros((LANE, LANE), F32) for _ in range(N_HEADS // 2)]
    for i in range(bt):
        for hp in range(N_HEADS // 2):
            v_colb = jnp.broadcast_to(vv[i:i + 1, hp * LANE:(hp + 1) * LANE], (LANE, LANE)).T
            ys = []
            for hh in range(2):
                h = 2 * hp + hh
                sl = slice(h * HEAD_DIM, (h + 1) * HEAD_DIM)
                s0 = s_ref[i, h]
                sa = -jnp.sum(s0 * kk[i:i + 1, sl], axis=1, keepdims=True)
                s1 = (s0 * w[i:i + 1, sl] + sa * be[i:i + 1, sl]
                      + v_colb[hh * HEAD_DIM:(hh + 1) * HEAD_DIM, 0:HEAD_DIM] * k[i:i + 1, sl])
                so_ref[i, h] = s1
                ys.append(jnp.sum(s1 * r[i:i + 1, sl], axis=1, keepdims=True))
            y_cols[hp] = jnp.where(lane_t == i, jnp.concatenate(ys, axis=0), y_cols[hp])
    y = jnp.concatenate([yc.T[0:bt] for yc in y_cols], axis=1)
    mu = _head_sum(y) * (1.0 / HEAD_DIM)
    yc = y - mu
    yn = yc * lax.rsqrt(_head_sum(yc * yc) * (1.0 / HEAD_DIM) + RWKV_GN_EPS)
    or_ref[...] = (yn * lnw_ref[...] + lnb2_ref[...] + bonus) * g


def _sample_mixers(z, shift_prev, wkv_prev, lp, bt):
    b = z.shape[0]
    wd = RWKV_WIDTH
    pw = 4 * wd
    gw = jnp.repeat(lp['gmlp_ws'][:, 0, 0], HEAD_DIM).reshape(1, GMLP_WIDTH)
    gb = jnp.repeat(lp['gmlp_bs'][:, 0], HEAD_DIM).reshape(1, GMLP_WIDTH)
    vec = lambda n: pl.BlockSpec((1, n), lambda i: (0, 0))
    mat = lambda r, c: pl.BlockSpec((r, c), lambda i: (0, 0))
    row = lambda n: pl.BlockSpec((bt, n), lambda i: (i, 0))
    sspec = pl.BlockSpec((bt, N_HEADS, HEAD_DIM, HEAD_DIM), lambda i: (i, 0, 0, 0))
    fl = jax.ShapeDtypeStruct((b, wd), F32)
    return pl.pallas_call(
        _smix_kernel,
        grid=(b // bt,),
        in_specs=[pl.BlockSpec((bt, 2 * GMLP_WIDTH), lambda i: (i, C_GMLP // (2 * GMLP_WIDTH))),
                  pl.BlockSpec((bt, pw), lambda i: (i, C_RWKV // pw)),
                  row(pw), sspec, vec(wd), vec(wd), vec(wd), vec(wd),
                  vec(pw), vec(wd), mat(64, wd), vec(wd), mat(64, wd), mat(128, wd), vec(wd), vec(wd), vec(wd), vec(wd), vec(wd)],
        out_specs=[row(wd), row(wd), row(wd), sspec],
        out_shape=[fl, fl, fl, jax.ShapeDtypeStruct(wkv_prev.shape, F32)],
        compiler_params=_cparams(("parallel",)),
        name="sample_mixers",
    )(z, z, shift_prev, wkv_prev, lp['gmlp_ln_g'].reshape(1, wd), lp['gmlp_ln_b'].reshape(1, wd), gw, gb,
      lp['rwkv_mu'].reshape(1, pw), lp['rwkv_w0'].reshape(1, wd), lp['rwkv_w2'].astype(BF16),
      lp['rwkv_a0'].reshape(1, wd), lp['rwkv_a2'].astype(BF16), lp['rwkv_g2'].astype(BF16),
      lp['rwkv_kk'].reshape(1, wd), lp['rwkv_ka'].reshape(1, wd), lp['rwkv_rk'].reshape(1, wd),
      lp['rwkv_ln_w'].reshape(1, wd), lp['rwkv_ln_b'].reshape(1, wd))


_W_IN_COLS = (('q_nsa', 256), ('kv_nsa', 384), ('g_nsa', 12), ('q_dsa', 256), ('kv_dsa', 128), ('q_idx', 256),
              ('k_idx', 64), ('w_idx', 4), ('gmlp', 512), ('rwkv', 1024), ('merge', 4096))


def _proj_weights(w_in):
    d = w_in.shape[0]
    parts, off = {}, 0
    for name, width in _W_IN_COLS:
        parts[name] = w_in[:, off:off + width]
        off += width

    def pad_heads(w):
        w4 = w.reshape(d, N_HEADS, HEAD_DIM)
        return jnp.concatenate([w4, jnp.zeros_like(w4)], axis=-1).reshape(d, N_HEADS * LANE)

    zeros = lambda n: jnp.zeros((d, n), w_in.dtype)
    w = jnp.concatenate([parts['merge'], parts['rwkv'], parts['gmlp'], pad_heads(parts['q_nsa']), pad_heads(parts['q_dsa']),
                         parts['g_nsa'], zeros(LANE - 12), parts['kv_nsa'], parts['kv_dsa']], axis=1)
    w_idx = jnp.concatenate([pad_heads(parts['q_idx']), parts['k_idx'], zeros(LANE - HEAD_DIM),
                             parts['w_idx'], zeros(LANE - N_HEADS)], axis=1)
    return w.astype(BF16), _split_bf16(w_idx)


def _compress_weights_rows(w1, w2, pos):
    half = NSA_CMP_STRIDE * HEAD_DIM
    return (w1[:, :half].astype(BF16), w1[:, half:].astype(BF16),
            pos[:, :NSA_CMP_STRIDE].reshape(2, 1, half), pos[:, NSA_CMP_STRIDE:].reshape(2, 1, half), w2.astype(BF16))


def _compress_weights_pairs(w1, w2, pos):
    hid = w1.shape[-1]
    w1r = w1.reshape(2, 2, NSA_CMP_STRIDE, HEAD_DIM, hid)
    posr = pos.reshape(2, 2, NSA_CMP_STRIDE, HEAD_DIM)
    bigs, poss = [], []
    for half in range(2):
        big = jnp.zeros((NSA_CMP_STRIDE, 2, HEAD_DIM, 2, hid), w1.dtype)
        for c in range(2):
            big = big.at[:, c, :, c, :].set(w1r[c, half])
        bigs.append(big.reshape(NSA_CMP_STRIDE * 2 * HEAD_DIM, 2 * hid).astype(BF16))
        poss.append(jnp.transpose(posr[:, half], (1, 0, 2)).reshape(1, NSA_CMP_STRIDE * 2 * HEAD_DIM))
    w2b = jnp.zeros((2, hid, 2, HEAD_DIM), w2.dtype)
    for c in range(2):
        w2b = w2b.at[c, :, c, :].set(w2[c])
    return bigs[0], bigs[1], poss[0], poss[1], w2b.reshape(2 * hid, 2 * HEAD_DIM).astype(BF16)


def _pick(n, cands):
    for c in cands:
        if n % c == 0:
            return c
    return n


def kernel(x_prompt, x_sample, cache_nsa, cache_dsa, state_nsa_win, state_rwkv_shift, state_rwkv_wkv, page_table,
           c_prompt, c_sample, w_ada, b_ada, norm_mix_g, norm_ffn_g, w_in, nsa_cmp_w1, nsa_cmp_w2, nsa_cmp_pos,
           gmlp_ln_g, gmlp_ln_b, gmlp_ws, gmlp_bs, rwkv_mu, rwkv_w0, rwkv_w2, rwkv_a0, rwkv_a2, rwkv_g2, rwkv_kk,
           rwkv_ka, rwkv_rk, rwkv_ln_w, rwkv_ln_b, w_branch, w_out, w_ffn_in, w_ffn_out, final_norm_g):
    depth = w_ada.shape[0]
    bp, t, d = x_prompt.shape
    bs = x_sample.shape[0]
    assert x_sample.shape[1] == 1 and t % CK == 0 and t % (8 * NSA_CMP_STRIDE) == 0
    cache_n = jnp.transpose(cache_nsa, (0, 1, 3, 4, 2))
    cache_d = jnp.transpose(cache_dsa, (0, 1, 3, 4, 2))
    win_t = jnp.transpose(state_nsa_win, (0, 1, 3, 4, 2))
    past = page_table.shape[1] * cache_nsa.shape[2]
    c_all = jnp.concatenate([c_prompt, c_sample], axis=0)
    xp, xs = x_prompt, x_sample.reshape(1, bs, d)
    tm_p = _pick(t, (512, 256, 128))
    tm_f = _pick(t, (1024, 512, 256, 128))
    tf = _pick(w_ffn_out.shape[1], (256, 128))
    tn = _pick(N_PROJ, (2432,))
    outs = {k: [] for k in ('rows_n_p', 'rows_n_s', 'rows_d_p', 'rows_d_s', 'win_p', 'win_s', 'v_s', 'shift_p', 'shift_s',
                            'wkv_p', 'wkv_s')}
    for l in range(depth):
        lp = {'gmlp_ln_g': gmlp_ln_g[l], 'gmlp_ln_b': gmlp_ln_b[l], 'gmlp_ws': gmlp_ws[l], 'gmlp_bs': gmlp_bs[l],
              'rwkv_mu': rwkv_mu[l], 'rwkv_w0': rwkv_w0[l], 'rwkv_w2': rwkv_w2[l], 'rwkv_a0': rwkv_a0[l],
              'rwkv_a2': rwkv_a2[l], 'rwkv_g2': rwkv_g2[l], 'rwkv_kk': rwkv_kk[l], 'rwkv_ka': rwkv_ka[l],
              'rwkv_rk': rwkv_rk[l], 'rwkv_ln_w': rwkv_ln_w[l], 'rwkv_ln_b': rwkv_ln_b[l]}
        last = l == depth - 1
        mod = _ada(c_all, w_ada[l], b_ada[l]).reshape(bp + bs, 6, d)
        mod_p = [mod[:bp, i:i + 1] for i in range(6)]
        mod_s = [mod[bp:, i][None] for i in range(6)]
        w_proj, (wi_hi, wi_lo) = _proj_weights(w_in[l])
        wb, wo = w_branch[l].astype(BF16), w_out[l].astype(BF16)
        wfi, wfo = w_ffn_in[l].astype(BF16), w_ffn_out[l].astype(BF16)

        zp = _inproj(xp, norm_mix_g[l], mod_p[0], mod_p[1], w_proj, tm_p, tn)
        zi = _inproj_hp(xp, norm_mix_g[l], mod_p[0], mod_p[1], wi_hi, wi_lo, tm_p)
        outs['rows_n_p'].append(zp[..., C_K:C_K + 4 * HEAD_DIM].reshape(bp, t, 4, HEAD_DIM))
        outs['rows_d_p'].append(jnp.concatenate([zp[..., C_K + 3 * LANE:C_K + 4 * LANE], zi[..., ZI_K:ZI_K + HEAD_DIM]],
                                                axis=-1).reshape(bp, t, 3, HEAD_DIM))
        wn = min(NSA_WINDOW, t)
        outs['win_p'].append(zp[:, t - wn:, C_K + 2 * LANE:C_K + 3 * LANE].reshape(bp, wn, 2, HEAD_DIM))
        outs['shift_p'].append(zp[:, t - 1, C_RWKV:C_RWKV + 4 * RWKV_WIDTH])
        n16 = t // NSA_CMP_STRIDE
        r = jnp.stack([zp[..., C_K:C_K + HEAD_DIM].reshape(bp, n16, NSA_CMP_STRIDE * HEAD_DIM),
                       zp[..., C_K + HEAD_DIM:C_K + 2 * HEAD_DIM].reshape(bp, n16, NSA_CMP_STRIDE * HEAD_DIM)], axis=1)
        w1a, w1b, pos_a, pos_b, w2 = _compress_weights_rows(nsa_cmp_w1[l], nsa_cmp_w2[l], nsa_cmp_pos[l])
        cb, ct = _nsa_compress(r, pos_a, pos_b, w1a, w1b, w2)
        kb, kt = _kprep(zp)
        o_att = _prompt_attention(zp, zi, kb, kt, _kprep_idx(zi), cb, ct)
        o_c = _gmlp(zp, lp['gmlp_ln_g'], lp['gmlp_ln_b'], lp['gmlp_ws'], lp['gmlp_bs'], _pick(t, (512, 256, 128)))
        pre = _rwkv_pre(zp, jnp.zeros((bp, 1, 4 * RWKV_WIDTH), F32), lp, _pick(t, (256, 128)))
        o_r, wkv_bd = _rwkv_scan(*pre, lp['rwkv_ln_w'], lp['rwkv_ln_b'], jnp.zeros((bp, RWKV_WIDTH, RWKV_WIDTH), F32))
        outs['wkv_p'].append(jnp.stack([wkv_bd[:, h * HEAD_DIM:(h + 1) * HEAD_DIM, h * HEAD_DIM:(h + 1) * HEAD_DIM]
                                        for h in range(N_HEADS)], axis=1).swapaxes(-1, -2))
        xp = _merge(zp, o_att, o_c, o_r, wb, wo, xp, mod_p[2], tm_p)
        xp = _ffn(xp, norm_ffn_g[l], mod_p[3], mod_p[4], mod_p[5], wfi, wfo, final_norm_g, last, tm_f, tf)

        zs = _inproj(xs, norm_mix_g[l], mod_s[0], mod_s[1], w_proj, bs, tn)
        z2 = zs[0]
        zi2 = _inproj_hp(xs, norm_mix_g[l], mod_s[0], mod_s[1], wi_hi, wi_lo, bs)[0]
        k_idx_new = zi2[:, ZI_K:ZI_K + HEAD_DIM]
        outs['rows_n_s'].append(z2[:, C_K:C_K + 4 * HEAD_DIM].reshape(bs, 1, 4, HEAD_DIM))
        outs['rows_d_s'].append(jnp.concatenate([z2[:, C_K + 3 * LANE:C_K + 4 * LANE], k_idx_new],
                                                axis=-1).reshape(bs, 1, 3, HEAD_DIM))
        outs['shift_s'].append(z2[:, C_RWKV:C_RWKV + 4 * RWKV_WIDTH])
        idx = _sample_index(page_table, zi2.reshape(bs, 1, N_HP), cache_d, l)
        width = idx.shape[-1]
        dsel = _sample_topk(idx.reshape(bs, width), min(DSA_TOPK, (past + 1) // 4)).reshape(bs, 1, width)
        cmp_w = _compress_weights_pairs(nsa_cmp_w1[l], nsa_cmp_w2[l], nsa_cmp_pos[l])
        o_rows, win_new = _sample_attention(page_table, zs.reshape(bs, 1, N_PROJ), dsel, win_t, cmp_w, cache_n, cache_d, l)
        o_heads = jnp.concatenate([o_rows[:, 0:N_HEADS, 0:HEAD_DIM].reshape(bs, N_HEADS * HEAD_DIM),
                                   o_rows[:, 0:N_HEADS, HEAD_DIM:2 * HEAD_DIM].reshape(bs, N_HEADS * HEAD_DIM)], axis=1)
        outs['win_s'].append(jnp.transpose(win_new, (0, 3, 1, 2)))
        o_cs, v_cs, o_rs, wkv_s = _sample_mixers(z2, state_rwkv_shift[l], state_rwkv_wkv[l], lp, 8)
        outs['v_s'].append(v_cs.reshape(bs, 1, GMLP_WIDTH))
        outs['wkv_s'].append(wkv_s)
        xs = _merge(zs, o_heads[None], o_cs[None], o_rs[None], wb, wo, xs, mod_s[2], bs)
        xs = _ffn(xs, norm_ffn_g[l], mod_s[3], mod_s[4], mod_s[5], wfi, wfo, final_norm_g, last, bs, tf)

    st = lambda k: jnp.stack(outs[k])
    return (xp, xs.reshape(bs, 1, d), st('rows_n_p'), st('rows_n_s'), st('rows_d_p'), st('rows_d_s'), st('win_p'),
            st('win_s'), st('v_s'), st('shift_p'), st('shift_s'), st('wkv_p'), st('wkv_s'))
```

```python
import functools

import numpy as np
import jax
import jax.numpy as jnp
from jax import lax
from jax.experimental import pallas as pl
from jax.experimental.pallas import tpu as pltpu

F32 = jnp.float32
BF16 = jnp.bfloat16
I32 = jnp.int32

HEAD_DIM = 64
N_HEADS = 4
NSA_CMP_LEN = 32
NSA_CMP_STRIDE = 16
NSA_SLC_LEN = 64
NSA_TOPN = 8
NSA_WINDOW = 512
DSA_TOPK = 256
GMLP_GROUPS = 4
GMLP_WIDTH = GMLP_GROUPS * HEAD_DIM
CHUNK = 128
RWKV_WIDTH = N_HEADS * HEAD_DIM
RWKV_GN_EPS = 64e-5
QB = 128
EPS = 1e-6
LN_EPS = 1e-5
NEG = -1e30
FORCE_BONUS = 1e4
LANE = 128
VMEM_LIMIT = 56 * 1024 * 1024

C_MERGE, C_RWKV, C_GMLP = 0, 4096, 5120
C_QN, C_QD, C_MISC, C_K = 5632, 6144, 6656, 6784
N_PROJ = 7296
K_PAIRS = 4
P_CMP, P_SLC, P_WIN, P_DSA = range(K_PAIRS)
ZI_Q, ZI_K, ZI_W, N_HP = 0, 512, 640, 768


def _cparams(sem):
    return pltpu.CompilerParams(dimension_semantics=sem, vmem_limit_bytes=VMEM_LIMIT)


def _dot(a, b):
    return jnp.dot(a, b, preferred_element_type=F32)


def _dot_nt(a, b):
    return lax.dot_general(a, b, (((1,), (1,)), ((), ())), preferred_element_type=F32)


def _dot_tn(a, b):
    return lax.dot_general(a, b, (((0,), (0,)), ((), ())), preferred_element_type=F32)


def _gelu_tanh(x):
    return 0.5 * x * (1.0 + jnp.tanh(np.sqrt(2.0 / np.pi).astype(np.float32) * (x + 0.044715 * (x * x * x))))


def _sigmoid(x):
    return 0.5 * jnp.tanh(0.5 * x) + 0.5


def _rms_mod(x, g, scale, shift):
    ms = jnp.mean(x * x, axis=-1, keepdims=True)
    return (x * lax.rsqrt(ms + EPS) * g) * (1.0 + scale) + shift


def _ada_kernel(c_ref, w_ref, b_ref, o_ref):
    c = c_ref[...]
    s = (c * _sigmoid(c)).astype(BF16)
    o_ref[...] = _dot(s, w_ref[...].astype(BF16)) + b_ref[...]


def _ada(c, w, b):
    m, d = c.shape
    n = w.shape[1]
    tn = 1536
    return pl.pallas_call(
        _ada_kernel,
        grid=(n // tn,),
        in_specs=[pl.BlockSpec((m, d), lambda j: (0, 0)),
                  pl.BlockSpec((d, tn), lambda j: (0, j)),
                  pl.BlockSpec((1, tn), lambda j: (0, j))],
        out_specs=pl.BlockSpec((m, tn), lambda j: (0, j)),
        out_shape=jax.ShapeDtypeStruct((m, n), F32),
        compiler_params=_cparams(("parallel",)),
        name="ada",
    )(c, w, b.reshape(1, n))


def _inproj_kernel(x_ref, g_ref, sh_ref, sc_ref, w_ref, o_ref):
    h = _rms_mod(x_ref[0], g_ref[...], sc_ref[0], sh_ref[0])
    o_ref[0] = _dot(h.astype(BF16), w_ref[...])


def _inproj(x, g, shift, scale, w, tm, tn):
    b, t, d = x.shape
    n = w.shape[1]
    tmod = tm if shift.shape[1] == t else 1
    mod_map = (lambda j, bi, i: (bi, i, 0)) if shift.shape[1] == t else (lambda j, bi, i: (bi, 0, 0))
    return pl.pallas_call(
        _inproj_kernel,
        grid=(n // tn, b, t // tm),
        in_specs=[pl.BlockSpec((1, tm, d), lambda j, bi, i: (bi, i, 0)),
                  pl.BlockSpec((1, d), lambda j, bi, i: (0, 0)),
                  pl.BlockSpec((1, tmod, d), mod_map),
                  pl.BlockSpec((1, tmod, d), mod_map),
                  pl.BlockSpec((d, tn), lambda j, bi, i: (0, j))],
        out_specs=pl.BlockSpec((1, tm, tn), lambda j, bi, i: (bi, i, j)),
        out_shape=jax.ShapeDtypeStruct((b, t, n), F32),
        compiler_params=_cparams(("parallel", "parallel", "parallel")),
        name="inproj",
    )(x, g.reshape(1, d), shift, scale, w)


def _split_bf16(x):
    hi = x.astype(BF16)
    return hi, (x - hi.astype(F32)).astype(BF16)


def _inproj_hp_kernel(x_ref, g_ref, sh_ref, sc_ref, wh_ref, wl_ref, o_ref):
    h_hi, h_lo = _split_bf16(_rms_mod(x_ref[0], g_ref[...], sc_ref[0], sh_ref[0]))
    o_ref[0] = _dot(h_hi, wh_ref[...]) + (_dot(h_hi, wl_ref[...]) + _dot(h_lo, wh_ref[...]))


def _inproj_hp(x, g, shift, scale, w_hi, w_lo, tm):
    b, t, d = x.shape
    n = w_hi.shape[1]
    tmod = tm if shift.shape[1] == t else 1
    mod_map = (lambda bi, i: (bi, i, 0)) if shift.shape[1] == t else (lambda bi, i: (bi, 0, 0))
    return pl.pallas_call(
        _inproj_hp_kernel,
        grid=(b, t // tm),
        in_specs=[pl.BlockSpec((1, tm, d), lambda bi, i: (bi, i, 0)),
                  pl.BlockSpec((1, d), lambda bi, i: (0, 0)),
                  pl.BlockSpec((1, tmod, d), mod_map),
                  pl.BlockSpec((1, tmod, d), mod_map),
                  pl.BlockSpec((d, n), lambda bi, i: (0, 0)),
                  pl.BlockSpec((d, n), lambda bi, i: (0, 0))],
        out_specs=pl.BlockSpec((1, tm, n), lambda bi, i: (bi, i, 0)),
        out_shape=jax.ShapeDtypeStruct((b, t, n), F32),
        compiler_params=_cparams(("parallel", "parallel")),
        name="inproj_hp",
    )(x, g.reshape(1, d), shift, scale, w_hi, w_lo)


def _merge_kernel(zm_ref, oa_ref, oc_ref, or_ref, wb_ref, wo_ref, x_ref, gate_ref, o_ref):
    bw = wb_ref.shape[1]
    d = x_ref.shape[-1]
    outs = (oa_ref[0, :, 0:bw], oa_ref[0, :, bw:2 * bw], oc_ref[0], or_ref[0])
    mixed = None
    for n, o in enumerate(outs):
        br = _dot(o.astype(BF16), wb_ref[n])
        term = _sigmoid(zm_ref[0, :, n * d:(n + 1) * d]) * br
        mixed = term if mixed is None else mixed + term
    y = _dot(mixed.astype(BF16), wo_ref[...])
    o_ref[0] = x_ref[0] + gate_ref[0] * y


def _merge(z, o_att, o_c, o_r, w_branch, w_out, x, gate, tm):
    b, t, d = x.shape
    nb, bw, _ = w_branch.shape
    tmod = tm if gate.shape[1] == t else 1
    mod_map = (lambda bi, i: (bi, i, 0)) if gate.shape[1] == t else (lambda bi, i: (bi, 0, 0))
    return pl.pallas_call(
        _merge_kernel,
        grid=(b, t // tm),
        in_specs=[pl.BlockSpec((1, tm, nb * d), lambda bi, i: (bi, i, C_MERGE // (nb * d))),
                  pl.BlockSpec((1, tm, 2 * bw), lambda bi, i: (bi, i, 0)),
                  pl.BlockSpec((1, tm, bw), lambda bi, i: (bi, i, 0)),
                  pl.BlockSpec((1, tm, bw), lambda bi, i: (bi, i, 0)),
                  pl.BlockSpec((nb, bw, d), lambda bi, i: (0, 0, 0)),
                  pl.BlockSpec((d, d), lambda bi, i: (0, 0)),
                  pl.BlockSpec((1, tm, d), lambda bi, i: (bi, i, 0)),
                  pl.BlockSpec((1, tmod, d), mod_map)],
        out_specs=pl.BlockSpec((1, tm, d), lambda bi, i: (bi, i, 0)),
        out_shape=jax.ShapeDtypeStruct((b, t, d), F32),
        compiler_params=_cparams(("parallel", "parallel")),
        name="merge",
    )(z, o_att, o_c, o_r, w_branch, w_out, x, gate)


def _ffn_kernel(x_ref, g_ref, sh_ref, sc_ref, gate_ref, wg_ref, wu_ref, wd_ref, fg_ref, o_ref,
                h_scr, acc_scr, *, final_norm):
    k = pl.program_id(2)

    @pl.when(k == 0)
    def _():
        h_scr[...] = _rms_mod(x_ref[0], g_ref[...], sc_ref[0], sh_ref[0]).astype(BF16)
        acc_scr[...] = jnp.zeros_like(acc_scr)

    h = h_scr[...]
    gt = _dot(h, wg_ref[...])
    up = _dot(h, wu_ref[...])
    act = (gt * _sigmoid(gt)) * up
    acc_scr[...] += _dot(act.astype(BF16), wd_ref[...])

    @pl.when(k == pl.num_programs(2) - 1)
    def _():
        y = x_ref[0] + gate_ref[0] * acc_scr[...]
        if final_norm:
            ms = jnp.mean(y * y, axis=-1, keepdims=True)
            y = y * lax.rsqrt(ms + EPS) * fg_ref[...]
        o_ref[0] = y


def _ffn(x, g, shift, scale, gate, w_in, w_out, final_g, final_norm, tm, tf):
    b, t, d = x.shape
    ff = w_out.shape[0]
    nk = ff // tf
    tmod = tm if gate.shape[1] == t else 1
    mod_map = (lambda bi, i, k: (bi, i, 0)) if gate.shape[1] == t else (lambda bi, i, k: (bi, 0, 0))
    return pl.pallas_call(
        functools.partial(_ffn_kernel, final_norm=final_norm),
        grid=(b, t // tm, nk),
        in_specs=[pl.BlockSpec((1, tm, d), lambda bi, i, k: (bi, i, 0)),
                  pl.BlockSpec((1, d), lambda bi, i, k: (0, 0)),
                  pl.BlockSpec((1, tmod, d), mod_map),
                  pl.BlockSpec((1, tmod, d), mod_map),
                  pl.BlockSpec((1, tmod, d), mod_map),
                  pl.BlockSpec((d, tf), lambda bi, i, k: (0, k)),
                  pl.BlockSpec((d, tf), lambda bi, i, k: (0, nk + k)),
                  pl.BlockSpec((tf, d), lambda bi, i, k: (k, 0)),
                  pl.BlockSpec((1, d), lambda bi, i, k: (0, 0))],
        out_specs=pl.BlockSpec((1, tm, d), lambda bi, i, k: (bi, i, 0)),
        out_shape=jax.ShapeDtypeStruct((b, t, d), F32),
        scratch_shapes=[pltpu.VMEM((tm, d), BF16), pltpu.VMEM((tm, d), F32)],
        compiler_params=_cparams(("parallel", "parallel", "arbitrary")),
        name="ffn",
    )(x, g.reshape(1, d), shift, scale, gate, w_in, w_in, w_out, final_g.reshape(1, d))


def _cmp_kernel(r_ref, pa_ref, pb_ref, w1a_ref, w1b_ref, w2_ref, ob_ref, ot_ref):
    n16 = r_ref.shape[2]
    comp = []
    for z in range(2):
        r = r_ref[0, z]
        a = _dot((r + pa_ref[z]).astype(BF16), w1a_ref[z])
        b = _dot((r + pb_ref[z]).astype(BF16), w1b_ref[z])
        hid = _gelu_tanh(a + pltpu.roll(b, n16 - 1, axis=0))
        comp.append(_dot(hid.astype(BF16), w2_ref[z]))
    pair = jnp.concatenate(comp, axis=1)
    ob_ref[0] = pair.astype(BF16)
    ot_ref[0] = pair.T.astype(BF16)


def _nsa_compress(r, pos_a, pos_b, w1a, w1b, w2):
    b, _, n16, kd = r.shape
    hid = w1a.shape[-1]
    return pl.pallas_call(
        _cmp_kernel,
        grid=(b,),
        in_specs=[pl.BlockSpec((1, 2, n16, kd), lambda bi: (bi, 0, 0, 0)),
                  pl.BlockSpec((2, 1, kd), lambda bi: (0, 0, 0)),
                  pl.BlockSpec((2, 1, kd), lambda bi: (0, 0, 0)),
                  pl.BlockSpec((2, kd, hid), lambda bi: (0, 0, 0)),
                  pl.BlockSpec((2, kd, hid), lambda bi: (0, 0, 0)),
                  pl.BlockSpec((2, hid, HEAD_DIM), lambda bi: (0, 0, 0))],
        out_specs=[pl.BlockSpec((1, n16, 2 * HEAD_DIM), lambda bi: (bi, 0, 0)),
                   pl.BlockSpec((1, 2 * HEAD_DIM, n16), lambda bi: (bi, 0, 0))],
        out_shape=[jax.ShapeDtypeStruct((b, n16, 2 * HEAD_DIM), BF16),
                   jax.ShapeDtypeStruct((b, 2 * HEAD_DIM, n16), BF16)],
        compiler_params=_cparams(("parallel",)),
        name="nsa_compress",
    )(r, pos_a, pos_b, w1a, w1b, w2)


CK = 256
AUX_POS = 32


def _attn_aux(t):
    pos = np.arange(t)
    aux = np.zeros((t, LANE), np.float32)
    aux[pos, pos // NSA_SLC_LEN] = 1.0
    aux[:, AUX_POS] = pos % CK
    aux[:, AUX_POS + 1] = pos // CK
    return jnp.asarray(aux, BF16)


def _kprep_kernel(z_ref, aux_ref, ob_ref, ot_ref):
    x = z_ref[0]
    ob_ref[0, 0] = jnp.concatenate([x.astype(BF16), aux_ref[...]], axis=1)
    for c in range(x.shape[0] // CK):
        ot_ref[0, 0, c] = x[c * CK:(c + 1) * CK].T[HEAD_DIM:2 * HEAD_DIM].astype(BF16)


def _kprep(z):
    b, t, _ = z.shape
    tk = _pick(t, (1024, 512, CK))
    return pl.pallas_call(
        _kprep_kernel,
        grid=(b, K_PAIRS, t // tk),
        in_specs=[pl.BlockSpec((1, tk, LANE), lambda bi, p, c: (bi, c, C_K // LANE + p)),
                  pl.BlockSpec((tk, LANE), lambda bi, p, c: (c, 0))],
        out_specs=[pl.BlockSpec((1, 1, tk, 2 * LANE), lambda bi, p, c: (bi, p, c, 0)),
                   pl.BlockSpec((1, 1, tk // CK, HEAD_DIM, CK), lambda bi, p, c: (bi, p, c, 0, 0))],
        out_shape=[jax.ShapeDtypeStruct((b, K_PAIRS, t, 2 * LANE), BF16),
                   jax.ShapeDtypeStruct((b, K_PAIRS, t // CK, HEAD_DIM, CK), BF16)],
        compiler_params=_cparams(("parallel", "parallel", "parallel")),
        name="kprep",
    )(z, _attn_aux(t))


def _kprep_idx_kernel(z_ref, o_ref):
    x = z_ref[0]
    hi = x.astype(BF16).astype(F32)
    o_ref[0] = jnp.concatenate([hi + pltpu.roll(x - hi, HEAD_DIM, axis=1), hi], axis=1).astype(BF16)


def _kprep_idx(zi):
    b, t, _ = zi.shape
    tt = _pick(t, (512, 256, 128))
    return pl.pallas_call(
        _kprep_idx_kernel,
        grid=(b, t // tt),
        in_specs=[pl.BlockSpec((1, tt, LANE), lambda bi, i: (bi, i, ZI_K // LANE))],
        out_specs=pl.BlockSpec((1, tt, 2 * LANE), lambda bi, i: (bi, i, 0)),
        out_shape=jax.ShapeDtypeStruct((b, t, 2 * LANE), BF16),
        compiler_params=_cparams(("parallel", "parallel")),
        name="kprep_idx",
    )(zi)


INT_MIN = -2 ** 31


def _sortable_key(v):
    v = jnp.where(v == 0.0, 0.0, v)
    u = lax.bitcast_convert_type(v, I32)
    return jnp.where(u < 0, u ^ 0x7FFFFFFF, u)


_KEY_HALF_NEG = int(np.array(0.5 * NEG, np.float32).view(np.int32) ^ 0x7FFFFFFF)


def _lane_consts(q0):
    lane = lax.broadcasted_iota(I32, (1, N_HEADS * QB), 1)
    hl = lane >> 7
    q_pos = q0 + (lane & (QB - 1))
    slope = jnp.where(hl == 0, 2.0 ** -2, jnp.where(hl == 1, 2.0 ** -4, jnp.where(hl == 2, 2.0 ** -6, 2.0 ** -8)))
    return q_pos, slope.astype(F32)


def _tile_heads(x):
    return jnp.concatenate([x] * N_HEADS, axis=1)


def _place_heads(tiles):
    lane = lax.broadcasted_iota(I32, (QB, LANE), 1)
    out = []
    for t in range(N_HEADS // 2):
        out.append(jnp.where(lane < HEAD_DIM, pltpu.roll(tiles[2 * t], HEAD_DIM, axis=1), tiles[2 * t + 1]))
    return jnp.concatenate(out, axis=1)


def _attn_kernel(qn_ref, qd_ref, misc_ref, qi_ref, wi_ref, kb_ref, kt_ref, ki_ref, cb_ref, ct_ref, c2s_ref, o_ref,
                 key_scr, dsel_scr, *, seq_len, topk):
    t = seq_len
    q0 = pl.program_id(1) * QB
    nc = (q0 + QB + CK - 1) // CK
    hq = N_HEADS * QB
    scale = HEAD_DIM ** -0.5

    def stack_q(ref):
        return jnp.concatenate([ref[0, :, h * LANE:(h + 1) * LANE] for h in range(N_HEADS)], axis=0) * scale

    qn, qd = stack_q(qn_ref).astype(BF16), stack_q(qd_ref).astype(BF16)
    qi = stack_q(qi_ref)
    qi_cat = jnp.concatenate([(qi + pltpu.roll(qi, HEAD_DIM, axis=1)).astype(BF16),
                              (qi - qi.astype(BF16).astype(F32)).astype(BF16)], axis=1)
    misc_t = misc_ref[0].T
    wi_t = wi_ref[0].T
    q_pos, slope = _lane_consts(q0)

    n16 = cb_ref.shape[1]
    sc = _dot_nt(cb_ref[0], qn)
    cmp_end = lax.broadcasted_iota(I32, (n16, hq), 0) * NSA_CMP_STRIDE + (NSA_CMP_LEN - 1)
    d = q_pos - cmp_end
    mask = d >= 0
    sm = jnp.where(mask, sc - slope * d.astype(F32), NEG)
    e = jnp.where(mask, jnp.exp(sm - jnp.max(sm, axis=0, keepdims=True)), 0.0)
    l = jnp.sum(e, axis=0, keepdims=True)
    p = e / jnp.where(l > 0.0, l, 1.0)
    o_cmp_t = _dot(ct_ref[0], p.astype(BF16))
    psum_t = p[:, 0:QB]
    for h in range(1, N_HEADS):
        psum_t = psum_t + p[:, h * QB:(h + 1) * QB]
    imp_t = jnp.dot(c2s_ref[...], psum_t, preferred_element_type=F32, precision=lax.Precision.HIGHEST)

    nslc = c2s_ref.shape[0]
    jrow = lax.broadcasted_iota(I32, (nslc, QB), 0)
    qp = q0 + lax.broadcasted_iota(I32, (nslc, QB), 1)
    cur = qp >> 6
    adm = jrow * NSA_SLC_LEN <= qp
    forced = (jrow == 0) | (jrow == cur) | (jrow == cur - 1)
    score = jnp.where(adm, imp_t + jnp.where(forced, FORCE_BONUS, 0.0), NEG)
    rank = jnp.zeros((nslc, QB), I32)
    for j in range(nslc):
        row = score[j:j + 1, :]
        rank = rank + jnp.where((row > score) | ((row == score) & (jrow > j)), 1, 0)
    sel_bias = jnp.where((rank < min(NSA_TOPN, nslc)) & (score > 0.5 * NEG), 0.0, NEG)

    w_rows = [wi_t[h:h + 1, :] * (N_HEADS ** -0.5) for h in range(N_HEADS)]

    tri = jnp.where(lax.broadcasted_iota(I32, (CK, CK), 1) < lax.broadcasted_iota(I32, (CK, CK), 0), 1.0, 0.0).astype(BF16)

    def dsa_select(n_spans):
        span = CK
        for c in range(n_spans):
            lg = jnp.maximum(_dot_nt(ki_ref[0, c * span:(c + 1) * span, :], qi_cat), 0.0)
            idx = lg[:, 0:QB] * w_rows[0]
            for h in range(1, N_HEADS):
                idx = idx + lg[:, h * QB:(h + 1) * QB] * w_rows[h]
            kpos = c * span + lax.broadcasted_iota(I32, (span, QB), 0)
            causal = kpos <= q0 + lax.broadcasted_iota(I32, (span, QB), 1)
            key_scr[c * span:(c + 1) * span, :] = _sortable_key(jnp.where(causal, idx, NEG))

        if n_spans * span <= topk:
            for c in range(n_spans):
                dsel_scr[c * CK:(c + 1) * CK, :] = jnp.where(key_scr[c * CK:(c + 1) * CK, :] > _KEY_HALF_NEG, 0.0, NEG)
            return jnp.int32(0)

        def count(pred_fn):
            acc = jnp.zeros((1, QB), I32)
            for c in range(n_spans):
                acc = acc + jnp.sum(jnp.where(pred_fn(key_scr[c * span:(c + 1) * span, :]), 1, 0), axis=0, keepdims=True)
            return acc

        def bit_body(i, ans):
            cand = ans + lax.shift_left(jnp.int32(1), 31 - i)
            return jnp.where(count(lambda k: k >= cand) >= topk, cand, ans)

        thr = lax.fori_loop(0, 32, bit_body, jnp.full((1, QB), INT_MIN, I32))
        need = (topk - count(lambda k: k > thr)).astype(F32)
        running = jnp.zeros((1, QB), F32)
        for c in range(n_spans):
            k = key_scr[c * CK:(c + 1) * CK, :]
            eq = k == thr
            eqf = jnp.where(eq, 1.0, 0.0)
            before = _dot(tri, eqf.astype(BF16)) + running
            sel = ((k > thr) | (eq & (before < need))) & (k > _KEY_HALF_NEG)
            dsel_scr[c * CK:(c + 1) * CK, :] = jnp.where(sel, 0.0, NEG)
            running = running + jnp.sum(eqf, axis=0, keepdims=True)
        return jnp.int32(0)

    lax.switch(nc - 1, [functools.partial(dsa_select, n) for n in range(1, t // CK + 1)])

    sel_bias_t = jnp.concatenate([sel_bias, jnp.zeros((LANE - nslc, QB), F32)], axis=0).T
    lane = lax.broadcasted_iota(I32, (QB, LANE), 1)
    q_slc, q_dsa = [], []
    for h in range(N_HEADS):
        pos_cols = jnp.where(lane == AUX_POS, SLOPES[h], jnp.where(lane == AUX_POS + 1, SLOPES[h] * CK, 0.0))
        q_slc.append(sel_bias_t + pos_cols)
        q_dsa.append(pos_cols)
    qn_cat = jnp.concatenate([qn, jnp.concatenate(q_slc, axis=0).astype(BF16)], axis=1)
    qd_cat = jnp.concatenate([qd, jnp.concatenate(q_dsa, axis=0).astype(BF16)], axis=1)
    def online(state, s, vt):
        m_old, l_old, acc_old = state
        m_new = jnp.maximum(m_old, jnp.max(s, axis=0, keepdims=True))
        alpha = jnp.exp(m_old - m_new)
        e_ = jnp.exp(s - m_new)
        return m_new, alpha * l_old + jnp.sum(e_, axis=0, keepdims=True), alpha * acc_old + _dot(vt, e_.astype(BF16))

    def scores(c, diagonal):
        rows = pl.ds(pl.multiple_of(c * CK, CK), CK)
        s_slc = _dot_nt(kb_ref[0, P_SLC, rows, :], qn_cat)
        if diagonal:
            kpos = c * CK + lax.broadcasted_iota(I32, (CK, hq), 0)
            s_slc = jnp.where(kpos <= q_pos, s_slc, NEG)
        return s_slc, _dot_nt(kb_ref[0, P_DSA, rows, :], qd_cat) + _tile_heads(dsel_scr[rows, :])

    def flash_chunks(chunks, state, diagonal=None):
        diagonal = diagonal or (False,) * len(chunks)
        sc = [scores(c, d) for c, d in zip(chunks, diagonal)]
        st_slc, st_dsa = state
        for c, (s_slc, s_dsa) in zip(chunks, sc):
            st_slc = online(st_slc, s_slc, kt_ref[0, P_SLC, c])
            st_dsa = online(st_dsa, s_dsa, kt_ref[0, P_DSA, c])
        return st_slc, st_dsa

    init = (jnp.full((1, hq), NEG, F32), jnp.zeros((1, hq), F32), jnp.zeros((HEAD_DIM, hq), F32))
    n_full = nc - 1
    state = lax.fori_loop(0, n_full // 2, lambda i, st: flash_chunks((2 * i, 2 * i + 1), st), (init, init))
    state = lax.cond(n_full % 2 == 1,
                     lambda st: flash_chunks((nc - 2, nc - 1), st, diagonal=(False, True)),
                     lambda st: flash_chunks((nc - 1,), st, diagonal=(True,)), state)

    def finish(st):
        m_, l_, acc = st
        return acc * jnp.where(m_ > 0.5 * NEG, 1.0 / l_, 0.0)

    o_slc_t, o_dsa_t = finish(state[0]), finish(state[1])

    wk = min(NSA_WINDOW + QB, t)
    ws = pl.multiple_of(jnp.clip(q0 - NSA_WINDOW, 0, t - wk), QB)
    kw = kb_ref[0, P_WIN, pl.ds(ws, wk), :]
    dw = q_pos - (ws + lax.broadcasted_iota(I32, (wk, hq), 0))
    qw_cat = jnp.concatenate([qn, qd_cat[:, LANE:2 * LANE]], axis=1)
    sm = jnp.where((dw >= 0) & (dw <= NSA_WINDOW), _dot_nt(kw, qw_cat), NEG)
    mw = jnp.max(sm, axis=0, keepdims=True)
    e = jnp.exp(sm - mw)
    lw = jnp.sum(e, axis=0, keepdims=True)
    o_win_t = _dot_tn(kw[:, 0:LANE], e.astype(BF16)) * jnp.where(mw > 0.5 * NEG, 1.0 / lw, 0.0)

    gates = _sigmoid(misc_t[0:3 * N_HEADS, :])
    nsa_tiles, dsa_tiles = [], []
    top = jnp.zeros((HEAD_DIM, QB), F32)
    vrows = slice(HEAD_DIM, 2 * HEAD_DIM)
    for h in range(N_HEADS):
        cols = slice(h * QB, (h + 1) * QB)
        on = (gates[3 * h:3 * h + 1, :] * o_cmp_t[vrows, cols] + gates[3 * h + 1:3 * h + 2, :] * o_slc_t[:, cols]
              + gates[3 * h + 2:3 * h + 3, :] * o_win_t[vrows, cols])
        nsa_tiles.append(jnp.concatenate([top, on], axis=0).T)
        dsa_tiles.append(jnp.concatenate([top, o_dsa_t[:, cols]], axis=0).T)
    o_ref[0, :, 0:N_HEADS * HEAD_DIM] = _place_heads(nsa_tiles)
    o_ref[0, :, N_HEADS * HEAD_DIM:2 * N_HEADS * HEAD_DIM] = _place_heads(dsa_tiles)


def _cmp_to_slc_t(n16, n_slc):
    start = np.arange(n16) * NSA_CMP_STRIDE
    bstart = np.arange(n_slc) * NSA_SLC_LEN
    ov = np.minimum(start[:, None] + NSA_CMP_LEN, bstart[None, :] + NSA_SLC_LEN) - np.maximum(start[:, None], bstart[None, :])
    return (np.clip(ov, 0, None) / NSA_CMP_LEN).T.astype(np.float32)


def _prompt_attention(z, zi, kb, kt, ki, cb, ct):
    b, t, _ = z.shape
    n16 = cb.shape[1]
    n_slc = -(-t // NSA_SLC_LEN)
    assert n_slc <= AUX_POS and t // CK <= 256 and t % (2 * CK) == 0
    hq = N_HEADS * QB
    qw = N_HEADS * LANE
    topk = min(DSA_TOPK, t // 4)
    c2s = jnp.asarray(_cmp_to_slc_t(n16, n_slc))
    qspec = lambda col: pl.BlockSpec((1, QB, qw), lambda bi, i: (bi, i, col // qw))
    tile = lambda col: pl.BlockSpec((1, QB, LANE), lambda bi, i: (bi, i, col // LANE))
    return pl.pallas_call(
        functools.partial(_attn_kernel, seq_len=t, topk=topk),
        grid=(b, t // QB),
        in_specs=[qspec(C_QN), qspec(C_QD), tile(C_MISC), qspec(ZI_Q), tile(ZI_W),
                  pl.BlockSpec((1, K_PAIRS, t, 2 * LANE), lambda bi, i: (bi, 0, 0, 0)),
                  pl.BlockSpec((1, K_PAIRS, t // CK, HEAD_DIM, CK), lambda bi, i: (bi, 0, 0, 0, 0)),
                  pl.BlockSpec((1, t, 2 * LANE), lambda bi, i: (bi, 0, 0)),
                  pl.BlockSpec((1, n16, LANE), lambda bi, i: (bi, 0, 0)),
                  pl.BlockSpec((1, LANE, n16), lambda bi, i: (bi, 0, 0)),
                  pl.BlockSpec((n_slc, n16), lambda bi, i: (0, 0))],
        out_specs=pl.BlockSpec((1, QB, 2 * N_HEADS * HEAD_DIM), lambda bi, i: (bi, i, 0)),
        out_shape=jax.ShapeDtypeStruct((b, t, 2 * N_HEADS * HEAD_DIM), F32),
        scratch_shapes=[pltpu.VMEM((t, QB), I32), pltpu.VMEM((t, QB), F32)],
        compiler_params=_cparams(("parallel", "parallel")),
        name="prompt_attention",
    )(z, z, z, zi, zi, kb, kt, ki, cb, ct, c2s)


def _layer_norm(v, g, b):
    mu = jnp.mean(v, axis=-1, keepdims=True)
    var = jnp.mean(jnp.square(v - mu), axis=-1, keepdims=True)
    return (v - mu) * lax.rsqrt(var + LN_EPS) * g + b


def _gmlp_kernel(z_ref, lng_ref, lnb_ref, ws_ref, bs_ref, o_ref):
    tc = z_ref.shape[1]
    c = ws_ref.shape[1]
    tril = lax.broadcasted_iota(I32, (c, c), 1) <= lax.broadcasted_iota(I32, (c, c), 0)
    lane_g = lax.broadcasted_iota(I32, (1, GMLP_WIDTH), 1) >> 6
    ws = [jnp.where(tril, ws_ref[g], 0.0).astype(BF16) for g in range(GMLP_GROUPS)]
    for ci in range(tc // c):
        zg = _gelu_tanh(z_ref[0, ci * c:(ci + 1) * c, :])
        u = zg[:, 0:GMLP_WIDTH]
        v = _layer_norm(zg[:, GMLP_WIDTH:2 * GMLP_WIDTH], lng_ref[...], lnb_ref[...]).astype(BF16)
        s = bs_ref[...]
        for g in range(GMLP_GROUPS):
            s = s + jnp.where(lane_g == g, _dot(ws[g], v), 0.0)
        o_ref[0, ci * c:(ci + 1) * c, :] = u * s


def _gmlp(z, ln_g, ln_b, w_s, b_s, tc):
    b, t, _ = z.shape
    c = w_s.shape[1]
    bs_exp = jnp.repeat(b_s.T, HEAD_DIM, axis=1)
    return pl.pallas_call(
        _gmlp_kernel,
        grid=(b, t // tc),
        in_specs=[pl.BlockSpec((1, tc, 2 * GMLP_WIDTH), lambda bi, i: (bi, i, C_GMLP // (2 * GMLP_WIDTH))),
                  pl.BlockSpec((1, GMLP_WIDTH), lambda bi, i: (0, 0)),
                  pl.BlockSpec((1, GMLP_WIDTH), lambda bi, i: (0, 0)),
                  pl.BlockSpec((GMLP_GROUPS, c, c), lambda bi, i: (0, 0, 0)),
                  pl.BlockSpec((c, GMLP_WIDTH), lambda bi, i: (0, 0))],
        out_specs=pl.BlockSpec((1, tc, GMLP_WIDTH), lambda bi, i: (bi, i, 0)),
        out_shape=jax.ShapeDtypeStruct((b, t, GMLP_WIDTH), F32),
        compiler_params=_cparams(("parallel", "parallel")),
        name="gmlp",
    )(z, ln_g.reshape(1, -1), ln_b.reshape(1, -1), w_s, bs_exp)


def _head_sum(x):
    lane_h = lax.broadcasted_iota(I32, (1, x.shape[-1]), 1) >> 6
    out = jnp.zeros_like(x)
    for h in range(x.shape[-1] // HEAD_DIM):
        msk = lane_h == h
        out = out + jnp.where(msk, jnp.sum(jnp.where(msk, x, 0.0), axis=-1, keepdims=True), 0.0)
    return out


def _softplus(x):
    return jnp.maximum(x, 0.0) + jnp.log1p(jnp.exp(-jnp.abs(x)))


def _rwkv_features(f, w0, w2, a0, a2, g2, kkw, ka, rk):
    wd = RWKV_WIDTH
    r, k, v = f[:, 0:wd], f[:, wd:2 * wd], f[:, 2 * wd:3 * wd]
    wl, al, gl = f[:, 3 * wd:3 * wd + 64], f[:, 3 * wd + 64:3 * wd + 128], f[:, 3 * wd + 128:3 * wd + 256]
    w_log = -_softplus(-(w0 + _dot(jnp.tanh(wl).astype(BF16), w2))) - 0.5
    log_decay = -jnp.exp(w_log)
    a = _sigmoid(a0 + _dot(al.astype(BF16), a2))
    g = _dot(_sigmoid(gl).astype(BF16), g2)
    kk = k * kkw
    kk = kk * lax.rsqrt(_head_sum(kk * kk) + 1e-12)
    k = k * (1.0 + (a - 1.0) * ka)
    bonus = _head_sum(r * k * rk) * v
    return r, log_decay, k, v, kk, kk * a, g, bonus


RWKV_CHUNK = 64


def _rwkv_pre_kernel(f_ref, prev_ref, shift_ref, mu_ref, w0_ref, w2_ref, a0_ref, a2_ref, g2_ref, kkw_ref, ka_ref, rk_ref,
                     kh_o, rh_o, kb_o, bb_o, kbe_o, bbe_o, v_o, pe_o, g_o, bonus_o, *, chunk):
    feat = f_ref[0]
    tm = feat.shape[0]
    first = jnp.where(pl.program_id(1) == 0, shift_ref[0], prev_ref[0, 7:8, :])
    prev = jnp.where(lax.broadcasted_iota(I32, (tm, 1), 0) == 0, first, pltpu.roll(feat, 1, axis=0))
    f = feat + mu_ref[...] * (prev - feat)
    r, lw, k, v, kk, be, g, bonus = _rwkv_features(f, w0_ref[...], w2_ref[...], a0_ref[...], a2_ref[...], g2_ref[...],
                                                   kkw_ref[...], ka_ref[...], rk_ref[...])
    row = lax.broadcasted_iota(I32, (tm, tm), 0)
    col = lax.broadcasted_iota(I32, (tm, tm), 1)
    tri = jnp.where((col >= (row // chunk) * chunk) & (col <= row), 1.0, 0.0).astype(BF16)
    p1 = lw.astype(BF16)
    r1 = lw - p1.astype(F32)
    p2 = r1.astype(BF16)
    p3 = (r1 - p2.astype(F32)).astype(BF16)
    cum = _dot(tri, p1) + (_dot(tri, p2) + _dot(tri, p3))
    cum_end = jnp.concatenate([jnp.broadcast_to(cum[c * chunk + chunk - 1:(c + 1) * chunk, :], (chunk, cum.shape[1]))
                               for c in range(tm // chunk)], axis=0)
    down, to_end = jnp.exp(-cum), jnp.exp(cum_end - cum)
    for o, x in ((kh_o, kk * jnp.exp(cum - lw)), (rh_o, r * jnp.exp(cum)), (kb_o, k * down), (bb_o, be * down),
                 (kbe_o, k * to_end), (bbe_o, be * to_end), (v_o, v), (pe_o, jnp.exp(cum_end)), (g_o, g), (bonus_o, bonus)):
        o[0] = x.astype(o.dtype)


def _rwkv_pre(z, shift_prev, lp, tm):
    b, t, _ = z.shape
    wd = RWKV_WIDTH
    pw = 4 * wd
    vec = lambda n: pl.BlockSpec((1, n), lambda bi, i: (0, 0))
    mat = lambda r, c: pl.BlockSpec((r, c), lambda bi, i: (0, 0))
    fl = jax.ShapeDtypeStruct((b, t, wd), F32)
    hl = jax.ShapeDtypeStruct((b, t, wd), BF16)
    fspec = pl.BlockSpec((1, tm, wd), lambda bi, i: (bi, i, 0))
    return pl.pallas_call(
        functools.partial(_rwkv_pre_kernel, chunk=min(RWKV_CHUNK, t)),
        grid=(b, t // tm),
        in_specs=[pl.BlockSpec((1, tm, pw), lambda bi, i: (bi, i, C_RWKV // pw)),
                  pl.BlockSpec((1, 8, pw), lambda bi, i: (bi, jnp.maximum(i * (tm // 8) - 1, 0), C_RWKV // pw)),
                  pl.BlockSpec((1, 1, pw), lambda bi, i: (bi, 0, 0)),
                  vec(pw), vec(wd), mat(64, wd), vec(wd), mat(64, wd), mat(128, wd), vec(wd), vec(wd), vec(wd)],
        out_specs=[fspec] * 10,
        out_shape=[hl] * 7 + [fl] * 3,
        compiler_params=_cparams(("parallel", "parallel")),
        name="rwkv_pre",
    )(z, z, shift_prev, lp['rwkv_mu'].reshape(1, pw), lp['rwkv_w0'].reshape(1, wd), lp['rwkv_w2'].astype(BF16),
      lp['rwkv_a0'].reshape(1, wd), lp['rwkv_a2'].astype(BF16), lp['rwkv_g2'].astype(BF16),
      lp['rwkv_kk'].reshape(1, wd), lp['rwkv_ka'].reshape(1, wd), lp['rwkv_rk'].reshape(1, wd))


def _hdot(a, b):
    return jnp.dot(a, b, preferred_element_type=F32, precision=lax.Precision.HIGHEST)


def _rwkv_scan_kernel(kh_ref, rh_ref, kb_ref, bb_ref, kbe_ref, bbe_ref, v_ref, pe_ref, g_ref, bonus_ref, lnw_ref, lnb_ref,
                      s0_ref, o_ref, s_ref):
    @pl.when(pl.program_id(1) == 0)
    def _():
        s_ref[...] = s0_ref[...]

    bt, c, wd = kh_ref.shape
    n = N_HEADS * c
    row = lax.broadcasted_iota(I32, (n, wd), 0)
    col = lax.broadcasted_iota(I32, (n, wd), 1)
    same_head = (row // c) == (col // HEAD_DIM)
    tpos, jpos = row % c, col % HEAD_DIM
    strict, incl = same_head & (jpos < tpos), same_head & (jpos <= tpos)
    eye = jnp.where(same_head & (jpos == tpos), 1.0, 0.0)
    bf = lambda x: x.astype(BF16)
    block_diag = lambda ref, i: jnp.where(same_head, jnp.concatenate([ref[i]] * N_HEADS, axis=0), jnp.zeros((), BF16))
    for i in range(bt):
        kh, rh, kb, bb, kbe, bbe, v = (block_diag(ref, i) for ref in (kh_ref, rh_ref, kb_ref, bb_ref, kbe_ref, bbe_ref, v_ref))
        g1 = _dot_nt(jnp.concatenate([kh, rh], axis=0), jnp.concatenate([bb, kb], axis=0))
        a_kb = jnp.where(strict, g1[0:n, 0:n], 0.0)
        a_kk = jnp.where(strict, g1[0:n, n:2 * n], 0.0)
        a_rb = jnp.where(incl, g1[n:2 * n, 0:n], 0.0)
        a_rk = jnp.where(incl, g1[n:2 * n, n:2 * n], 0.0)
        s0 = s_ref[i]
        zy = _dot(jnp.concatenate([jnp.concatenate([kh, bf(a_kk)], axis=1), jnp.concatenate([rh, bf(a_rk)], axis=1)], axis=0),
                  jnp.concatenate([bf(s0), v], axis=0))
        z, y = zy[0:n], zy[n:2 * n]
        p = bf(-a_kb)
        tinv = eye - a_kb
        p = bf(_dot(p, p))
        for _ in range(int(np.log2(c)) - 2):
            tinv, p = tinv + _dot(bf(tinv), p), bf(_dot(p, p))
        tinv = tinv + _dot(bf(tinv), p)
        u = bf(-_dot(bf(tinv), bf(z)))
        y = y + _dot(bf(a_rb), u)
        pe_rows = pe_ref[i]
        pe_col = jnp.concatenate([pe_rows] * (LANE // c), axis=0).T
        s_ref[i] = (s0 * jnp.concatenate([pe_col] * (wd // LANE), axis=1)
                    + _dot_tn(jnp.concatenate([kbe, bbe], axis=0), jnp.concatenate([v, u], axis=0)))
        y_flat = y[0:c]
        for h in range(1, N_HEADS):
            y_flat = y_flat + y[h * c:(h + 1) * c]
        mu = _head_sum(y_flat) * (1.0 / HEAD_DIM)
        yc = y_flat - mu
        yn = yc * lax.rsqrt(_head_sum(yc * yc) * (1.0 / HEAD_DIM) + RWKV_GN_EPS)
        o_ref[i] = (yn * lnw_ref[...] + lnb_ref[...] + bonus_ref[i]) * g_ref[i]


def _rwkv_scan(kh, rh, kb, bb, kbe, bbe, v, pe, g, bonus, ln_w, ln_b, s0):
    b, t, wd = kh.shape
    c = min(RWKV_CHUNK, t)
    assert c == HEAD_DIM
    bt = _pick(b, (4, 2, 1))
    fspec = pl.BlockSpec((bt, c, wd), lambda bi, i: (bi, i, 0))
    sspec = pl.BlockSpec((bt, wd, wd), lambda bi, i: (bi, 0, 0))
    vec = pl.BlockSpec((1, wd), lambda bi, i: (0, 0))
    return pl.pallas_call(
        _rwkv_scan_kernel,
        grid=(b // bt, t // c),
        in_specs=[fspec] * 10 + [vec, vec, sspec],
        out_specs=[fspec, sspec],
        out_shape=[jax.ShapeDtypeStruct((b, t, wd), F32), jax.ShapeDtypeStruct((b, wd, wd), F32)],
        compiler_params=_cparams(("parallel", "arbitrary")),
        name="rwkv_scan",
    )(kh, rh, kb, bb, kbe, bbe, v, pe, g, bonus, ln_w.reshape(1, wd), ln_b.reshape(1, wd), s0)


SLOPES = tuple(2.0 ** (-8.0 * (h + 1) / N_HEADS) for h in range(N_HEADS))
SEQ_PER_STEP = 2


def _head_rows(tiles):
    rows8 = lax.broadcasted_iota(I32, (8, LANE), 0)
    out = jnp.zeros((8, LANE), F32)
    for h in range(N_HEADS):
        out = jnp.where(rows8 == h, tiles[:, h * LANE:(h + 1) * LANE], out)
    return out * (HEAD_DIM ** -0.5)


def _per_head(vals):
    head_row = lax.broadcasted_iota(I32, (8, 1), 0)
    out = jnp.zeros((8, 1), F32)
    for h in range(N_HEADS):
        out = jnp.where(head_row == h, vals[h], out)
    return out


def _sidx_kernel(pt_ref, qi_ref, kn_ref, w_ref, *refs, n_pages, page, group):
    o_ref = refs[group * n_pages]
    past = n_pages * page
    lane = lax.broadcasted_iota(I32, (1, LANE), 1)
    for g in range(group):
        pages = refs[g * n_pages:(g + 1) * n_pages]
        q = _head_rows(qi_ref[g])
        w = _per_head([w_ref[g, :, h:h + 1] * (N_HEADS ** -0.5) for h in range(N_HEADS)])
        q_hi, q_lo = _split_bf16(q[:, 0:HEAD_DIM])
        k_hi, k_lo = _split_bf16(jnp.concatenate([r[0, 0, 0] for r in pages], axis=1))
        logits = _dot(q_hi, k_hi) + (_dot(q_hi, k_lo) + _dot(q_lo, k_hi))
        o_ref[g, :, 0:past] = jnp.sum(jnp.maximum(logits, 0.0) * w, axis=0, keepdims=True)
        self_logit = jnp.sum(q * kn_ref[g], axis=1, keepdims=True)
        idx_self = jnp.sum(jnp.maximum(self_logit, 0.0) * w, axis=0, keepdims=True)
        o_ref[g, :, past:past + LANE] = jnp.where(lane == 0, idx_self, NEG)


def _sample_index(page_table, zi, cache_d, layer):
    b, n_pages = page_table.shape
    page = cache_d.shape[-1]
    width = n_pages * page + LANE
    grp = _pick(b, (SEQ_PER_STEP, 1))
    kern = functools.partial(_sidx_kernel, n_pages=n_pages, page=page, group=grp)
    page_spec = lambda g, p: pl.BlockSpec((1, 1, 1, HEAD_DIM, page), lambda bi, pt: (layer, pt[grp * bi + g, p], 2, 0, 0))
    zblk = lambda width_, col: pl.BlockSpec((grp, 1, width_), lambda bi, pt: (bi, 0, col // width_))
    gs = pltpu.PrefetchScalarGridSpec(
        num_scalar_prefetch=1, grid=(b // grp,),
        in_specs=[zblk(N_HEADS * LANE, ZI_Q), zblk(LANE, ZI_K), zblk(LANE, ZI_W)]
                 + [page_spec(g, p) for g in range(grp) for p in range(n_pages)],
        out_specs=pl.BlockSpec((grp, 1, width), lambda bi, pt: (bi, 0, 0)))
    return pl.pallas_call(kern, grid_spec=gs, out_shape=jax.ShapeDtypeStruct((b, 1, width), F32),
                          compiler_params=_cparams(("parallel",)), name="sample_index",
                          )(page_table, zi, zi, zi, *([cache_d] * (grp * n_pages)))


def _stopk_kernel(idx_ref, o_ref, *, topk):
    key = _sortable_key(idx_ref[...])
    rows, width = key.shape

    def bit_body(i, ans):
        cand = ans + lax.shift_left(jnp.int32(1), 31 - i)
        cnt = jnp.sum(jnp.where(key >= cand, 1, 0), axis=1, keepdims=True)
        return jnp.where(cnt >= topk, cand, ans)

    thr = lax.fori_loop(0, 32, bit_body, jnp.full((rows, 1), INT_MIN, I32))
    need = (topk - jnp.sum(jnp.where(key > thr, 1, 0), axis=1, keepdims=True)).astype(F32)
    tri = jnp.where(lax.broadcasted_iota(I32, (LANE, LANE), 0) < lax.broadcasted_iota(I32, (LANE, LANE), 1), 1.0, 0.0).astype(BF16)
    running = jnp.zeros((rows, 1), F32)
    for c in range(width // LANE):
        k = key[:, c * LANE:(c + 1) * LANE]
        eq = k == thr
        eqf = jnp.where(eq, 1.0, 0.0)
        before = _dot(eqf.astype(BF16), tri) + running
        sel = ((k > thr) | (eq & (before < need))) & (k > _KEY_HALF_NEG)
        o_ref[:, c * LANE:(c + 1) * LANE] = jnp.where(sel, 1.0, 0.0)
        running = running + jnp.sum(eqf, axis=1, keepdims=True)


def _sample_topk(idx, topk):
    return pl.pallas_call(functools.partial(_stopk_kernel, topk=topk),
                          out_shape=jax.ShapeDtypeStruct(idx.shape, F32),
                          compiler_params=pltpu.CompilerParams(vmem_limit_bytes=VMEM_LIMIT), name="sample_topk")(idx)


def _sattn_kernel(pt_ref, qn_ref, qd_ref, misc_ref, kslc_ref, kwin_ref, kdsa_ref, dsel_ref, win_ref, w1a_ref, w1b_ref,
                  posa_ref, posb_ref, w2_ref, c2s_ref, exp_ref, *refs, n_pages, page, group):
    o_ref, win_o_ref, xt_scr = refs[2 * group * n_pages:2 * group * n_pages + 3]
    for g in range(group):
        one = pl.ds(g, 1)
        _sattn_one(qn_ref.at[one], qd_ref.at[one], misc_ref.at[one], kslc_ref.at[one], kwin_ref.at[one], kdsa_ref.at[one],
                   dsel_ref.at[one], win_ref.at[:, one], w1a_ref, w1b_ref, posa_ref, posb_ref, w2_ref, c2s_ref, exp_ref,
                   refs[g * n_pages:(g + 1) * n_pages], refs[(group + g) * n_pages:(group + g + 1) * n_pages],
                   o_ref.at[one], win_o_ref.at[one], xt_scr.at[g], n_pages, page)


def _sattn_one(qn_ref, qd_ref, misc_ref, kslc_ref, kwin_ref, kdsa_ref, dsel_ref, win_ref, w1a_ref, w1b_ref, posa_ref,
               posb_ref, w2_ref, c2s_ref, exp_ref, nsa_pages, dsa_pages, o_ref, win_o_ref, xt_scr, n_pages, page):
    past = n_pages * page
    q_pos = past
    live = lax.broadcasted_iota(I32, (8, 1), 0) < N_HEADS
    qn, qd = _head_rows(qn_ref[0]), _head_rows(qd_ref[0])
    slope = _per_head(SLOPES)
    gates = _sigmoid(misc_ref[0, :, 0:3 * N_HEADS])
    gate = [_per_head([gates[:, 3 * h + c:3 * h + c + 1] for h in range(N_HEADS)]) for c in range(3)]

    def attend(q, k_t, v_t, mask, dist, new_pair, self_mask):
        sm = jnp.where(mask, _dot(q[:, 0:HEAD_DIM].astype(BF16), k_t) - slope * dist.astype(F32), NEG)
        s_self = jnp.where(self_mask, jnp.sum(q * new_pair, axis=1, keepdims=True), NEG)
        mx = jnp.maximum(s_self, jnp.max(sm, axis=1, keepdims=True))
        e = jnp.where(mask, jnp.exp(sm - mx), 0.0)
        e_self = jnp.where(self_mask, jnp.exp(s_self - mx), 0.0)
        l = e_self + jnp.sum(e, axis=1, keepdims=True)
        o = _dot_nt(e.astype(BF16), v_t) + e_self * new_pair[:, HEAD_DIM:2 * HEAD_DIM]
        return o / jnp.where(l > 0.0, l, 1.0)

    for p in range(n_pages):
        xt_scr[p * page:(p + 1) * page, :] = nsa_pages[p][0, 0, 0:2].reshape(2 * HEAD_DIM, page).T
    n16 = past // NSA_CMP_STRIDE
    x_all = jnp.concatenate([xt_scr[pl.ds(r, n16, stride=NSA_CMP_STRIDE), :] for r in range(NSA_CMP_STRIDE)], axis=1)
    a = _dot((x_all + posa_ref[...]).astype(BF16), w1a_ref[...])
    b = _dot((x_all + posb_ref[...]).astype(BF16), w1b_ref[...])
    hid = _gelu_tanh(a + pltpu.roll(b, n16 - 1, axis=0))
    comp = _dot(hid.astype(BF16), w2_ref[...]).astype(BF16)

    blk = lax.broadcasted_iota(I32, (1, n16), 1)
    d_cmp = q_pos - (blk * NSA_CMP_STRIDE + NSA_CMP_LEN - 1)
    m_cmp = (d_cmp >= 0) & live
    sm = jnp.where(m_cmp, _dot_nt(qn.astype(BF16), comp) - slope * d_cmp.astype(F32), NEG)
    e = jnp.where(m_cmp, jnp.exp(sm - jnp.max(sm, axis=1, keepdims=True)), 0.0)
    l = jnp.sum(e, axis=1, keepdims=True)
    pr = e / jnp.where(l > 0.0, l, 1.0)
    o_cmp = _dot(pr.astype(BF16), comp)[:, HEAD_DIM:2 * HEAD_DIM]
    psum = jnp.sum(pr, axis=0, keepdims=True)

    n_slc = past // NSA_SLC_LEN + 1
    imp = _hdot(jnp.broadcast_to(psum, (8, n16)), c2s_ref[...])[0:1]
    j = lax.broadcasted_iota(I32, (1, LANE), 1)
    cur = q_pos // NSA_SLC_LEN
    forced = (j == 0) | (j == cur) | (j == cur - 1)
    score = jnp.where((j * NSA_SLC_LEN <= q_pos) & (j < n_slc), imp + jnp.where(forced, FORCE_BONUS, 0.0), NEG)
    srow = jnp.broadcast_to(score, (LANE, LANE))
    scol = srow.T
    jp = lax.broadcasted_iota(I32, (LANE, LANE), 0)
    jj = lax.broadcasted_iota(I32, (LANE, LANE), 1)
    rank = jnp.sum(jnp.where((scol > srow) | ((scol == srow) & (jp < jj)), 1, 0), axis=0, keepdims=True)
    sel = jnp.where((rank < min(NSA_TOPN, n_slc)) & (score > 0.5 * NEG), 1.0, 0.0)
    sel_pos = _dot(jnp.broadcast_to(sel, (8, LANE)).astype(BF16), exp_ref[...])[0:1]

    cat = lambda refs, comp_i: jnp.concatenate([r[0, 0, comp_i] for r in refs], axis=1).astype(BF16)
    dist = q_pos - lax.broadcasted_iota(I32, (1, past), 1)
    m_slc = (sel_pos[:, 0:past] > 0.5) & live
    m_dsa = (dsel_ref[0, :, 0:past] > 0.5) & live
    self_slc = (sel_pos[:, past:past + 1] > 0.5) & live
    self_dsa = (dsel_ref[0, :, past:past + 1] > 0.5) & live
    wb = win_ref.shape[-1]
    d_win = q_pos - (past - wb + lax.broadcasted_iota(I32, (1, wb), 1))
    m_win = (d_win >= 0) & (d_win <= NSA_WINDOW) & (d_win <= q_pos) & live
    o_slc = attend(qn, cat(nsa_pages, 2), cat(nsa_pages, 3), m_slc, dist, kslc_ref[0], self_slc)
    o_win = attend(qn, win_ref[0, 0, 0].astype(BF16), win_ref[0, 0, 1].astype(BF16), m_win, d_win, kwin_ref[0], live)
    o_dsa = attend(qd, cat(dsa_pages, 0), cat(dsa_pages, 1), m_dsa, dist, kdsa_ref[0], self_dsa)
    o_nsa = gate[0] * o_cmp + gate[1] * o_slc + gate[2] * o_win
    o_ref[0] = jnp.concatenate([o_nsa, o_dsa], axis=1)

    new_cols = jnp.broadcast_to(kwin_ref[0], (LANE, LANE)).T
    lane_w = lax.broadcasted_iota(I32, (HEAD_DIM, wb), 1)
    for c in range(2):
        win_o_ref[0, c] = jnp.where(lane_w == wb - 1, new_cols[c * HEAD_DIM:(c + 1) * HEAD_DIM, 0:1],
                                    pltpu.roll(win_ref[0, 0, c], wb - 1, axis=1))


def _sample_attention(page_table, z, dsel, win_t, cmp_w, cache_n, cache_d, layer):
    b, n_pages = page_table.shape
    page = cache_n.shape[-1]
    past = n_pages * page
    wb = win_t.shape[-1]
    n16 = past // NSA_CMP_STRIDE
    w1a, w1b, pos_a, pos_b, w2 = cmp_w
    n_slc = past // NSA_SLC_LEN + 1
    c2s = np.zeros((n16, LANE), np.float32)
    c2s[:, :n_slc] = _cmp_to_slc_t(n16, n_slc).T
    expand = (np.arange(past + LANE)[None, :] // NSA_SLC_LEN == np.arange(LANE)[:, None]) & (np.arange(past + LANE)[None, :] <= past)
    grp = _pick(b, (SEQ_PER_STEP, 1))
    kern = functools.partial(_sattn_kernel, n_pages=n_pages, page=page, group=grp)
    full = lambda shape: pl.BlockSpec(shape, lambda bi, pt: (0,) * len(shape))
    nspec = lambda g, p: pl.BlockSpec((1, 1, 4, HEAD_DIM, page), lambda bi, pt: (layer, pt[grp * bi + g, p], 0, 0, 0))
    dspec = lambda g, p: pl.BlockSpec((1, 1, 2, HEAD_DIM, page), lambda bi, pt: (layer, pt[grp * bi + g, p], 0, 0, 0))
    pages = [(g, p) for g in range(grp) for p in range(n_pages)]
    qw = N_HEADS * LANE
    zblk = lambda width, col: pl.BlockSpec((grp, 1, width), lambda bi, pt: (bi, 0, col // width))
    gs = pltpu.PrefetchScalarGridSpec(
        num_scalar_prefetch=1, grid=(b // grp,),
        in_specs=[zblk(qw, C_QN), zblk(qw, C_QD), zblk(LANE, C_MISC), zblk(LANE, C_K + P_SLC * LANE),
                  zblk(LANE, C_K + P_WIN * LANE), zblk(LANE, C_K + P_DSA * LANE),
                  pl.BlockSpec((grp, 1, past + LANE), lambda bi, pt: (bi, 0, 0)),
                  pl.BlockSpec((1, grp, 2, HEAD_DIM, wb), lambda bi, pt: (layer, bi, 0, 0, 0)),
                  full(w1a.shape), full(w1b.shape), full(pos_a.shape), full(pos_b.shape), full(w2.shape),
                  full(c2s.shape), full(expand.shape)]
                 + [nspec(g, p) for g, p in pages] + [dspec(g, p) for g, p in pages],
        out_specs=[pl.BlockSpec((grp, 8, LANE), lambda bi, pt: (bi, 0, 0)),
                   pl.BlockSpec((grp, 2, HEAD_DIM, wb), lambda bi, pt: (bi, 0, 0, 0))],
        scratch_shapes=[pltpu.VMEM((grp, past, 2 * HEAD_DIM), F32)])
    return pl.pallas_call(
        kern, grid_spec=gs,
        out_shape=[jax.ShapeDtypeStruct((b, 8, LANE), F32), jax.ShapeDtypeStruct((b, 2, HEAD_DIM, wb), F32)],
        compiler_params=_cparams(("parallel",)), name="sample_attention",
    )(page_table, z, z, z, z, z, z, dsel, win_t, w1a, w1b, pos_a, pos_b, w2, jnp.asarray(c2s), jnp.asarray(expand, BF16),
      *([cache_n] * (grp * n_pages)), *([cache_d] * (grp * n_pages)))


def _smix_kernel(zg_ref, zr_ref, shift_ref, s_ref, lng_ref, lnb_ref, gw_ref, gb_ref, mu_ref, w0_ref, w2_ref, a0_ref,
                 a2_ref, g2_ref, kkw_ref, ka_ref, rk_ref, lnw_ref, lnb2_ref, oc_ref, vc_ref, or_ref, so_ref):
    bt = zg_ref.shape[0]
    zg = _gelu_tanh(zg_ref[...])
    v = _layer_norm(zg[:, GMLP_WIDTH:2 * GMLP_WIDTH], lng_ref[...], lnb_ref[...])
    vc_ref[...] = v
    oc_ref[...] = zg[:, 0:GMLP_WIDTH] * (v * gw_ref[...] + gb_ref[...])

    feat = zr_ref[...]
    f = feat + mu_ref[...] * (shift_ref[...] - feat)
    r, lw, k, vv, kk, be, g, bonus = _rwkv_features(f, w0_ref[...], w2_ref[...], a0_ref[...], a2_ref[...], g2_ref[...],
                                                    kkw_ref[...], ka_ref[...], rk_ref[...])
    w = jnp.exp(lw)
    lane_t = lax.broadcasted_iota(I32, (LANE, LANE), 1)
    y_cols = [jnp.zeros((LANE, LANE), F32) for _ in range(N_HEADS // 2)]
    for i in range(bt):
        for hp in range(N_HEADS // 2):
            v_colb = jnp.broadcast_to(vv[i:i + 1, hp * LANE:(hp + 1) * LANE], (LANE, LANE)).T
            ys = []
            for hh in range(2):
                h = 2 * hp + hh
                sl = slice(h * HEAD_DIM, (h + 1) * HEAD_DIM)
                s0 = s_ref[i, h]
                sa = -jnp.sum(s0 * kk[i:i + 1, sl], axis=1, keepdims=True)
                s1 = (s0 * w[i:i + 1, sl] + sa * be[i:i + 1, sl]
                      + v_colb[hh * HEAD_DIM:(hh + 1) * HEAD_DIM, 0:HEAD_DIM] * k[i:i + 1, sl])
                so_ref[i, h] = s1
                ys.append(jnp.sum(s1 * r[i:i + 1, sl], axis=1, keepdims=True))
            y_cols[hp] = jnp.where(lane_t == i, jnp.concatenate(ys, axis=0), y_cols[hp])
    y = jnp.concatenate([yc.T[0:bt] for yc in y_cols], axis=1)
    mu = _head_sum(y) * (1.0 / HEAD_DIM)
    yc = y - mu
    yn = yc * lax.rsqrt(_head_sum(yc * yc) * (1.0 / HEAD_DIM) + RWKV_GN_EPS)
    or_ref[...] = (yn * lnw_ref[...] + lnb2_ref[...] + bonus) * g


def _sample_mixers(z, shift_prev, wkv_prev, lp, bt):
    b = z.shape[0]
    wd = RWKV_WIDTH
    pw = 4 * wd
    gw = jnp.repeat(lp['gmlp_ws'][:, 0, 0], HEAD_DIM).reshape(1, GMLP_WIDTH)
    gb = jnp.repeat(lp['gmlp_bs'][:, 0], HEAD_DIM).reshape(1, GMLP_WIDTH)
    vec = lambda n: pl.BlockSpec((1, n), lambda i: (0, 0))
    mat = lambda r, c: pl.BlockSpec((r, c), lambda i: (0, 0))
    row = lambda n: pl.BlockSpec((bt, n), lambda i: (i, 0))
    sspec = pl.BlockSpec((bt, N_HEADS, HEAD_DIM, HEAD_DIM), lambda i: (i, 0, 0, 0))
    fl = jax.ShapeDtypeStruct((b, wd), F32)
    return pl.pallas_call(
        _smix_kernel,
        grid=(b // bt,),
        in_specs=[pl.BlockSpec((bt, 2 * GMLP_WIDTH), lambda i: (i, C_GMLP // (2 * GMLP_WIDTH))),
                  pl.BlockSpec((bt, pw), lambda i: (i, C_RWKV // pw)),
                  row(pw), sspec, vec(wd), vec(wd), vec(wd), vec(wd),
                  vec(pw), vec(wd), mat(64, wd), vec(wd), mat(64, wd), mat(128, wd), vec(wd), vec(wd), vec(wd), vec(wd), vec(wd)],
        out_specs=[row(wd), row(wd), row(wd), sspec],
        out_shape=[fl, fl, fl, jax.ShapeDtypeStruct(wkv_prev.shape, F32)],
        compiler_params=_cparams(("parallel",)),
        name="sample_mixers",
    )(z, z, shift_prev, wkv_prev, lp['gmlp_ln_g'].reshape(1, wd), lp['gmlp_ln_b'].reshape(1, wd), gw, gb,
      lp['rwkv_mu'].reshape(1, pw), lp['rwkv_w0'].reshape(1, wd), lp['rwkv_w2'].astype(BF16),
      lp['rwkv_a0'].reshape(1, wd), lp['rwkv_a2'].astype(BF16), lp['rwkv_g2'].astype(BF16),
      lp['rwkv_kk'].reshape(1, wd), lp['rwkv_ka'].reshape(1, wd), lp['rwkv_rk'].reshape(1, wd),
      lp['rwkv_ln_w'].reshape(1, wd), lp['rwkv_ln_b'].reshape(1, wd))


_W_IN_COLS = (('q_nsa', 256), ('kv_nsa', 384), ('g_nsa', 12), ('q_dsa', 256), ('kv_dsa', 128), ('q_idx', 256),
              ('k_idx', 64), ('w_idx', 4), ('gmlp', 512), ('rwkv', 1024), ('merge', 4096))


def _proj_weights(w_in):
    d = w_in.shape[0]
    parts, off = {}, 0
    for name, width in _W_IN_COLS:
        parts[name] = w_in[:, off:off + width]
        off += width

    def pad_heads(w):
        w4 = w.reshape(d, N_HEADS, HEAD_DIM)
        return jnp.concatenate([w4, jnp.zeros_like(w4)], axis=-1).reshape(d, N_HEADS * LANE)

    zeros = lambda n: jnp.zeros((d, n), w_in.dtype)
    w = jnp.concatenate([parts['merge'], parts['rwkv'], parts['gmlp'], pad_heads(parts['q_nsa']), pad_heads(parts['q_dsa']),
                         parts['g_nsa'], zeros(LANE - 12), parts['kv_nsa'], parts['kv_dsa']], axis=1)
    w_idx = jnp.concatenate([pad_heads(parts['q_idx']), parts['k_idx'], zeros(LANE - HEAD_DIM),
                             parts['w_idx'], zeros(LANE - N_HEADS)], axis=1)
    return w.astype(BF16), _split_bf16(w_idx)


def _compress_weights_rows(w1, w2, pos):
    half = NSA_CMP_STRIDE * HEAD_DIM
    return (w1[:, :half].astype(BF16), w1[:, half:].astype(BF16),
            pos[:, :NSA_CMP_STRIDE].reshape(2, 1, half), pos[:, NSA_CMP_STRIDE:].reshape(2, 1, half), w2.astype(BF16))


def _compress_weights_pairs(w1, w2, pos):
    hid = w1.shape[-1]
    w1r = w1.reshape(2, 2, NSA_CMP_STRIDE, HEAD_DIM, hid)
    posr = pos.reshape(2, 2, NSA_CMP_STRIDE, HEAD_DIM)
    bigs, poss = [], []
    for half in range(2):
        big = jnp.zeros((NSA_CMP_STRIDE, 2, HEAD_DIM, 2, hid), w1.dtype)
        for c in range(2):
            big = big.at[:, c, :, c, :].set(w1r[c, half])
        bigs.append(big.reshape(NSA_CMP_STRIDE * 2 * HEAD_DIM, 2 * hid).astype(BF16))
        poss.append(jnp.transpose(posr[:, half], (1, 0, 2)).reshape(1, NSA_CMP_STRIDE * 2 * HEAD_DIM))
    w2b = jnp.zeros((2, hid, 2, HEAD_DIM), w2.dtype)
    for c in range(2):
        w2b = w2b.at[c, :, c, :].set(w2[c])
    return bigs[0], bigs[1], poss[0], poss[1], w2b.reshape(2 * hid, 2 * HEAD_DIM).astype(BF16)


def _pick(n, cands):
    for c in cands:
        if n % c == 0:
            return c
    return n


def kernel(x_prompt, x_sample, cache_nsa, cache_dsa, state_nsa_win, state_rwkv_shift, state_rwkv_wkv, page_table,
           c_prompt, c_sample, w_ada, b_ada, norm_mix_g, norm_ffn_g, w_in, nsa_cmp_w1, nsa_cmp_w2, nsa_cmp_pos,
           gmlp_ln_g, gmlp_ln_b, gmlp_ws, gmlp_bs, rwkv_mu, rwkv_w0, rwkv_w2, rwkv_a0, rwkv_a2, rwkv_g2, rwkv_kk,
           rwkv_ka, rwkv_rk, rwkv_ln_w, rwkv_ln_b, w_branch, w_out, w_ffn_in, w_ffn_out, final_norm_g):
    depth = w_ada.shape[0]
    bp, t, d = x_prompt.shape
    bs = x_sample.shape[0]
    assert x_sample.shape[1] == 1 and t % CK == 0 and t % (8 * NSA_CMP_STRIDE) == 0
    cache_n = jnp.transpose(cache_nsa, (0, 1, 3, 4, 2))
    cache_d = jnp.transpose(cache_dsa, (0, 1, 3, 4, 2))
    win_t = jnp.transpose(state_nsa_win, (0, 1, 3, 4, 2))
    past = page_table.shape[1] * cache_nsa.shape[2]
    c_all = jnp.concatenate([c_prompt, c_sample], axis=0)
    xp, xs = x_prompt, x_sample.reshape(1, bs, d)
    tm_p = _pick(t, (512, 256, 128))
    tm_f = _pick(t, (1024, 512, 256, 128))
    tf = _pick(w_ffn_out.shape[1], (256, 128))
    tn = _pick(N_PROJ, (2432,))
    outs = {k: [] for k in ('rows_n_p', 'rows_n_s', 'rows_d_p', 'rows_d_s', 'win_p', 'win_s', 'v_s', 'shift_p', 'shift_s',
                            'wkv_p', 'wkv_s')}
    for l in range(depth):
        lp = {'gmlp_ln_g': gmlp_ln_g[l], 'gmlp_ln_b': gmlp_ln_b[l], 'gmlp_ws': gmlp_ws[l], 'gmlp_bs': gmlp_bs[l],
              'rwkv_mu': rwkv_mu[l], 'rwkv_w0': rwkv_w0[l], 'rwkv_w2': rwkv_w2[l], 'rwkv_a0': rwkv_a0[l],
              'rwkv_a2': rwkv_a2[l], 'rwkv_g2': rwkv_g2[l], 'rwkv_kk': rwkv_kk[l], 'rwkv_ka': rwkv_ka[l],
              'rwkv_rk': rwkv_rk[l], 'rwkv_ln_w': rwkv_ln_w[l], 'rwkv_ln_b': rwkv_ln_b[l]}
        last = l == depth - 1
        mod = _ada(c_all, w_ada[l], b_ada[l]).reshape(bp + bs, 6, d)
        mod_p = [mod[:bp, i:i + 1] for i in range(6)]
        mod_s = [mod[bp:, i][None] for i in range(6)]
        w_proj, (wi_hi, wi_lo) = _proj_weights(w_in[l])
        wb, wo = w_branch[l].astype(BF16), w_out[l].astype(BF16)
        wfi, wfo = w_ffn_in[l].astype(BF16), w_ffn_out[l].astype(BF16)

        zp = _inproj(xp, norm_mix_g[l], mod_p[0], mod_p[1], w_proj, tm_p, tn)
        zi = _inproj_hp(xp, norm_mix_g[l], mod_p[0], mod_p[1], wi_hi, wi_lo, tm_p)
        outs['rows_n_p'].append(zp[..., C_K:C_K + 4 * HEAD_DIM].reshape(bp, t, 4, HEAD_DIM))
        outs['rows_d_p'].append(jnp.concatenate([zp[..., C_K + 3 * LANE:C_K + 4 * LANE], zi[..., ZI_K:ZI_K + HEAD_DIM]],
                                                axis=-1).reshape(bp, t, 3, HEAD_DIM))
        wn = min(NSA_WINDOW, t)
        outs['win_p'].append(zp[:, t - wn:, C_K + 2 * LANE:C_K + 3 * LANE].reshape(bp, wn, 2, HEAD_DIM))
        outs['shift_p'].append(zp[:, t - 1, C_RWKV:C_RWKV + 4 * RWKV_WIDTH])
        n16 = t // NSA_CMP_STRIDE
        r = jnp.stack([zp[..., C_K:C_K + HEAD_DIM].reshape(bp, n16, NSA_CMP_STRIDE * HEAD_DIM),
                       zp[..., C_K + HEAD_DIM:C_K + 2 * HEAD_DIM].reshape(bp, n16, NSA_CMP_STRIDE * HEAD_DIM)], axis=1)
        w1a, w1b, pos_a, pos_b, w2 = _compress_weights_rows(nsa_cmp_w1[l], nsa_cmp_w2[l], nsa_cmp_pos[l])
        cb, ct = _nsa_compress(r, pos_a, pos_b, w1a, w1b, w2)
        kb, kt = _kprep(zp)
        o_att = _prompt_attention(zp, zi, kb, kt, _kprep_idx(zi), cb, ct)
        o_c = _gmlp(zp, lp['gmlp_ln_g'], lp['gmlp_ln_b'], lp['gmlp_ws'], lp['gmlp_bs'], _pick(t, (512, 256, 128)))
        pre = _rwkv_pre(zp, jnp.zeros((bp, 1, 4 * RWKV_WIDTH), F32), lp, _pick(t, (256, 128)))
        o_r, wkv_bd = _rwkv_scan(*pre, lp['rwkv_ln_w'], lp['rwkv_ln_b'], jnp.zeros((bp, RWKV_WIDTH, RWKV_WIDTH), F32))
        outs['wkv_p'].append(jnp.stack([wkv_bd[:, h * HEAD_DIM:(h + 1) * HEAD_DIM, h * HEAD_DIM:(h + 1) * HEAD_DIM]
                                        for h in range(N_HEADS)], axis=1).swapaxes(-1, -2))
        xp = _merge(zp, o_att, o_c, o_r, wb, wo, xp, mod_p[2], tm_p)
        xp = _ffn(xp, norm_ffn_g[l], mod_p[3], mod_p[4], mod_p[5], wfi, wfo, final_norm_g, last, tm_f, tf)

        zs = _inproj(xs, norm_mix_g[l], mod_s[0], mod_s[1], w_proj, bs, tn)
        z2 = zs[0]
        zi2 = _inproj_hp(xs, norm_mix_g[l], mod_s[0], mod_s[1], wi_hi, wi_lo, bs)[0]
        k_idx_new = zi2[:, ZI_K:ZI_K + HEAD_DIM]
        outs['rows_n_s'].append(z2[:, C_K:C_K + 4 * HEAD_DIM].reshape(bs, 1, 4, HEAD_DIM))
        outs['rows_d_s'].append(jnp.concatenate([z2[:, C_K + 3 * LANE:C_K + 4 * LANE], k_idx_new],
                                                axis=-1).reshape(bs, 1, 3, HEAD_DIM))
        outs['shift_s'].append(z2[:, C_RWKV:C_RWKV + 4 * RWKV_WIDTH])
        idx = _sample_index(page_table, zi2.reshape(bs, 1, N_HP), cache_d, l)
        width = idx.shape[-1]
        dsel = _sample_topk(idx.reshape(bs, width), min(DSA_TOPK, (past + 1) // 4)).reshape(bs, 1, width)
        cmp_w = _compress_weights_pairs(nsa_cmp_w1[l], nsa_cmp_w2[l], nsa_cmp_pos[l])
        o_rows, win_new = _sample_attention(page_table, zs.reshape(bs, 1, N_PROJ), dsel, win_t, cmp_w, cache_n, cache_d, l)
        o_heads = jnp.concatenate([o_rows[:, 0:N_HEADS, 0:HEAD_DIM].reshape(bs, N_HEADS * HEAD_DIM),
                                   o_rows[:, 0:N_HEADS, HEAD_DIM:2 * HEAD_DIM].reshape(bs, N_HEADS * HEAD_DIM)], axis=1)
        outs['win_s'].append(jnp.transpose(win_new, (0, 3, 1, 2)))
        o_cs, v_cs, o_rs, wkv_s = _sample_mixers(z2, state_rwkv_shift[l], state_rwkv_wkv[l], lp, 8)
        outs['v_s'].append(v_cs.reshape(bs, 1, GMLP_WIDTH))
        outs['wkv_s'].append(wkv_s)
        xs = _merge(zs, o_heads[None], o_cs[None], o_rs[None], wb, wo, xs, mod_s[2], bs)
        xs = _ffn(xs, norm_ffn_g[l], mod_s[3], mod_s[4], mod_s[5], wfi, wfo, final_norm_g, last, bs, tf)

    st = lambda k: jnp.stack(outs[k])
    return (xp, xs.reshape(bs, 1, d), st('rows_n_p'), st('rows_n_s'), st('rows_d_p'), st('rows_d_s'), st('win_p'),
            st('win_s'), st('v_s'), st('shift_p'), st('shift_s'), st('wkv_p'), st('wkv_s'))
```
